```python
import math
import jax, jax.numpy as jnp
from jax import lax
import numpy as np

D_MODEL = 1024
BATCH = 4
SEQ = 4096
DEPTH = 2

GRID_W = 64
CTX_LEN = 256
N_MIXERS = 4
W_GROUP = D_MODEL // N_MIXERS
MIX_W = N_MIXERS * W_GROUP
HG_HEADS = 4
HG_DK = W_GROUP // HG_HEADS
HG_CHUNK = 16
ML_HEADS = 4
ML_DH = W_GROUP // ML_HEADS
ML_CHUNK = 64
HY_ORDER = 2
HY_BANDS = 8
HY_EMB = 2 * HY_BANDS + 1
HY_FFN = 64
HY_FAST_DECAY = 0.3
HY_SLOW_DECAY = 1.5
HY_TARGET = 1e-2
GD_HEADS = 4
GD_DH = W_GROUP // GD_HEADS
GD_CHUNK = 64
SHORT_CONV = 3
N_EXPERTS = 32
TOP_K = 4
D_EXPERT = D_MODEL
SWIGLU_LIMIT = 7.0
SWIGLU_ALPHA = 1.702
MOE_BLOCK = 256
EPS = 1e-6
NEG = -1e30
HG_COLS = 5 * W_GROUP
ML_COLS = 4 * W_GROUP + 4 * ML_HEADS
HY_COLS = 3 * W_GROUP
GD_COLS = 4 * W_GROUP + 4 * GD_HEADS
IN_W = HG_COLS + ML_COLS + HY_COLS + GD_COLS

kernel_name = "hybrid_parallel_groups_diffusion_trunk"

F32 = jnp.float32


def _rms(x, gain):
    xf = x.astype(F32)
    y = xf * lax.rsqrt(jnp.mean(xf * xf, -1, keepdims=True) + EPS)
    return (y * gain.astype(F32)).astype(x.dtype)


def _head_rms(o, gain, n_heads):
    b, l, w = o.shape
    oh = o.reshape(b, l, n_heads, w // n_heads)
    oh = oh * lax.rsqrt(jnp.mean(oh * oh, -1, keepdims=True) + EPS)
    return oh.reshape(b, l, w) * gain.astype(F32)


def _heads(t, n_heads):
    b, l, w = t.shape
    return t.reshape(b, l, n_heads, w // n_heads).transpose(0, 2, 1, 3)


def _unheads(t):
    b, h, l, d = t.shape
    return t.transpose(0, 2, 1, 3).reshape(b, l, h * d)


def _l2(t):
    return t * lax.rsqrt(jnp.sum(t * t, -1, keepdims=True) + EPS)


def _short_conv(x, w, b=None):
    k = w.shape[0]
    y = lax.conv_general_dilated(x, w[:, None, :].astype(x.dtype), window_strides=(1,),
                                 padding=[(k // 2, k - 1 - k // 2)],
                                 dimension_numbers=('NWC', 'WIO', 'NWC'),
                                 feature_group_count=x.shape[-1])
    return y if b is None else y + b.astype(x.dtype)


def _to_colmajor(t, rows):
    b, l, w = t.shape
    return t.reshape(b, rows, GRID_W, w).swapaxes(1, 2).reshape(b, l, w)


def _from_colmajor(t, rows):
    b, l, w = t.shape
    return t.reshape(b, GRID_W, rows, w).swapaxes(1, 2).reshape(b, l, w)


def _bidirectional(scan_fn, ctx_fwd, ctx_bwd, lat_fwd, lat_bwd, state0):
    flip = lambda ts: tuple(jnp.flip(t, 2) for t in ts)
    o_cf, s_cf = scan_fn(*ctx_fwd, state0)
    o_cb, s_cb = scan_fn(*flip(ctx_bwd), state0)
    o_lf, _ = scan_fn(*lat_fwd, s_cf)
    o_lb, _ = scan_fn(*flip(lat_bwd), s_cb)
    return o_lf + jnp.flip(o_lb, 2), o_cf + jnp.flip(o_cb, 2)


def _gla_chunked(q, k, v, log_f, state):
    b, h, l, dk = q.shape
    dv = v.shape[-1]
    c = HG_CHUNK
    n = l // c
    q, k, log_f = (t.reshape(b, h, n, c, dk) for t in (q, k, log_f))
    v = v.reshape(b, h, n, c, dv)
    cum = jnp.cumsum(log_f, axis=3)
    causal = jnp.tril(jnp.ones((c, c), bool))[:, :, None]
    decay = jnp.exp(jnp.where(causal, cum[..., :, None, :] - cum[..., None, :, :], NEG))
    scores = jnp.einsum('bhntd,bhnsd,bhntsd->bhnts', q, k, decay)
    o = jnp.einsum('bhnts,bhnsv->bhntv', scores, v)
    last = cum[..., -1, :]
    u = jnp.einsum('bhnsd,bhnsv->bhndv', k * jnp.exp(last[..., None, :] - cum), v)

    def step(s, inp):
        a, uc = inp
        return a[..., None] * s + uc, s

    s_fin, s_start = lax.scan(step, state, (jnp.moveaxis(jnp.exp(last), 2, 0), jnp.moveaxis(u, 2, 0)))
    o = o + jnp.einsum('bhntd,bhndv->bhntv', q * jnp.exp(cum), jnp.moveaxis(s_start, 0, 2))
    return o.reshape(b, h, l, dv), s_fin


def _hgrn2(z, zc, lb, norm_g, need_ctx):
    def prep(t):
        q, i, ff, fb, g = jnp.split(t.astype(F32), 5, -1)
        out = [_heads(jax.nn.silu(q), HG_HEADS), _heads(i, HG_HEADS)]
        for zf in (ff, fb):
            f = lb + (1.0 - lb) * jax.nn.sigmoid(zf)
            key = (1.0 - lb) * jax.nn.sigmoid(-zf)
            out += [_heads(key, HG_HEADS), _heads(jnp.log(f), HG_HEADS)]
        return out, g
    (q, i, kf, lff, kb, lfb), g = prep(z)
    (qc, ic, kcf, lcf, kcb, lcb), gc = prep(zc)
    s0 = jnp.zeros((z.shape[0], HG_HEADS, HG_DK, HG_DK), F32)
    o, o_c = _bidirectional(_gla_chunked, (qc, kcf, ic, lcf), (qc, kcb, ic, lcb),
                            (q, kf, i, lff), (q, kb, i, lfb), s0)
    out = _head_rms(_unheads(o), norm_g, HG_HEADS) * jax.nn.silu(g)
    out_c = _head_rms(_unheads(o_c), norm_g, HG_HEADS) * jax.nn.silu(gc) if need_ctx else None
    return out, out_c


def _mlstm_chunked(q, k, v, log_i, log_f, state):
    b, h, l, d = q.shape
    c = ML_CHUNK
    n = l // c
    k = k * (d ** -0.5)
    q, k, v = (t.reshape(b, h, n, c, d) for t in (q, k, v))
    log_i, log_f = (t.reshape(b, h, n, c) for t in (log_i, log_f))
    fcum = jnp.cumsum(log_f, -1)
    causal = jnp.tril(jnp.ones((c, c), bool))
    a = jnp.where(causal, fcum[..., :, None] - fcum[..., None, :] + log_i[..., None, :], NEG)
    g = fcum[..., -1:] - fcum + log_i
    g_max = jnp.max(g, -1)
    kw = k * jnp.exp(g - g_max[..., None])[..., None]
    u_c = jnp.einsum('bhnsd,bhnse->bhnde', kw, v)
    u_n = jnp.sum(kw, 3)

    def step(carry, inp):
        c_mat, n_vec, m = carry
        f_last, gm, uc, un = inp
        m_new = jnp.maximum(f_last + m, gm)
        a_old = jnp.exp(f_last + m - m_new)
        a_new = jnp.exp(gm - m_new)
        new = (a_old[..., None, None] * c_mat + a_new[..., None, None] * uc,
               a_old[..., None] * n_vec + a_new[..., None] * un, m_new)
        return new, carry

    xs = tuple(jnp.moveaxis(t, 2, 0) for t in (fcum[..., -1], g_max, u_c, u_n))
    final, starts = lax.scan(step, state, xs)
    c0, n0, m0 = (jnp.moveaxis(t, 0, 2) for t in starts)
    a_init = fcum + m0[..., None]
    m_t = jnp.maximum(jnp.max(a, -1), a_init)
    p = jnp.exp(a - m_t[..., None]) * jnp.einsum('bhntd,bhnsd->bhnts', q, k)
    e0 = jnp.exp(a_init - m_t)
    num = jnp.einsum('bhnts,bhnse->bhnte', p, v) + e0[..., None] * jnp.einsum('bhntd,bhnde->bhnte', q, c0)
    den = jnp.sum(p, -1) + e0 * jnp.einsum('bhntd,bhnd->bhnt', q, n0)
    hout = num / jnp.maximum(jnp.abs(den), jnp.exp(-m_t))[..., None]
    return hout.reshape(b, h, l, d), final


def _mlstm(z, zc, gate_b, norm_g, rows, need_ctx):
    z = _to_colmajor(z, rows)
    def prep(t):
        t = t.astype(F32)
        q, k, v, o = jnp.split(t[..., :4 * W_GROUP], 4, -1)
        gates = t[..., 4 * W_GROUP:].reshape(t.shape[0], t.shape[1], 4, ML_HEADS) + gate_b.astype(F32)
        gates = gates.transpose(2, 0, 3, 1)
        qh, kh, vh = (_heads(a, ML_HEADS) for a in (q, k, v))
        fwd = (qh, kh, vh, gates[0], jax.nn.log_sigmoid(gates[1]))
        bwd = (qh, kh, vh, gates[2], jax.nn.log_sigmoid(gates[3]))
        return fwd, bwd, o
    lat_f, lat_b, o = prep(z)
    ctx_f, ctx_b, oc = prep(zc)
    bsz = z.shape[0]
    s0 = (jnp.zeros((bsz, ML_HEADS, ML_DH, ML_DH), F32), jnp.zeros((bsz, ML_HEADS, ML_DH), F32),
          jnp.zeros((bsz, ML_HEADS), F32))
    h, h_c = _bidirectional(_mlstm_chunked, ctx_f, ctx_b, lat_f, lat_b, s0)
    out = _from_colmajor(jax.nn.sigmoid(o) * _head_rms(_unheads(h), norm_g, ML_HEADS), rows)
    out_c = jax.nn.sigmoid(oc) * _head_rms(_unheads(h_c), norm_g, ML_HEADS) if need_ctx else None
    return out, out_c


def _hyena_filters(length, w1, b1, w2, b2, freq, w3):
    t = jnp.linspace(0.0, 1.0, length, dtype=F32)[:, None]
    pos = jnp.arange(length, dtype=F32)[:, None]
    band = jnp.linspace(1e-4, HY_BANDS - 1, HY_BANDS, dtype=F32)[None, :]
    ang = 2.0 * math.pi * pos * band / length
    feats = jnp.concatenate([t, jnp.cos(ang), -jnp.sin(ang)], -1)
    freq = freq.astype(F32)
    hdn = jnp.sin(freq[0] * (feats @ w1.astype(F32) + b1.astype(F32)))
    hdn = jnp.sin(freq[1] * (hdn @ w2.astype(F32) + b2.astype(F32)))
    filt = (hdn @ w3.astype(F32)).reshape(length, 2, HY_ORDER, W_GROUP)
    deltas = jnp.abs(jnp.linspace(math.log(HY_TARGET) / HY_SLOW_DECAY,
                                  math.log(HY_TARGET) / HY_FAST_DECAY, W_GROUP, dtype=F32))
    filt = filt * jnp.exp(-t * deltas)[:, None, None, :]
    fwd, bwd = filt[:, 0], filt[:, 1]
    two_sided = jnp.concatenate([fwd, jnp.zeros_like(fwd[:1]), jnp.flip(bwd[1:], 0)], 0)
    two_sided = two_sided * lax.rsqrt(jnp.sum(two_sided * two_sided, 0, keepdims=True) + EPS)
    return jnp.fft.rfft(two_sided, axis=0)


def _hyena(z, conv_w, conv_b, filt_params, bias):
    length = z.shape[1]
    u = _short_conv(z.astype(F32), conv_w.astype(F32), conv_b.astype(F32))
    v, x1, x2 = jnp.split(u, 3, -1)
    kf = _hyena_filters(length, *filt_params)
    bias = bias.astype(F32)
    y = v
    for n, gate in enumerate((x1, x2)):
        conv = jnp.fft.irfft(jnp.fft.rfft(y, n=2 * length, axis=1) * kf[:, n], n=2 * length, axis=1)[:, :length]
        y = gate * (conv + bias[n] * y)
    return y


def _gated_delta_chunked(q, k, v, log_a, beta, state):
    b, h, l, dk = q.shape
    dv = v.shape[-1]
    c = GD_CHUNK
    n = l // c
    q = q * dk ** -0.5
    q, k = (t.reshape(b, h, n, c, dk) for t in (q, k))
    v = v.reshape(b, h, n, c, dv)
    log_a, beta = (t.reshape(b, h, n, c) for t in (log_a, beta))
    g = jnp.cumsum(log_a, -1)
    causal = jnp.tril(jnp.ones((c, c), bool))
    strict = jnp.tril(jnp.ones((c, c), bool), -1)
    decay = jnp.exp(jnp.where(causal, g[..., :, None] - g[..., None, :], NEG))
    kb = k * beta[..., None]
    m = jnp.eye(c, dtype=q.dtype) + jnp.where(strict, jnp.einsum('bhnid,bhnjd->bhnij', kb, k) * decay, 0.0)
    rhs = jnp.concatenate([v * beta[..., None], kb * jnp.exp(g)[..., None]], -1)
    sol = lax.linalg.triangular_solve(m, rhs, left_side=True, lower=True, unit_diagonal=True)
    u, w = sol[..., :dv], sol[..., dv:]
    qk = jnp.einsum('bhnid,bhnjd->bhnij', q, k) * decay
    q_dec = q * jnp.exp(g)[..., None]
    g_last = g[..., -1]
    k_dec = k * jnp.exp(g_last[..., None] - g)[..., None]

    def step(s, inp):
        u_c, w_c, qk_c, qd_c, kd_c, gl_c = inp
        v_new = u_c - jnp.einsum('bhcd,bhde->bhce', w_c, s)
        o_c = jnp.einsum('bhcd,bhde->bhce', qd_c, s) + jnp.einsum('bhij,bhje->bhie', qk_c, v_new)
        s = jnp.exp(gl_c)[..., None, None] * s + jnp.einsum('bhcd,bhce->bhde', kd_c, v_new)
        return s, o_c

    xs = tuple(jnp.moveaxis(t, 2, 0) for t in (u, w, qk, q_dec, k_dec, g_last))
    s_fin, o = lax.scan(step, state, xs)
    return jnp.moveaxis(o, 0, 2).reshape(b, h, l, dv), s_fin


def _gdn(z, zc, conv_w, a_log, dt_bias, norm_g, need_ctx):
    a_log = a_log.astype(F32)
    dt_bias = dt_bias.astype(F32)
    def prep(t):
        t = t.astype(F32)
        qkv = jax.nn.silu(_short_conv(t[..., :3 * W_GROUP], conv_w.astype(F32)))
        q, k, v = jnp.split(qkv, 3, -1)
        q, k, v = _l2(_heads(q, GD_HEADS)), _l2(_heads(k, GD_HEADS)), _heads(v, GD_HEADS)
        gate = t[..., 3 * W_GROUP:4 * W_GROUP]
        ab = t[..., 4 * W_GROUP:].reshape(t.shape[0], t.shape[1], 4, GD_HEADS).transpose(2, 0, 3, 1)
        dirs = []
        for d in range(2):
            log_a = -jnp.exp(a_log[d])[:, None] * jax.nn.softplus(ab[2 * d] + dt_bias[d][:, None])
            dirs.append((q, k, v, log_a, jax.nn.sigmoid(ab[2 * d + 1])))
        return dirs[0], dirs[1], gate
    lat_f, lat_b, gate = prep(z)
    ctx_f, ctx_b, gate_c = prep(zc)
    s0 = jnp.zeros((z.shape[0], GD_HEADS, GD_DH, GD_DH), F32)
    o, o_c = _bidirectional(_gated_delta_chunked, ctx_f, ctx_b, lat_f, lat_b, s0)
    out = _head_rms(_unheads(o), norm_g, GD_HEADS) * jax.nn.silu(gate)
    out_c = _head_rms(_unheads(o_c), norm_g, GD_HEADS) * jax.nn.silu(gate_c) if need_ctx else None
    return out, out_c


def _moe(h, router_w, router_b, w1, b1, w2, b2):
    t_count, d = h.shape
    logits = (h @ router_w + router_b).astype(F32)
    top_v, top_i = lax.top_k(logits, TOP_K)
    gates = jax.nn.softmax(top_v, -1)
    n_pairs = t_count * TOP_K
    e_flat = top_i.reshape(-1)
    tok_flat = jnp.repeat(jnp.arange(t_count, dtype=jnp.int32), TOP_K)
    g_flat = gates.reshape(-1)
    order = jnp.argsort(e_flat)
    e_sorted = e_flat[order]
    counts = jnp.bincount(e_flat, length=N_EXPERTS)
    padded = (counts + MOE_BLOCK - 1) // MOE_BLOCK * MOE_BLOCK
    start = jnp.cumsum(counts) - counts
    pend = jnp.cumsum(padded)
    pstart = pend - padded
    dest = pstart[e_sorted] + (jnp.arange(n_pairs, dtype=jnp.int32) - start[e_sorted])
    n_blocks = -(-(n_pairs + N_EXPERTS * (MOE_BLOCK - 1)) // MOE_BLOCK)
    n_slots = n_blocks * MOE_BLOCK
    slot_tok = jnp.full((n_slots,), t_count, jnp.int32).at[dest].set(tok_flat[order])
    slot_gate = jnp.zeros((n_slots,), F32).at[dest].set(g_flat[order])
    block_exp = jnp.clip(jnp.searchsorted(pend, jnp.arange(n_blocks) * MOE_BLOCK, side='right'),
                         0, N_EXPERTS - 1)
    h_pad = jnp.concatenate([h, jnp.zeros((1, d), h.dtype)], 0)

    def expert_block(args):
        toks, e = args
        u = h_pad[toks] @ w1[e] + b1[e]
        gate, lin = jnp.split(u, 2, -1)
        gate = jnp.minimum(gate, SWIGLU_LIMIT)
        lin = jnp.clip(lin, -SWIGLU_LIMIT, SWIGLU_LIMIT)
        y = (lin + 1.0) * gate * jax.nn.sigmoid(SWIGLU_ALPHA * gate)
        return y @ w2[e] + b2[e]

    y = lax.map(expert_block, (slot_tok.reshape(n_blocks, MOE_BLOCK), block_exp))
    y = y.reshape(n_slots, d) * slot_gate[:, None].astype(y.dtype)
    return jnp.zeros((t_count + 1, d), y.dtype).at[slot_tok].add(y)[:t_count]


def _layer(x, xc, rows, last, mod, mod_c, n1, n2, w_in, w_out, hg, ml, hy, gd, moe):
    need_ctx = not last
    sh1, sc1, g1, sh2, sc2, g2 = jnp.split(mod[:, None, :], 6, -1)
    csh1, csc1, cg1, csh2, csc2, cg2 = jnp.split(mod_c, 6, -1)
    h = _rms(x, n1) * (1 + sc1) + sh1
    hc = _rms(xc, n1) * (1 + csc1) + csh1
    splits = [HG_COLS, HG_COLS + ML_COLS, HG_COLS + ML_COLS + HY_COLS]
    z_hg, z_ml, z_hy, z_gd = jnp.split(h @ w_in, splits, -1)
    zc_hg, zc_ml, zc_hy, zc_gd = jnp.split(hc @ w_in, splits, -1)
    a_lat, a_ctx = _hgrn2(z_hg, zc_hg, hg[0], hg[1], need_ctx)
    b_lat, b_ctx = _mlstm(z_ml, zc_ml, ml[0], ml[1], rows, need_ctx)
    c_lat = _hyena(z_hy, hy[0], hy[1], hy[2], hy[3])
    d_lat, d_ctx = _gdn(z_gd, zc_gd, gd[0], gd[1], gd[2], gd[3], need_ctx)
    mix = jnp.concatenate([a_lat, b_lat, c_lat, d_lat], -1).astype(x.dtype)
    x = x + g1 * (mix @ w_out)
    if need_ctx:
        c_ctx_out = _hyena(zc_hy, hy[0], hy[1], hy[2], hy[3])
        mix_c = jnp.concatenate([a_ctx, b_ctx, c_ctx_out, d_ctx], -1).astype(xc.dtype)
        xc = xc + cg1 * (mix_c @ w_out)
    h = _rms(x, n2) * (1 + sc2) + sh2
    if last:
        y = _moe(h.reshape(-1, D_MODEL), *moe).reshape(x.shape)
        return x + g2 * y, xc
    hc = _rms(xc, n2) * (1 + csc2) + csh2
    n_lat = x.shape[0] * x.shape[1]
    y = _moe(jnp.concatenate([h.reshape(-1, D_MODEL), hc.reshape(-1, D_MODEL)], 0), *moe)
    x = x + g2 * y[:n_lat].reshape(x.shape)
    xc = xc + cg2 * y[n_lat:].reshape(xc.shape)
    return x, xc


def setup_inputs(seed: int = 0) -> dict:
    key = jax.random.key(seed)
    ks = iter(jax.random.split(key, 48))
    nrm = lambda shape, s: jax.random.normal(next(ks), shape, F32) * s
    dt = jnp.exp(jax.random.uniform(next(ks), (DEPTH, 2, GD_HEADS), F32, math.log(1e-3), math.log(1e-1)))
    gate_base = jnp.array([0.0, 3.0, 0.0, 3.0], F32)[None, :, None]
    return {
        "x": nrm((BATCH, SEQ, D_MODEL), 1.0),
        "c": nrm((BATCH, D_MODEL), 1.0),
        "ctx": nrm((BATCH, CTX_LEN, D_MODEL), 1.0),
        "c_ctx": nrm((D_MODEL,), 1.0),
        "mod_w": nrm((DEPTH, D_MODEL, 6 * D_MODEL), 0.5 * D_MODEL ** -0.5),
        "mod_b": nrm((DEPTH, 6 * D_MODEL), 0.01),
        "norm1_g": 1.0 + nrm((DEPTH, D_MODEL), 0.02),
        "norm2_g": 1.0 + nrm((DEPTH, D_MODEL), 0.02),
        "w_in": nrm((DEPTH, D_MODEL, IN_W), D_MODEL ** -0.5),
        "w_out": nrm((DEPTH, MIX_W, D_MODEL), MIX_W ** -0.5),
        "hg_lb_logits": nrm((DEPTH, W_GROUP), 0.5),
        "hg_norm_g": 1.0 + nrm((DEPTH, W_GROUP), 0.02),
        "ml_gate_b": gate_base + nrm((DEPTH, 4, ML_HEADS), 0.3),
        "ml_norm_g": 1.0 + nrm((DEPTH, W_GROUP), 0.02),
        "hy_conv_w": nrm((DEPTH, SHORT_CONV, HY_COLS), SHORT_CONV ** -0.5),
        "hy_conv_b": nrm((DEPTH, HY_COLS), 0.01),
        "hy_w1": nrm((DEPTH, HY_EMB, HY_FFN), HY_EMB ** -0.5),
        "hy_b1": nrm((DEPTH, HY_FFN), 0.1),
        "hy_w2": nrm((DEPTH, HY_FFN, HY_FFN), HY_FFN ** -0.5),
        "hy_b2": nrm((DEPTH, HY_FFN), 0.1),
        "hy_freq": 1.0 + nrm((DEPTH, 2, HY_FFN), 0.1),
        "hy_w3": nrm((DEPTH, HY_FFN, 2 * HY_ORDER * W_GROUP), HY_FFN ** -0.5),
        "hy_bias": nrm((DEPTH, HY_ORDER, W_GROUP), 0.5),
        "gd_conv_w": nrm((DEPTH, SHORT_CONV, 3 * W_GROUP), SHORT_CONV ** -0.5),
        "gd_a_log": jnp.log(jax.random.uniform(next(ks), (DEPTH, 2, GD_HEADS), F32, 1.0, 16.0)),
        "gd_dt_bias": dt + jnp.log(-jnp.expm1(-dt)),
        "gd_norm_g": 1.0 + nrm((DEPTH, W_GROUP), 0.02),
        "router_w": nrm((DEPTH, D_MODEL, N_EXPERTS), D_MODEL ** -0.5),
        "router_b": nrm((DEPTH, N_EXPERTS), 0.01),
        "exp_w1": nrm((DEPTH, N_EXPERTS, D_MODEL, 2 * D_EXPERT), D_MODEL ** -0.5),
        "exp_b1": nrm((DEPTH, N_EXPERTS, 2 * D_EXPERT), 0.01),
        "exp_w2": nrm((DEPTH, N_EXPERTS, D_EXPERT, D_MODEL), D_EXPERT ** -0.5),
        "exp_b2": nrm((DEPTH, N_EXPERTS, D_MODEL), 0.01),
        "final_g": 1.0 + nrm((D_MODEL,), 0.02),
    }


def reference(x, c, ctx, c_ctx, mod_w, mod_b, norm1_g, norm2_g, w_in, w_out, hg_lb_logits, hg_norm_g,
              ml_gate_b, ml_norm_g, hy_conv_w, hy_conv_b, hy_w1, hy_b1, hy_w2, hy_b2, hy_freq, hy_w3,
              hy_bias, gd_conv_w, gd_a_log, gd_dt_bias, gd_norm_g, router_w, router_b, exp_w1, exp_b1,
              exp_w2, exp_b2, final_g):
    rows = x.shape[1] // GRID_W
    p = jax.nn.softmax(hg_lb_logits.astype(F32), axis=0)
    lower_bounds = jnp.cumsum(p, 0) - p[0]
    s_c = jax.nn.silu(c)
    s_cc = jax.nn.silu(c_ctx)
    xc = ctx
    for l in range(DEPTH):
        mod = s_c @ mod_w[l] + mod_b[l]
        mod_c = s_cc @ mod_w[l] + mod_b[l]
        x, xc = _layer(
            x, xc, rows, l == DEPTH - 1, mod, mod_c, norm1_g[l], norm2_g[l], w_in[l], w_out[l],
            (lower_bounds[l], hg_norm_g[l]),
            (ml_gate_b[l], ml_norm_g[l]),
            (hy_conv_w[l], hy_conv_b[l], (hy_w1[l], hy_b1[l], hy_w2[l], hy_b2[l], hy_freq[l], hy_w3[l]), hy_bias[l]),
            (gd_conv_w[l], gd_a_log[l], gd_dt_bias[l], gd_norm_g[l]),
            (router_w[l], router_b[l], exp_w1[l], exp_b1[l], exp_w2[l], exp_b2[l]))
    return _rms(x, final_g)
```

```python
import functools
import math

import jax
import jax.numpy as jnp
from jax import lax
from jax.experimental import pallas as pl
from jax.experimental.pallas import tpu as pltpu

F32 = jnp.float32
BF16 = jnp.bfloat16

D_MODEL = 1024
GRID_W = 64
N_MIXERS = 4
W_GROUP = D_MODEL // N_MIXERS
HG_HEADS = 4
HG_DK = W_GROUP // HG_HEADS
HG_CHUNK = 16
ML_HEADS = 4
ML_DH = W_GROUP // ML_HEADS
ML_CHUNK = 64
HY_ORDER = 2
HY_BANDS = 8
HY_FAST_DECAY = 0.3
HY_SLOW_DECAY = 1.5
HY_TARGET = 1e-2
GD_HEADS = 4
GD_DH = W_GROUP // GD_HEADS
GD_CHUNK = 64
N_EXPERTS = 32
TOP_K = 4
SWIGLU_LIMIT = 7.0
SWIGLU_ALPHA = 1.702
EPS = 1e-6
NEG = -1e30
HG_COLS = 5 * W_GROUP
ML_MAIN = 4 * W_GROUP
ML_GATES = 4 * ML_HEADS
HY_COLS = 3 * W_GROUP
GD_MAIN = 4 * W_GROUP
GD_GATES = 4 * GD_HEADS
GATE_PAD = 128

ROW_TILE = 256
MOE_TILE = 256
VMEM_LIMIT = 56 * 1024 * 1024


def _norm_mod(x, gain, sc, sh):
    y = x * lax.rsqrt(jnp.mean(x * x, -1, keepdims=True) + EPS)
    return y * gain * (1.0 + sc) + sh


def _in_proj_kernel(x_ref, g_ref, sc_ref, sh_ref, w_ref, *out_refs, splits):
    h = _norm_mod(x_ref[...], g_ref[...], sc_ref[...], sh_ref[...]).astype(BF16)
    off = 0
    for o_ref, n in zip(out_refs, splits):
        o_ref[...] = jnp.dot(h, w_ref[:, off:off + n], preferred_element_type=F32)
        off += n


def _mod_index_map(n_batch):
    return lambda b, j: (jnp.where(j == 0, n_batch, b), 0, 0)


def _in_proj(xs, gain, sc_tab, sh_tab, w, splits):
    n_batch, seq, d = xs.shape
    n = w.shape[1]
    grid = (n_batch, seq // ROW_TILE)
    mod_spec = pl.BlockSpec((None, 1, d), _mod_index_map(n_batch))
    return pl.pallas_call(
        functools.partial(_in_proj_kernel, splits=splits),
        grid=grid,
        in_specs=[
            pl.BlockSpec((None, ROW_TILE, d), lambda b, j: (b, j, 0)),
            pl.BlockSpec((1, d), lambda b, j: (0, 0)),
            mod_spec, mod_spec,
            pl.BlockSpec((d, n), lambda b, j: (0, 0)),
        ],
        out_specs=[pl.BlockSpec((None, ROW_TILE, s), lambda b, j: (b, j, 0)) for s in splits],
        out_shape=[jax.ShapeDtypeStruct((n_batch, seq, s), F32) for s in splits],
        compiler_params=pltpu.CompilerParams(
            dimension_semantics=("parallel", "parallel"), vmem_limit_bytes=VMEM_LIMIT),
        name="in_proj",
    )(xs, gain.reshape(1, d), sc_tab, sh_tab, w)


def _moe_pre_kernel(x_ref, g_ref, sc_ref, sh_ref, rw_ref, rb_ref, h_ref, lg_ref):
    h = _norm_mod(x_ref[...], g_ref[...], sc_ref[...], sh_ref[...])
    h_ref[...] = h.astype(BF16)
    lg_ref[...] = jnp.dot(h, rw_ref[...], preferred_element_type=F32,
                          precision=lax.Precision.HIGHEST) + rb_ref[...]


def _moe_pre(xs, gain, sc_tab, sh_tab, router_w, router_b):
    n_batch, seq, d = xs.shape
    pad = GATE_PAD - N_EXPERTS
    rw = jnp.pad(router_w.astype(F32), ((0, 0), (0, pad)))
    rb = jnp.pad(router_b.astype(F32), (0, pad), constant_values=NEG).reshape(1, GATE_PAD)
    mod_spec = pl.BlockSpec((None, 1, d), _mod_index_map(n_batch))
    return pl.pallas_call(
        _moe_pre_kernel,
        grid=(n_batch, seq // ROW_TILE),
        in_specs=[
            pl.BlockSpec((None, ROW_TILE, d), lambda b, j: (b, j, 0)),
            pl.BlockSpec((1, d), lambda b, j: (0, 0)),
            mod_spec, mod_spec,
            pl.BlockSpec((d, GATE_PAD), lambda b, j: (0, 0)),
            pl.BlockSpec((1, GATE_PAD), lambda b, j: (0, 0)),
        ],
        out_specs=[pl.BlockSpec((None, ROW_TILE, d), lambda b, j: (b, j, 0)),
                   pl.BlockSpec((None, ROW_TILE, GATE_PAD), lambda b, j: (b, j, 0))],
        out_shape=[jax.ShapeDtypeStruct((n_batch, seq, d), BF16),
                   jax.ShapeDtypeStruct((n_batch, seq, GATE_PAD), F32)],
        compiler_params=pltpu.CompilerParams(
            dimension_semantics=("parallel", "parallel"), vmem_limit_bytes=VMEM_LIMIT),
        name="moe_pre",
    )(xs, gain.reshape(1, d), sc_tab, sh_tab, rw, rb)


def _out_proj_kernel(x_ref, a_ref, b_ref, c_ref, d_ref, g_ref, w_ref, o_ref):
    acc = jnp.zeros(o_ref.shape, F32)
    for i, m_ref in enumerate((a_ref, b_ref, c_ref, d_ref)):
        acc += jnp.dot(m_ref[...].astype(BF16), w_ref[i * W_GROUP:(i + 1) * W_GROUP, :],
                       preferred_element_type=F32)
    o_ref[...] = x_ref[...] + g_ref[...] * acc


def _out_proj(xs, mixes, gate_tab, w_out):
    n_batch, seq, d = xs.shape
    row = lambda w: pl.BlockSpec((None, ROW_TILE, w), lambda b, j: (b, j, 0))
    return pl.pallas_call(
        _out_proj_kernel,
        grid=(n_batch, seq // ROW_TILE),
        in_specs=[row(d)] + [row(W_GROUP)] * N_MIXERS + [
            pl.BlockSpec((None, 1, d), _mod_index_map(n_batch)),
            pl.BlockSpec((d, d), lambda b, j: (0, 0)),
        ],
        out_specs=row(d),
        out_shape=jax.ShapeDtypeStruct((n_batch, seq, d), F32),
        compiler_params=pltpu.CompilerParams(
            dimension_semantics=("parallel", "parallel"), vmem_limit_bytes=VMEM_LIMIT),
        name="out_proj",
    )(xs, *mixes, gate_tab, w_out)


def _expert_kernel(bexp_ref, x_ref, gate_ref, w1_ref, b1_ref, w2_ref, b2_ref, o_ref, w1b, w2b):
    i = pl.program_id(0)
    changed = jnp.logical_or(i == 0, bexp_ref[i] != bexp_ref[jnp.maximum(i - 1, 0)])

    @pl.when(changed)
    def _():
        w1b[...] = w1_ref[...].astype(BF16)
        w2b[...] = w2_ref[...].astype(BF16)

    f = w2b.shape[0]
    u = jnp.dot(x_ref[...], w1b[...], preferred_element_type=F32) + b1_ref[...]
    gate = jnp.minimum(u[:, :f], SWIGLU_LIMIT)
    lin = jnp.clip(u[:, f:], -SWIGLU_LIMIT, SWIGLU_LIMIT)
    y = (lin + 1.0) * gate * jax.nn.sigmoid(SWIGLU_ALPHA * gate)
    out = jnp.dot(y.astype(BF16), w2b[...], preferred_element_type=F32) + b2_ref[...]
    o_ref[...] = out * gate_ref[...]


def _expert_ffn(block_exp, xg, slot_gate, w1, b1, w2, b2):
    n_slots, d = xg.shape
    n_exp, _, f2 = w1.shape
    f = f2 // 2
    n_blocks = n_slots // MOE_TILE
    return pl.pallas_call(
        _expert_kernel,
        grid_spec=pltpu.PrefetchScalarGridSpec(
            num_scalar_prefetch=1,
            grid=(n_blocks,),
            in_specs=[
                pl.BlockSpec((MOE_TILE, d), lambda i, be: (i, 0)),
                pl.BlockSpec((MOE_TILE, 1), lambda i, be: (i, 0)),
                pl.BlockSpec((None, d, f2), lambda i, be: (be[i], 0, 0)),
                pl.BlockSpec((None, 1, f2), lambda i, be: (be[i], 0, 0)),
                pl.BlockSpec((None, f, d), lambda i, be: (be[i], 0, 0)),
                pl.BlockSpec((None, 1, d), lambda i, be: (be[i], 0, 0)),
            ],
            out_specs=pl.BlockSpec((MOE_TILE, d), lambda i, be: (i, 0)),
            scratch_shapes=[pltpu.VMEM((d, f2), BF16), pltpu.VMEM((f, d), BF16)],
        ),
        out_shape=jax.ShapeDtypeStruct((n_slots, d), F32),
        compiler_params=pltpu.CompilerParams(
            dimension_semantics=("arbitrary",), vmem_limit_bytes=VMEM_LIMIT),
        name="expert_ffn",
    )(block_exp, xg, slot_gate, w1, b1.reshape(n_exp, 1, f2), w2, b2.reshape(n_exp, 1, d))


def _moe(h, logits, w1, b1, w2, b2):
    t_count, d = h.shape
    top_v, top_i = lax.top_k(logits, TOP_K)
    gates = jax.nn.softmax(top_v, -1)
    n_pairs = t_count * TOP_K
    e_flat = top_i.reshape(-1)
    order = jnp.argsort(e_flat)
    e_sorted = e_flat[order]
    counts = jnp.bincount(e_flat, length=N_EXPERTS)
    padded = (counts + MOE_TILE - 1) // MOE_TILE * MOE_TILE
    start = jnp.cumsum(counts) - counts
    pend = jnp.cumsum(padded)
    pstart = pend - padded
    dest = pstart[e_sorted] + (jnp.arange(n_pairs, dtype=jnp.int32) - start[e_sorted])
    n_blocks = -(-(n_pairs + N_EXPERTS * (MOE_TILE - 1)) // MOE_TILE)
    n_slots = n_blocks * MOE_TILE
    pos = jnp.zeros((n_pairs,), jnp.int32).at[order].set(dest.astype(jnp.int32))
    slot_tok = jnp.zeros((n_slots,), jnp.int32).at[pos].set(
        jnp.repeat(jnp.arange(t_count, dtype=jnp.int32), TOP_K))
    slot_gate = jnp.zeros((n_slots,), F32).at[pos].set(gates.reshape(-1))
    block_exp = jnp.clip(jnp.searchsorted(pend, jnp.arange(n_blocks) * MOE_TILE, side='right'),
                         0, N_EXPERTS - 1).astype(jnp.int32)
    y = _expert_ffn(block_exp, h[slot_tok], slot_gate[:, None], w1, b1, w2, b2)
    return jnp.sum(y[pos.reshape(t_count, TOP_K)], 1)


def _final_norm_kernel(x_ref, g_ref, o_ref):
    x = x_ref[...]
    o_ref[...] = x * lax.rsqrt(jnp.mean(x * x, -1, keepdims=True) + EPS) * g_ref[...]


def _final_norm(xs, gain, ctx_len):
    n_batch, seq, d = xs.shape
    skip = ctx_len // ROW_TILE
    return pl.pallas_call(
        _final_norm_kernel,
        grid=(n_batch, (seq - ctx_len) // ROW_TILE),
        in_specs=[pl.BlockSpec((None, ROW_TILE, d), lambda b, j: (b, j + skip, 0)),
                  pl.BlockSpec((1, d), lambda b, j: (0, 0))],
        out_specs=pl.BlockSpec((None, ROW_TILE, d), lambda b, j: (b, j, 0)),
        out_shape=jax.ShapeDtypeStruct((n_batch, seq - ctx_len, d), F32),
        compiler_params=pltpu.CompilerParams(dimension_semantics=("parallel", "parallel")),
        name="final_norm",
    )(xs, gain.reshape(1, d))


def _head_rms(o, gain, n_heads):
    b, l, w = o.shape
    oh = o.reshape(b, l, n_heads, w // n_heads)
    oh = oh * lax.rsqrt(jnp.mean(oh * oh, -1, keepdims=True) + EPS)
    return oh.reshape(b, l, w) * gain.astype(F32)


def _heads(t, n_heads):
    b, l, w = t.shape
    return t.reshape(b, l, n_heads, w // n_heads).transpose(0, 2, 1, 3)


def _unheads(t):
    b, h, l, d = t.shape
    return t.transpose(0, 2, 1, 3).reshape(b, l, h * d)


def _l2(t):
    return t * lax.rsqrt(jnp.sum(t * t, -1, keepdims=True) + EPS)


def _short_conv(x, w, b=None):
    k = w.shape[0]
    y = lax.conv_general_dilated(x, w[:, None, :].astype(x.dtype), window_strides=(1,),
                                 padding=[(k // 2, k - 1 - k // 2)],
                                 dimension_numbers=('NWC', 'WIO', 'NWC'),
                                 feature_group_count=x.shape[-1])
    return y if b is None else y + b.astype(x.dtype)


def _to_colmajor(t, rows):
    b, l, w = t.shape
    return t.reshape(b, rows, GRID_W, w).swapaxes(1, 2).reshape(b, l, w)


def _from_colmajor(t, rows):
    b, l, w = t.shape
    return t.reshape(b, GRID_W, rows, w).swapaxes(1, 2).reshape(b, l, w)


def _bidirectional(scan_fn, ctx_fwd, ctx_bwd, lat_fwd, lat_bwd, state0):
    flip = lambda ts: tuple(jnp.flip(t, 2) for t in ts)
    o_cf, s_cf = scan_fn(*ctx_fwd, state0)
    o_cb, s_cb = scan_fn(*flip(ctx_bwd), state0)
    o_lf, _ = scan_fn(*lat_fwd, s_cf)
    o_lb, _ = scan_fn(*flip(lat_bwd), s_cb)
    return o_lf + jnp.flip(o_lb, 2), o_cf + jnp.flip(o_cb, 2)


def _gla_chunked(q, k, v, log_f, state):
    b, h, l, dk = q.shape
    dv = v.shape[-1]
    c = HG_CHUNK
    n = l // c
    q, k, log_f = (t.reshape(b, h, n, c, dk) for t in (q, k, log_f))
    v = v.reshape(b, h, n, c, dv)
    cum = jnp.cumsum(log_f, axis=3)
    causal = jnp.tril(jnp.ones((c, c), bool))[:, :, None]
    decay = jnp.exp(jnp.where(causal, cum[..., :, None, :] - cum[..., None, :, :], NEG))
    scores = jnp.einsum('bhntd,bhnsd,bhntsd->bhnts', q, k, decay)
    o = jnp.einsum('bhnts,bhnsv->bhntv', scores, v)
    last = cum[..., -1, :]
    u = jnp.einsum('bhnsd,bhnsv->bhndv', k * jnp.exp(last[..., None, :] - cum), v)

    def step(s, inp):
        a, uc = inp
        return a[..., None] * s + uc, s

    s_fin, s_start = lax.scan(step, state, (jnp.moveaxis(jnp.exp(last), 2, 0), jnp.moveaxis(u, 2, 0)))
    o = o + jnp.einsum('bhntd,bhndv->bhntv', q * jnp.exp(cum), jnp.moveaxis(s_start, 0, 2))
    return o.reshape(b, h, l, dv), s_fin


def _hgrn2(z, zc, lb, norm_g):
    def prep(t):
        q, i, ff, fb, g = jnp.split(t.astype(F32), 5, -1)
        out = [_heads(jax.nn.silu(q), HG_HEADS), _heads(i, HG_HEADS)]
        for zf in (ff, fb):
            f = lb + (1.0 - lb) * jax.nn.sigmoid(zf)
            key = (1.0 - lb) * jax.nn.sigmoid(-zf)
            out += [_heads(key, HG_HEADS), _heads(jnp.log(f), HG_HEADS)]
        return out, g
    (q, i, kf, lff, kb, lfb), g = prep(z)
    (qc, ic, kcf, lcf, kcb, lcb), gc = prep(zc)
    s0 = jnp.zeros((z.shape[0], HG_HEADS, HG_DK, HG_DK), F32)
    o, o_c = _bidirectional(_gla_chunked, (qc, kcf, ic, lcf), (qc, kcb, ic, lcb),
                            (q, kf, i, lff), (q, kb, i, lfb), s0)
    out = _head_rms(_unheads(o), norm_g, HG_HEADS) * jax.nn.silu(g)
    out_c = _head_rms(_unheads(o_c), norm_g, HG_HEADS) * jax.nn.silu(gc)
    return out, out_c


def _mlstm_chunked(q, k, v, log_i, log_f, state):
    b, h, l, d = q.shape
    c = ML_CHUNK
    n = l // c
    k = k * (d ** -0.5)
    q, k, v = (t.reshape(b, h, n, c, d) for t in (q, k, v))
    log_i, log_f = (t.reshape(b, h, n, c) for t in (log_i, log_f))
    fcum = jnp.cumsum(log_f, -1)
    causal = jnp.tril(jnp.ones((c, c), bool))
    a = jnp.where(causal, fcum[..., :, None] - fcum[..., None, :] + log_i[..., None, :], NEG)
    g = fcum[..., -1:] - fcum + log_i
    g_max = jnp.max(g, -1)
    kw = k * jnp.exp(g - g_max[..., None])[..., None]
    u_c = jnp.einsum('bhnsd,bhnse->bhnde', kw, v)
    u_n = jnp.sum(kw, 3)

    def step(carry, inp):
        c_mat, n_vec, m = carry
        f_last, gm, uc, un = inp
        m_new = jnp.maximum(f_last + m, gm)
        a_old = jnp.exp(f_last + m - m_new)
        a_new = jnp.exp(gm - m_new)
        new = (a_old[..., None, None] * c_mat + a_new[..., None, None] * uc,
               a_old[..., None] * n_vec + a_new[..., None] * un, m_new)
        return new, carry

    xs = tuple(jnp.moveaxis(t, 2, 0) for t in (fcum[..., -1], g_max, u_c, u_n))
    final, starts = lax.scan(step, state, xs)
    c0, n0, m0 = (jnp.moveaxis(t, 0, 2) for t in starts)
    a_init = fcum + m0[..., None]
    m_t = jnp.maximum(jnp.max(a, -1), a_init)
    p = jnp.exp(a - m_t[..., None]) * jnp.einsum('bhntd,bhnsd->bhnts', q, k)
    e0 = jnp.exp(a_init - m_t)
    num = jnp.einsum('bhnts,bhnse->bhnte', p, v) + e0[..., None] * jnp.einsum('bhntd,bhnde->bhnte', q, c0)
    den = jnp.sum(p, -1) + e0 * jnp.einsum('bhntd,bhnd->bhnt', q, n0)
    hout = num / jnp.maximum(jnp.abs(den), jnp.exp(-m_t))[..., None]
    return hout.reshape(b, h, l, d), final


def _mlstm(z, zg, zc, zcg, gate_b, norm_g, rows):
    z = _to_colmajor(z, rows)
    zg = _to_colmajor(zg, rows)
    def prep(t, tg):
        q, k, v, o = jnp.split(t, 4, -1)
        gates = tg.reshape(t.shape[0], t.shape[1], 4, ML_HEADS) + gate_b.astype(F32)
        gates = gates.transpose(2, 0, 3, 1)
        qh, kh, vh = (_heads(a, ML_HEADS) for a in (q, k, v))
        fwd = (qh, kh, vh, gates[0], jax.nn.log_sigmoid(gates[1]))
        bwd = (qh, kh, vh, gates[2], jax.nn.log_sigmoid(gates[3]))
        return fwd, bwd, o
    lat_f, lat_b, o = prep(z, zg)
    ctx_f, ctx_b, oc = prep(zc, zcg)
    bsz = z.shape[0]
    s0 = (jnp.zeros((bsz, ML_HEADS, ML_DH, ML_DH), F32), jnp.zeros((bsz, ML_HEADS, ML_DH), F32),
          jnp.zeros((bsz, ML_HEADS), F32))
    h, h_c = _bidirectional(_mlstm_chunked, ctx_f, ctx_b, lat_f, lat_b, s0)
    out = _from_colmajor(jax.nn.sigmoid(o) * _head_rms(_unheads(h), norm_g, ML_HEADS), rows)
    out_c = jax.nn.sigmoid(oc) * _head_rms(_unheads(h_c), norm_g, ML_HEADS)
    return out, out_c


def _hyena_filters(length, w1, b1, w2, b2, freq, w3):
    t = jnp.linspace(0.0, 1.0, length, dtype=F32)[:, None]
    pos = jnp.arange(length, dtype=F32)[:, None]
    band = jnp.linspace(1e-4, HY_BANDS - 1, HY_BANDS, dtype=F32)[None, :]
    ang = 2.0 * math.pi * pos * band / length
    feats = jnp.concatenate([t, jnp.cos(ang), -jnp.sin(ang)], -1)
    freq = freq.astype(F32)
    hdn = jnp.sin(freq[0] * (feats @ w1.astype(F32) + b1.astype(F32)))
    hdn = jnp.sin(freq[1] * (hdn @ w2.astype(F32) + b2.astype(F32)))
    filt = (hdn @ w3.astype(F32)).reshape(length, 2, HY_ORDER, W_GROUP)
    deltas = jnp.abs(jnp.linspace(math.log(HY_TARGET) / HY_SLOW_DECAY,
                                  math.log(HY_TARGET) / HY_FAST_DECAY, W_GROUP, dtype=F32))
    filt = filt * jnp.exp(-t * deltas)[:, None, None, :]
    fwd, bwd = filt[:, 0], filt[:, 1]
    two_sided = jnp.concatenate([fwd, jnp.zeros_like(fwd[:1]), jnp.flip(bwd[1:], 0)], 0)
    two_sided = two_sided * lax.rsqrt(jnp.sum(two_sided * two_sided, 0, keepdims=True) + EPS)
    return jnp.fft.rfft(two_sided, axis=0)


def _hyena(z, conv_w, conv_b, filt_params, bias):
    length = z.shape[1]
    u = _short_conv(z.astype(F32), conv_w.astype(F32), conv_b.astype(F32))
    v, x1, x2 = jnp.split(u, 3, -1)
    kf = _hyena_filters(length, *filt_params)
    bias = bias.astype(F32)
    y = v
    for n, gate in enumerate((x1, x2)):
        conv = jnp.fft.irfft(jnp.fft.rfft(y, n=2 * length, axis=1) * kf[:, n], n=2 * length, axis=1)[:, :length]
        y = gate * (conv + bias[n] * y)
    return y


def _gated_delta_chunked(q, k, v, log_a, beta, state):
    b, h, l, dk = q.shape
    dv = v.shape[-1]
    c = GD_CHUNK
    n = l // c
    q = q * dk ** -0.5
    q, k = (t.reshape(b, h, n, c, dk) for t in (q, k))
    v = v.reshape(b, h, n, c, dv)
    log_a, beta = (t.reshape(b, h, n, c) for t in (log_a, beta))
    g = jnp.cumsum(log_a, -1)
    causal = jnp.tril(jnp.ones((c, c), bool))
    strict = jnp.tril(jnp.ones((c, c), bool), -1)
    decay = jnp.exp(jnp.where(causal, g[..., :, None] - g[..., None, :], NEG))
    kb = k * beta[..., None]
    m = jnp.eye(c, dtype=q.dtype) + jnp.where(strict, jnp.einsum('bhnid,bhnjd->bhnij', kb, k) * decay, 0.0)
    rhs = jnp.concatenate([v * beta[..., None], kb * jnp.exp(g)[..., None]], -1)
    sol = lax.linalg.triangular_solve(m, rhs, left_side=True, lower=True, unit_diagonal=True)
    u, w = sol[..., :dv], sol[..., dv:]
    qk = jnp.einsum('bhnid,bhnjd->bhnij', q, k) * decay
    q_dec = q * jnp.exp(g)[..., None]
    g_last = g[..., -1]
    k_dec = k * jnp.exp(g_last[..., None] - g)[..., None]

    def step(s, inp):
        u_c, w_c, qk_c, qd_c, kd_c, gl_c = inp
        v_new = u_c - jnp.einsum('bhcd,bhde->bhce', w_c, s)
        o_c = jnp.einsum('bhcd,bhde->bhce', qd_c, s) + jnp.einsum('bhij,bhje->bhie', qk_c, v_new)
        s = jnp.exp(gl_c)[..., None, None] * s + jnp.einsum('bhcd,bhce->bhde', kd_c, v_new)
        return s, o_c

    xs = tuple(jnp.moveaxis(t, 2, 0) for t in (u, w, qk, q_dec, k_dec, g_last))
    s_fin, o = lax.scan(step, state, xs)
    return jnp.moveaxis(o, 0, 2).reshape(b, h, l, dv), s_fin


def _gdn(z, zg, zc, zcg, conv_w, a_log, dt_bias, norm_g):
    a_log = a_log.astype(F32)
    dt_bias = dt_bias.astype(F32)
    def prep(t, tg):
        qkv = jax.nn.silu(_short_conv(t[..., :3 * W_GROUP], conv_w.astype(F32)))
        q, k, v = jnp.split(qkv, 3, -1)
        q, k, v = _l2(_heads(q, GD_HEADS)), _l2(_heads(k, GD_HEADS)), _heads(v, GD_HEADS)
        gate = t[..., 3 * W_GROUP:4 * W_GROUP]
        ab = tg.reshape(t.shape[0], t.shape[1], 4, GD_HEADS).transpose(2, 0, 3, 1)
        dirs = []
        for d in range(2):
            log_a = -jnp.exp(a_log[d])[:, None] * jax.nn.softplus(ab[2 * d] + dt_bias[d][:, None])
            dirs.append((q, k, v, log_a, jax.nn.sigmoid(ab[2 * d + 1])))
        return dirs[0], dirs[1], gate
    lat_f, lat_b, gate = prep(z, zg)
    ctx_f, ctx_b, gate_c = prep(zc, zcg)
    s0 = jnp.zeros((z.shape[0], GD_HEADS, GD_DH, GD_DH), F32)
    o, o_c = _bidirectional(_gated_delta_chunked, ctx_f, ctx_b, lat_f, lat_b, s0)
    out = _head_rms(_unheads(o), norm_g, GD_HEADS) * jax.nn.silu(gate)
    out_c = _head_rms(_unheads(o_c), norm_g, GD_HEADS) * jax.nn.silu(gate_c)
    return out, out_c


def _permute_w_in(w_in):
    o_ml = HG_COLS
    o_hy = o_ml + ML_MAIN + ML_GATES
    o_gd = o_hy + HY_COLS
    parts = [w_in[:, :HG_COLS], w_in[:, o_ml:o_ml + ML_MAIN], w_in[:, o_hy:o_gd],
             w_in[:, o_gd:o_gd + GD_MAIN], w_in[:, o_ml + ML_MAIN:o_hy],
             w_in[:, o_gd + GD_MAIN:],
             jnp.zeros((w_in.shape[0], GATE_PAD - ML_GATES - GD_GATES), w_in.dtype)]
    return jnp.concatenate(parts, 1).astype(BF16)


IN_SPLITS = (HG_COLS, ML_MAIN, HY_COLS, GD_MAIN, GATE_PAD)


def kernel(x, c, ctx, c_ctx, mod_w, mod_b, norm1_g, norm2_g, w_in, w_out, hg_lb_logits, hg_norm_g,
           ml_gate_b, ml_norm_g, hy_conv_w, hy_conv_b, hy_w1, hy_b1, hy_w2, hy_b2, hy_freq, hy_w3,
           hy_bias, gd_conv_w, gd_a_log, gd_dt_bias, gd_norm_g, router_w, router_b, exp_w1, exp_b1,
           exp_w2, exp_b2, final_g):
    n_batch, seq, d = x.shape
    ctx_len = ctx.shape[1]
    depth = mod_w.shape[0]
    rows = seq // GRID_W
    p = jax.nn.softmax(hg_lb_logits.astype(F32), axis=0)
    lower_bounds = jnp.cumsum(p, 0) - p[0]
    s_all = jnp.concatenate([jax.nn.silu(c), jax.nn.silu(c_ctx)[None]], 0)
    xs = jnp.concatenate([ctx, x], 1)
    for l in range(depth):
        mod = (s_all @ mod_w[l] + mod_b[l]).reshape(n_batch + 1, 6, 1, d)
        sh1, sc1, g1, sh2, sc2, g2 = (mod[:, i] for i in range(6))
        z_hg, z_ml, z_hy, z_gd, z_gt = _in_proj(xs, norm1_g[l], sc1, sh1,
                                                _permute_w_in(w_in[l]), IN_SPLITS)
        lat = lambda t: t[:, ctx_len:]
        cx = lambda t: t[:, :ctx_len]
        ml_g, gd_g = z_gt[..., :ML_GATES], z_gt[..., ML_GATES:ML_GATES + GD_GATES]
        a_lat, a_ctx = _hgrn2(lat(z_hg), cx(z_hg), lower_bounds[l], hg_norm_g[l])
        b_lat, b_ctx = _mlstm(lat(z_ml), lat(ml_g), cx(z_ml), cx(ml_g), ml_gate_b[l], ml_norm_g[l], rows)
        filt = (hy_w1[l], hy_b1[l], hy_w2[l], hy_b2[l], hy_freq[l], hy_w3[l])
        c_lat = _hyena(lat(z_hy), hy_conv_w[l], hy_conv_b[l], filt, hy_bias[l])
        c_ctx = _hyena(cx(z_hy), hy_conv_w[l], hy_conv_b[l], filt, hy_bias[l])
        d_lat, d_ctx = _gdn(lat(z_gd), lat(gd_g), cx(z_gd), cx(gd_g), gd_conv_w[l], gd_a_log[l],
                            gd_dt_bias[l], gd_norm_g[l])
        mixes = [jnp.concatenate([mc, ml_], 1) for mc, ml_ in
                 ((a_ctx, a_lat), (b_ctx, b_lat), (c_ctx, c_lat), (d_ctx, d_lat))]
        xs = _out_proj(xs, mixes, g1, w_out[l].astype(BF16))
        h, logits = _moe_pre(xs, norm2_g[l], sc2, sh2, router_w[l], router_b[l])
        y = _moe(h.reshape(-1, d), logits.reshape(-1, GATE_PAD)[:, :N_EXPERTS],
                 exp_w1[l], exp_b1[l], exp_w2[l], exp_b2[l]).reshape(xs.shape)
        g2_rows = jnp.concatenate([jnp.broadcast_to(g2[n_batch:], (n_batch, ctx_len, d)),
                                   jnp.broadcast_to(g2[:n_batch], (n_batch, seq, d))], 1)
        xs = xs + g2_rows * y
    return _final_norm(xs, final_g, ctx_len)
```

```python
import functools
import math

import jax
import jax.numpy as jnp
from jax import lax
from jax.experimental import pallas as pl
from jax.experimental.pallas import tpu as pltpu

F32 = jnp.float32
BF16 = jnp.bfloat16

D_MODEL = 1024
GRID_W = 64
N_MIXERS = 4
W_GROUP = D_MODEL // N_MIXERS
HG_HEADS = 4
HG_DK = W_GROUP // HG_HEADS
HG_CHUNK = 16
ML_HEADS = 4
ML_DH = W_GROUP // ML_HEADS
ML_CHUNK = 64
HY_ORDER = 2
HY_BANDS = 8
HY_FAST_DECAY = 0.3
HY_SLOW_DECAY = 1.5
HY_TARGET = 1e-2
GD_HEADS = 4
GD_DH = W_GROUP // GD_HEADS
GD_CHUNK = 64
N_EXPERTS = 32
TOP_K = 4
SWIGLU_LIMIT = 7.0
SWIGLU_ALPHA = 1.702
EPS = 1e-6
NEG = -1e30
HG_COLS = 5 * W_GROUP
ML_MAIN = 4 * W_GROUP
ML_GATES = 4 * ML_HEADS
HY_COLS = 3 * W_GROUP
GD_MAIN = 4 * W_GROUP
GD_GATES = 4 * GD_HEADS
GATE_PAD = 128

ROW_TILE = 256
MOE_TILE = 256
VMEM_LIMIT = 56 * 1024 * 1024


def _norm_mod(x, gain, sc, sh):
    y = x * lax.rsqrt(jnp.mean(x * x, -1, keepdims=True) + EPS)
    return y * gain * (1.0 + sc) + sh


def _in_proj_kernel(x_ref, g_ref, sc_ref, sh_ref, w_ref, *out_refs, splits):
    h = _norm_mod(x_ref[...], g_ref[...], sc_ref[...], sh_ref[...]).astype(BF16)
    off = 0
    for o_ref, n in zip(out_refs, splits):
        o_ref[...] = jnp.dot(h, w_ref[:, off:off + n], preferred_element_type=F32)
        off += n


def _mod_index_map(n_batch, n_tiles):
    return lambda b, j: (jnp.where(j == n_tiles - 1, n_batch, b), 0, 0)


def _in_proj(xs, gain, sc_tab, sh_tab, w, splits):
    n_batch, seq, d = xs.shape
    n = w.shape[1]
    grid = (n_batch, seq // ROW_TILE)
    mod_spec = pl.BlockSpec((None, 1, d), _mod_index_map(n_batch, seq // ROW_TILE))
    return pl.pallas_call(
        functools.partial(_in_proj_kernel, splits=splits),
        grid=grid,
        in_specs=[
            pl.BlockSpec((None, ROW_TILE, d), lambda b, j: (b, j, 0)),
            pl.BlockSpec((1, d), lambda b, j: (0, 0)),
            mod_spec, mod_spec,
            pl.BlockSpec((d, n), lambda b, j: (0, 0)),
        ],
        out_specs=[pl.BlockSpec((None, ROW_TILE, s), lambda b, j: (b, j, 0)) for s in splits],
        out_shape=[jax.ShapeDtypeStruct((n_batch, seq, s), F32) for s in splits],
        compiler_params=pltpu.CompilerParams(
            dimension_semantics=("parallel", "parallel"), vmem_limit_bytes=VMEM_LIMIT),
        name="in_proj",
    )(xs, gain.reshape(1, d), sc_tab, sh_tab, w)


def _moe_pre_kernel(x_ref, g_ref, sc_ref, sh_ref, rw_ref, rb_ref, h_ref, lg_ref):
    h = _norm_mod(x_ref[...], g_ref[...], sc_ref[...], sh_ref[...])
    h_ref[...] = h.astype(BF16)
    lg_ref[...] = jnp.dot(h, rw_ref[...], preferred_element_type=F32,
                          precision=lax.Precision.HIGHEST) + rb_ref[...]


def _moe_pre(xs, gain, sc_tab, sh_tab, router_w, router_b):
    n_batch, seq, d = xs.shape
    pad = GATE_PAD - N_EXPERTS
    rw = jnp.pad(router_w.astype(F32), ((0, 0), (0, pad)))
    rb = jnp.pad(router_b.astype(F32), (0, pad), constant_values=NEG).reshape(1, GATE_PAD)
    mod_spec = pl.BlockSpec((None, 1, d), _mod_index_map(n_batch, seq // ROW_TILE))
    return pl.pallas_call(
        _moe_pre_kernel,
        grid=(n_batch, seq // ROW_TILE),
        in_specs=[
            pl.BlockSpec((None, ROW_TILE, d), lambda b, j: (b, j, 0)),
            pl.BlockSpec((1, d), lambda b, j: (0, 0)),
            mod_spec, mod_spec,
            pl.BlockSpec((d, GATE_PAD), lambda b, j: (0, 0)),
            pl.BlockSpec((1, GATE_PAD), lambda b, j: (0, 0)),
        ],
        out_specs=[pl.BlockSpec((None, ROW_TILE, d), lambda b, j: (b, j, 0)),
                   pl.BlockSpec((None, ROW_TILE, GATE_PAD), lambda b, j: (b, j, 0))],
        out_shape=[jax.ShapeDtypeStruct((n_batch, seq, d), BF16),
                   jax.ShapeDtypeStruct((n_batch, seq, GATE_PAD), F32)],
        compiler_params=pltpu.CompilerParams(
            dimension_semantics=("parallel", "parallel"), vmem_limit_bytes=VMEM_LIMIT),
        name="moe_pre",
    )(xs, gain.reshape(1, d), sc_tab, sh_tab, rw, rb)


def _out_proj_kernel(*refs, kinds):
    x_ref, refs = refs[0], refs[1:]
    gains_ref, g_ref, heads_ref, w_ref, o_ref = refs[-5:]
    heads = heads_ref[...] * (1.0 / HG_DK)
    acc = jnp.zeros(o_ref.shape, F32)
    pos = 0
    for i, kind in enumerate(kinds):
        if kind == "final":
            m = refs[pos][...]
            pos += 1
        else:
            o = refs[pos][...] + refs[pos + 1][...]
            gate = refs[pos + 2][...]
            pos += 3
            ms = _dot_hi(o * o, heads)
            o = o * lax.rsqrt(ms + EPS) * gains_ref[i:i + 1, :]
            m = o * (jax.nn.silu(gate) if kind == "silu" else jax.nn.sigmoid(gate))
        acc += jnp.dot(m.astype(BF16), w_ref[i * W_GROUP:(i + 1) * W_GROUP, :],
                       preferred_element_type=F32)
    o_ref[...] = x_ref[...] + g_ref[...] * acc


def _out_proj(xs, mixers, gains, gate_tab, w_out):
    n_batch, seq, d = xs.shape
    row = lambda w, cb=0: pl.BlockSpec((None, ROW_TILE, w), lambda b, j: (b, j, cb))
    args, specs, kinds = [], [], []
    for m in mixers:
        kinds.append(m[0])
        if m[0] == "final":
            args.append(m[1])
            specs.append(row(W_GROUP))
        else:
            src, cb = m[3]
            args += [m[1], m[2], src]
            specs += [row(W_GROUP), row(W_GROUP), row(W_GROUP, cb)]
    heads = _hg_consts()[3]
    return pl.pallas_call(
        functools.partial(_out_proj_kernel, kinds=tuple(kinds)),
        grid=(n_batch, seq // ROW_TILE),
        in_specs=[row(d)] + specs + [
            pl.BlockSpec((N_MIXERS, W_GROUP), lambda b, j: (0, 0)),
            pl.BlockSpec((None, 1, d), _mod_index_map(n_batch, seq // ROW_TILE)),
            pl.BlockSpec((W_GROUP, W_GROUP), lambda b, j: (0, 0)),
            pl.BlockSpec((d, d), lambda b, j: (0, 0)),
        ],
        out_specs=row(d),
        out_shape=jax.ShapeDtypeStruct((n_batch, seq, d), F32),
        compiler_params=pltpu.CompilerParams(
            dimension_semantics=("parallel", "parallel"), vmem_limit_bytes=VMEM_LIMIT),
        name="out_proj",
    )(xs, *args, gains, gate_tab, heads, w_out)


def _expert_kernel(bexp_ref, x_ref, gate_ref, w1_ref, b1_ref, w2_ref, b2_ref, o_ref, w1b, w2b):
    i = pl.program_id(0)
    changed = jnp.logical_or(i == 0, bexp_ref[i] != bexp_ref[jnp.maximum(i - 1, 0)])

    @pl.when(changed)
    def _():
        w1b[...] = w1_ref[...].astype(BF16)
        w2b[...] = w2_ref[...].astype(BF16)

    f = w2b.shape[0]
    u = jnp.dot(x_ref[...], w1b[...], preferred_element_type=F32) + b1_ref[...]
    gate = jnp.minimum(u[:, :f], SWIGLU_LIMIT)
    lin = jnp.clip(u[:, f:], -SWIGLU_LIMIT, SWIGLU_LIMIT)
    y = (lin + 1.0) * gate * jax.nn.sigmoid(SWIGLU_ALPHA * gate)
    out = jnp.dot(y.astype(BF16), w2b[...], preferred_element_type=F32) + b2_ref[...]
    o_ref[...] = out * gate_ref[...]


def _expert_ffn(block_exp, xg, slot_gate, w1, b1, w2, b2):
    n_slots, d = xg.shape
    n_exp, _, f2 = w1.shape
    f = f2 // 2
    n_blocks = n_slots // MOE_TILE
    return pl.pallas_call(
        _expert_kernel,
        grid_spec=pltpu.PrefetchScalarGridSpec(
            num_scalar_prefetch=1,
            grid=(n_blocks,),
            in_specs=[
                pl.BlockSpec((MOE_TILE, d), lambda i, be: (i, 0)),
                pl.BlockSpec((MOE_TILE, 1), lambda i, be: (i, 0)),
                pl.BlockSpec((None, d, f2), lambda i, be: (be[i], 0, 0)),
                pl.BlockSpec((None, 1, f2), lambda i, be: (be[i], 0, 0)),
                pl.BlockSpec((None, f, d), lambda i, be: (be[i], 0, 0)),
                pl.BlockSpec((None, 1, d), lambda i, be: (be[i], 0, 0)),
            ],
            out_specs=pl.BlockSpec((MOE_TILE, d), lambda i, be: (i, 0)),
            scratch_shapes=[pltpu.VMEM((d, f2), BF16), pltpu.VMEM((f, d), BF16)],
        ),
        out_shape=jax.ShapeDtypeStruct((n_slots, d), F32),
        compiler_params=pltpu.CompilerParams(
            dimension_semantics=("arbitrary",), vmem_limit_bytes=VMEM_LIMIT),
        name="expert_ffn",
    )(block_exp, xg, slot_gate, w1, b1.reshape(n_exp, 1, f2), w2, b2.reshape(n_exp, 1, d))


def _moe(h, logits, w1, b1, w2, b2):
    t_count, d = h.shape
    top_v, top_i = lax.top_k(logits, TOP_K)
    gates = jax.nn.softmax(top_v, -1)
    n_pairs = t_count * TOP_K
    e_flat = top_i.reshape(-1)
    order = jnp.argsort(e_flat)
    e_sorted = e_flat[order]
    counts = jnp.bincount(e_flat, length=N_EXPERTS)
    padded = (counts + MOE_TILE - 1) // MOE_TILE * MOE_TILE
    start = jnp.cumsum(counts) - counts
    pend = jnp.cumsum(padded)
    pstart = pend - padded
    dest = pstart[e_sorted] + (jnp.arange(n_pairs, dtype=jnp.int32) - start[e_sorted])
    n_blocks = -(-(n_pairs + N_EXPERTS * (MOE_TILE - 1)) // MOE_TILE)
    n_slots = n_blocks * MOE_TILE
    pos = jnp.zeros((n_pairs,), jnp.int32).at[order].set(dest.astype(jnp.int32))
    slot_tok = jnp.zeros((n_slots,), jnp.int32).at[pos].set(
        jnp.repeat(jnp.arange(t_count, dtype=jnp.int32), TOP_K))
    slot_gate = jnp.zeros((n_slots,), F32).at[pos].set(gates.reshape(-1))
    block_exp = jnp.clip(jnp.searchsorted(pend, jnp.arange(n_blocks) * MOE_TILE, side='right'),
                         0, N_EXPERTS - 1).astype(jnp.int32)
    y = _expert_ffn(block_exp, h[slot_tok], slot_gate[:, None], w1, b1, w2, b2)
    return jnp.sum(y[pos.reshape(t_count, TOP_K)], 1)


def _final_norm_kernel(x_ref, g_ref, o_ref):
    x = x_ref[...]
    o_ref[...] = x * lax.rsqrt(jnp.mean(x * x, -1, keepdims=True) + EPS) * g_ref[...]


def _final_norm(xs, gain, n_lat):
    n_batch, _, d = xs.shape
    return pl.pallas_call(
        _final_norm_kernel,
        grid=(n_batch, n_lat // ROW_TILE),
        in_specs=[pl.BlockSpec((None, ROW_TILE, d), lambda b, j: (b, j, 0)),
                  pl.BlockSpec((1, d), lambda b, j: (0, 0))],
        out_specs=pl.BlockSpec((None, ROW_TILE, d), lambda b, j: (b, j, 0)),
        out_shape=jax.ShapeDtypeStruct((n_batch, n_lat, d), F32),
        compiler_params=pltpu.CompilerParams(dimension_semantics=("parallel", "parallel")),
        name="final_norm",
    )(xs, gain.reshape(1, d))


HG_SUB = 16


def _hg_consts():
    r = lax.broadcasted_iota(jnp.int32, (ROW_TILE, ROW_TILE), 0)
    c = lax.broadcasted_iota(jnp.int32, (ROW_TILE, ROW_TILE), 1)
    same_chunk = (r // HG_SUB) == (c // HG_SUB)
    tri_f = jnp.where(same_chunk & (c <= r), 1.0, 0.0).astype(F32)
    tri_b = jnp.where(same_chunk & (c >= r), 1.0, 0.0).astype(F32)
    blk = jnp.where(same_chunk, 1.0, 0.0).astype(F32)
    heads = jnp.where((r // HG_DK) == (c // HG_DK), 1.0, 0.0).astype(F32)
    return tri_f, tri_b, blk, heads


def _hg_prepare(z_ref, zf_col, lb, tri, blk, qs_s, key_s, cm_s, qh_s, kh_s, dec_s):
    q = jax.nn.silu(z_ref[:, 0:W_GROUP])
    zf = z_ref[:, zf_col:zf_col + W_GROUP]
    f = lb + (1.0 - lb) * jax.nn.sigmoid(zf)
    key = (1.0 - lb) * jax.nn.sigmoid(-zf)
    logf = jnp.log(f)
    cum = jnp.dot(tri, logf, preferred_element_type=F32, precision=lax.Precision.HIGHEST)
    tot = jnp.dot(blk, logf, preferred_element_type=F32, precision=lax.Precision.HIGHEST)
    qs_s[...] = q
    key_s[...] = key
    cm_s[...] = cum
    qh_s[...] = (q * jnp.exp(cum)).astype(BF16)
    kh_s[...] = (key * jnp.exp(tot - cum)).astype(BF16)
    dec_s[...] = jnp.exp(tot)


def _hg_chunk(c, reverse, z_ref, o_ref, st_ref, heads_bf, heads_f, qs_s, key_s, cm_s, qh_s, kh_s, dec_s):
    sl = pl.ds(pl.multiple_of(c * HG_SUB, HG_SUB), HG_SUB)
    q, k, cm = qs_s[sl, :], key_s[sl, :], cm_s[sl, :]
    v = z_ref[sl, W_GROUP:2 * W_GROUP]
    st = st_ref[...]
    o = lax.dot_general(qh_s[sl, :], st.astype(BF16), (((1,), (1,)), ((), ())),
                        preferred_element_type=F32)
    t_idx = lax.broadcasted_iota(jnp.int32, (HG_SUB, W_GROUP), 0)
    parts = []
    for s in range(HG_SUB):
        live = (t_idx <= s) if reverse else (t_idx >= s)
        e = jnp.exp(jnp.where(live, cm - cm[s:s + 1, :], NEG))
        parts.append((q * k[s:s + 1, :] * e).astype(BF16))
    r = jnp.dot(jnp.concatenate(parts, 0), heads_bf, preferred_element_type=F32)
    for s in range(HG_SUB):
        o += r[s * HG_SUB:(s + 1) * HG_SUB, :] * v[s:s + 1, :]
    o_ref[sl, :] = o
    ut = lax.dot_general(v.astype(BF16), kh_s[sl, :], (((0,), (0,)), ((), ())),
                         preferred_element_type=F32)
    st_ref[...] = st * dec_s[sl, :][0:1, :] + ut * heads_f


def _hgrn2_kernel(zf_ref, zb_ref, lb_ref, trif_ref, trib_ref, blk_ref, heads_ref, of_ref, ob_ref,
                  st_s, qs_s, key_s, cm_s, qh_s, kh_s, dec_s):
    @pl.when(pl.program_id(1) == 0)
    def _():
        st_s[...] = jnp.zeros_like(st_s)

    lb = lb_ref[...]
    blk = blk_ref[...]
    heads_f = heads_ref[...]
    heads_bf = heads_f.astype(BF16)
    _hg_prepare(zf_ref, 2 * W_GROUP, lb, trif_ref[...], blk, qs_s.at[0], key_s.at[0], cm_s.at[0],
                qh_s.at[0], kh_s.at[0], dec_s.at[0])
    _hg_prepare(zb_ref, 3 * W_GROUP, lb, trib_ref[...], blk, qs_s.at[1], key_s.at[1], cm_s.at[1],
                qh_s.at[1], kh_s.at[1], dec_s.at[1])
    n_chunks = ROW_TILE // HG_SUB

    def body(c, carry):
        _hg_chunk(c, False, zf_ref, of_ref, st_s.at[0], heads_bf, heads_f, qs_s.at[0], key_s.at[0],
                  cm_s.at[0], qh_s.at[0], kh_s.at[0], dec_s.at[0])
        _hg_chunk(n_chunks - 1 - c, True, zb_ref, ob_ref, st_s.at[1], heads_bf, heads_f, qs_s.at[1],
                  key_s.at[1], cm_s.at[1], qh_s.at[1], kh_s.at[1], dec_s.at[1])
        return carry

    lax.fori_loop(0, n_chunks, body, 0)


def _scan_tiles(n_lat_tiles):
    fwd = lambda j: jnp.where(j == 0, n_lat_tiles, j - 1)
    bwd = lambda j: jnp.where(j == 0, n_lat_tiles, n_lat_tiles - j)
    return fwd, bwd


def _hgrn2_scan(z_hg, lb):
    n_batch, seq, _ = z_hg.shape
    n_tiles = seq // ROW_TILE
    fwd, bwd = _scan_tiles(n_tiles - 1)
    const = pl.BlockSpec((ROW_TILE, ROW_TILE), lambda b, j: (0, 0))
    scr = lambda dt: pltpu.VMEM((2, ROW_TILE, W_GROUP), dt)
    return pl.pallas_call(
        _hgrn2_kernel,
        grid=(n_batch, n_tiles),
        in_specs=[pl.BlockSpec((None, ROW_TILE, HG_COLS), lambda b, j: (b, fwd(j), 0)),
                  pl.BlockSpec((None, ROW_TILE, HG_COLS), lambda b, j: (b, bwd(j), 0)),
                  pl.BlockSpec((1, W_GROUP), lambda b, j: (0, 0)),
                  const, const, const, const],
        out_specs=[pl.BlockSpec((None, ROW_TILE, W_GROUP), lambda b, j: (b, fwd(j), 0)),
                   pl.BlockSpec((None, ROW_TILE, W_GROUP), lambda b, j: (b, bwd(j), 0))],
        out_shape=[jax.ShapeDtypeStruct((n_batch, seq, W_GROUP), F32)] * 2,
        scratch_shapes=[pltpu.VMEM((2, W_GROUP, W_GROUP), F32), scr(F32), scr(F32), scr(F32),
                        scr(BF16), scr(BF16), scr(F32)],
        compiler_params=pltpu.CompilerParams(
            dimension_semantics=("parallel", "arbitrary"), vmem_limit_bytes=VMEM_LIMIT),
        name="hgrn2_scan",
    )(z_hg, z_hg, lb.reshape(1, W_GROUP), *_hg_consts())


def _dot_hi(a, b):
    return jnp.dot(a, b, preferred_element_type=F32, precision=lax.Precision.HIGHEST)


def _dot_nt(a, b, precision=None):
    return lax.dot_general(a, b, (((1,), (1,)), ((), ())), preferred_element_type=F32,
                           precision=precision)


def _dot_tn(a, b):
    return lax.dot_general(a, b, (((0,), (0,)), ((), ())), preferred_element_type=F32)


def _chunk_consts(chunk, n_heads, gate_cols):
    r = lax.broadcasted_iota(jnp.int32, (chunk, chunk), 0)
    c = lax.broadcasted_iota(jnp.int32, (chunk, chunk), 1)
    low = (c <= r).astype(F32)
    upp = (c >= r).astype(F32)
    dh = W_GROUP // n_heads
    hr = lax.broadcasted_iota(jnp.int32, (W_GROUP, W_GROUP), 0) // dh
    hc = lax.broadcasted_iota(jnp.int32, (W_GROUP, W_GROUP), 1) // dh
    heads = (hr == hc).astype(F32)
    gc = lax.broadcasted_iota(jnp.int32, (GATE_PAD, W_GROUP), 0)
    gh = lax.broadcasted_iota(jnp.int32, (GATE_PAD, W_GROUP), 1) // dh
    expand = jnp.stack([(gc == g0 + gh).astype(F32) for g0 in gate_cols])
    sr = lax.broadcasted_iota(jnp.int32, (8, GATE_PAD), 0)
    sc = lax.broadcasted_iota(jnp.int32, (8, GATE_PAD), 1)
    select = jnp.stack([((sc == g0 + sr) & (sr < n_heads)).astype(F32) for g0 in gate_cols])
    return low, upp, heads, expand, select


def _ml_direction(reverse, z_ref, g_ref, gb, o_ref, c_ref, n_ref, m_ref, low, upp, heads,
                  ex_i, ex_f, sel_i, sel_f):
    q = z_ref[:, 0:W_GROUP]
    k = z_ref[:, W_GROUP:2 * W_GROUP] * (ML_DH ** -0.5)
    v = z_ref[:, 2 * W_GROUP:3 * W_GROUP]
    gates = g_ref[...] + gb
    col_sum, row_sum = (upp, low) if reverse else (low, upp)
    li = _dot_hi(gates, ex_i)
    lf = jax.nn.log_sigmoid(_dot_hi(gates, ex_f))
    li_rows = _dot_nt(sel_i, gates, lax.Precision.HIGHEST)
    lf_rows = jax.nn.log_sigmoid(_dot_nt(sel_f, gates, lax.Precision.HIGHEST))
    fcum = _dot_hi(col_sum, lf)
    fcum_rows = _dot_hi(lf_rows, row_sum)
    last = 0 if reverse else ML_CHUNK - 1
    ftot = fcum[last:last + 1, :]
    g = ftot - fcum + li
    g_max = jnp.max(g, axis=0, keepdims=True)
    kw = k * jnp.exp(g - g_max)
    c0, n0, m0 = c_ref[...], n_ref[...], m_ref[...]
    q_bf = q.astype(BF16)
    k_bf = k.astype(BF16)
    qc = jnp.dot(q_bf, c0.astype(BF16), preferred_element_type=F32)
    qn = jnp.dot((q * n0).astype(BF16), heads.astype(BF16), preferred_element_type=F32)
    a_init = fcum + m0
    rr = lax.broadcasted_iota(jnp.int32, (ML_CHUNK, ML_CHUNK), 0)
    cc = lax.broadcasted_iota(jnp.int32, (ML_CHUNK, ML_CHUNK), 1)
    live = (cc >= rr) if reverse else (cc <= rr)
    lane_head = lax.broadcasted_iota(jnp.int32, (ML_CHUNK, W_GROUP), 1) // ML_DH
    outs = []
    for h in range(ML_HEADS):
        hs = slice(h * ML_DH, (h + 1) * ML_DH)
        a = jnp.where(live, fcum[:, hs] - fcum_rows[h:h + 1, :] + li_rows[h:h + 1, :], NEG)
        a_init_h = a_init[:, hs]
        m_t = jnp.maximum(jnp.max(a, axis=1, keepdims=True), a_init_h)
        qk = _dot_nt(jnp.where(lane_head == h, q, 0.0).astype(BF16), k_bf)
        p = jnp.exp(a - m_t) * qk
        e0 = jnp.exp(a_init_h - m_t)
        num = jnp.dot(p.astype(BF16), v[:, hs].astype(BF16), preferred_element_type=F32) + e0 * qc[:, hs]
        den = jnp.sum(p, axis=1, keepdims=True) + e0 * qn[:, hs]
        outs.append(num / jnp.maximum(jnp.abs(den), jnp.exp(-m_t)))
    o_ref[...] = jnp.concatenate(outs, axis=1)
    m_new = jnp.maximum(ftot + m0, g_max)
    a_old = jnp.exp(ftot + m0 - m_new)
    a_new = jnp.exp(g_max - m_new)
    u_c = _dot_tn(kw.astype(BF16), v.astype(BF16)) * heads
    c_ref[...] = a_old * c0 + a_new * u_c
    n_ref[...] = a_old * n0 + a_new * jnp.sum(kw, axis=0, keepdims=True)
    m_ref[...] = m_new


def _mlstm_kernel(zf_ref, zb_ref, gf_ref, gb_ref, bias_ref, low_ref, upp_ref, heads_ref, ex_ref, sel_ref,
                  of_ref, ob_ref, c_s, n_s, m_s):
    @pl.when(pl.program_id(1) == 0)
    def _():
        c_s[...] = jnp.zeros_like(c_s)
        n_s[...] = jnp.zeros_like(n_s)
        m_s[...] = jnp.zeros_like(m_s)

    low, upp, heads, bias = low_ref[...], upp_ref[...], heads_ref[...], bias_ref[...]
    _ml_direction(False, zf_ref, gf_ref, bias, of_ref, c_s.at[0], n_s.at[0], m_s.at[0], low, upp, heads,
                  ex_ref[0], ex_ref[1], sel_ref[0], sel_ref[1])
    _ml_direction(True, zb_ref, gb_ref, bias, ob_ref, c_s.at[1], n_s.at[1], m_s.at[1], low, upp, heads,
                  ex_ref[2], ex_ref[3], sel_ref[2], sel_ref[3])


def _scan_chunks(n_lat, n_ctx):
    fwd = lambda j: jnp.where(j < n_ctx, n_lat + j, j - n_ctx)
    bwd = lambda j: jnp.where(j < n_ctx, n_lat + n_ctx - 1 - j, n_lat + n_ctx - 1 - j)
    return fwd, bwd


def _mlstm_scan(z_ml, z_gt, gate_b, ctx_len):
    n_batch, seq, _ = z_ml.shape
    n_chunks = seq // ML_CHUNK
    n_ctx = ctx_len // ML_CHUNK
    fwd, bwd = _scan_chunks(n_chunks - n_ctx, n_ctx)
    low, upp, heads, expand, select = _chunk_consts(ML_CHUNK, ML_HEADS, (0, 4, 8, 12))
    bias = jnp.pad(gate_b.astype(F32).reshape(1, ML_GATES), ((0, 0), (0, GATE_PAD - ML_GATES)))
    full = lambda a: pl.BlockSpec(a.shape, lambda b, j: (0,) * a.ndim)
    return pl.pallas_call(
        _mlstm_kernel,
        grid=(n_batch, n_chunks),
        in_specs=[pl.BlockSpec((None, ML_CHUNK, ML_MAIN), lambda b, j: (b, fwd(j), 0)),
                  pl.BlockSpec((None, ML_CHUNK, ML_MAIN), lambda b, j: (b, bwd(j), 0)),
                  pl.BlockSpec((None, ML_CHUNK, GATE_PAD), lambda b, j: (b, fwd(j), 0)),
                  pl.BlockSpec((None, ML_CHUNK, GATE_PAD), lambda b, j: (b, bwd(j), 0)),
                  full(bias), full(low), full(upp), full(heads), full(expand), full(select)],
        out_specs=[pl.BlockSpec((None, ML_CHUNK, W_GROUP), lambda b, j: (b, fwd(j), 0)),
                   pl.BlockSpec((None, ML_CHUNK, W_GROUP), lambda b, j: (b, bwd(j), 0))],
        out_shape=[jax.ShapeDtypeStruct((n_batch, seq, W_GROUP), F32)] * 2,
        scratch_shapes=[pltpu.VMEM((2, W_GROUP, W_GROUP), F32), pltpu.VMEM((2, 1, W_GROUP), F32),
                        pltpu.VMEM((2, 1, W_GROUP), F32)],
        compiler_params=pltpu.CompilerParams(
            dimension_semantics=("parallel", "arbitrary"), vmem_limit_bytes=VMEM_LIMIT),
        name="mlstm_scan",
    )(z_ml, z_ml, z_gt, z_gt, bias, low, upp, heads, expand, select)


def _head_rms(o, gain, n_heads):
    b, l, w = o.shape
    oh = o.reshape(b, l, n_heads, w // n_heads)
    oh = oh * lax.rsqrt(jnp.mean(oh * oh, -1, keepdims=True) + EPS)
    return oh.reshape(b, l, w) * gain.astype(F32)


def _heads(t, n_heads):
    b, l, w = t.shape
    return t.reshape(b, l, n_heads, w // n_heads).transpose(0, 2, 1, 3)


def _unheads(t):
    b, h, l, d = t.shape
    return t.transpose(0, 2, 1, 3).reshape(b, l, h * d)


def _l2(t):
    return t * lax.rsqrt(jnp.sum(t * t, -1, keepdims=True) + EPS)


def _short_conv(x, w, b=None):
    k = w.shape[0]
    y = lax.conv_general_dilated(x, w[:, None, :].astype(x.dtype), window_strides=(1,),
                                 padding=[(k // 2, k - 1 - k // 2)],
                                 dimension_numbers=('NWC', 'WIO', 'NWC'),
                                 feature_group_count=x.shape[-1])
    return y if b is None else y + b.astype(x.dtype)


def _to_colmajor(t, rows):
    b, l, w = t.shape
    return t.reshape(b, rows, GRID_W, w).swapaxes(1, 2).reshape(b, l, w)


def _from_colmajor(t, rows):
    b, l, w = t.shape
    return t.reshape(b, GRID_W, rows, w).swapaxes(1, 2).reshape(b, l, w)


def _bidirectional(scan_fn, ctx_fwd, ctx_bwd, lat_fwd, lat_bwd, state0):
    flip = lambda ts: tuple(jnp.flip(t, 2) for t in ts)
    o_cf, s_cf = scan_fn(*ctx_fwd, state0)
    o_cb, s_cb = scan_fn(*flip(ctx_bwd), state0)
    o_lf, _ = scan_fn(*lat_fwd, s_cf)
    o_lb, _ = scan_fn(*flip(lat_bwd), s_cb)
    return o_lf + jnp.flip(o_lb, 2), o_cf + jnp.flip(o_cb, 2)


def _gla_chunked(q, k, v, log_f, state):
    b, h, l, dk = q.shape
    dv = v.shape[-1]
    c = HG_CHUNK
    n = l // c
    q, k, log_f = (t.reshape(b, h, n, c, dk) for t in (q, k, log_f))
    v = v.reshape(b, h, n, c, dv)
    cum = jnp.cumsum(log_f, axis=3)
    causal = jnp.tril(jnp.ones((c, c), bool))[:, :, None]
    decay = jnp.exp(jnp.where(causal, cum[..., :, None, :] - cum[..., None, :, :], NEG))
    scores = jnp.einsum('bhntd,bhnsd,bhntsd->bhnts', q, k, decay)
    o = jnp.einsum('bhnts,bhnsv->bhntv', scores, v)
    last = cum[..., -1, :]
    u = jnp.einsum('bhnsd,bhnsv->bhndv', k * jnp.exp(last[..., None, :] - cum), v)

    def step(s, inp):
        a, uc = inp
        return a[..., None] * s + uc, s

    s_fin, s_start = lax.scan(step, state, (jnp.moveaxis(jnp.exp(last), 2, 0), jnp.moveaxis(u, 2, 0)))
    o = o + jnp.einsum('bhntd,bhndv->bhntv', q * jnp.exp(cum), jnp.moveaxis(s_start, 0, 2))
    return o.reshape(b, h, l, dv), s_fin


def _hgrn2(z, zc, lb, norm_g):
    def prep(t):
        q, i, ff, fb, g = jnp.split(t.astype(F32), 5, -1)
        out = [_heads(jax.nn.silu(q), HG_HEADS), _heads(i, HG_HEADS)]
        for zf in (ff, fb):
            f = lb + (1.0 - lb) * jax.nn.sigmoid(zf)
            key = (1.0 - lb) * jax.nn.sigmoid(-zf)
            out += [_heads(key, HG_HEADS), _heads(jnp.log(f), HG_HEADS)]
        return out, g
    (q, i, kf, lff, kb, lfb), g = prep(z)
    (qc, ic, kcf, lcf, kcb, lcb), gc = prep(zc)
    s0 = jnp.zeros((z.shape[0], HG_HEADS, HG_DK, HG_DK), F32)
    o, o_c = _bidirectional(_gla_chunked, (qc, kcf, ic, lcf), (qc, kcb, ic, lcb),
                            (q, kf, i, lff), (q, kb, i, lfb), s0)
    out = _head_rms(_unheads(o), norm_g, HG_HEADS) * jax.nn.silu(g)
    out_c = _head_rms(_unheads(o_c), norm_g, HG_HEADS) * jax.nn.silu(gc)
    return out, out_c


def _mlstm_chunked(q, k, v, log_i, log_f, state):
    b, h, l, d = q.shape
    c = ML_CHUNK
    n = l // c
    k = k * (d ** -0.5)
    q, k, v = (t.reshape(b, h, n, c, d) for t in (q, k, v))
    log_i, log_f = (t.reshape(b, h, n, c) for t in (log_i, log_f))
    fcum = jnp.cumsum(log_f, -1)
    causal = jnp.tril(jnp.ones((c, c), bool))
    a = jnp.where(causal, fcum[..., :, None] - fcum[..., None, :] + log_i[..., None, :], NEG)
    g = fcum[..., -1:] - fcum + log_i
    g_max = jnp.max(g, -1)
    kw = k * jnp.exp(g - g_max[..., None])[..., None]
    u_c = jnp.einsum('bhnsd,bhnse->bhnde', kw, v)
    u_n = jnp.sum(kw, 3)

    def step(carry, inp):
        c_mat, n_vec, m = carry
        f_last, gm, uc, un = inp
        m_new = jnp.maximum(f_last + m, gm)
        a_old = jnp.exp(f_last + m - m_new)
        a_new = jnp.exp(gm - m_new)
        new = (a_old[..., None, None] * c_mat + a_new[..., None, None] * uc,
               a_old[..., None] * n_vec + a_new[..., None] * un, m_new)
        return new, carry

    xs = tuple(jnp.moveaxis(t, 2, 0) for t in (fcum[..., -1], g_max, u_c, u_n))
    final, starts = lax.scan(step, state, xs)
    c0, n0, m0 = (jnp.moveaxis(t, 0, 2) for t in starts)
    a_init = fcum + m0[..., None]
    m_t = jnp.maximum(jnp.max(a, -1), a_init)
    p = jnp.exp(a - m_t[..., None]) * jnp.einsum('bhntd,bhnsd->bhnts', q, k)
    e0 = jnp.exp(a_init - m_t)
    num = jnp.einsum('bhnts,bhnse->bhnte', p, v) + e0[..., None] * jnp.einsum('bhntd,bhnde->bhnte', q, c0)
    den = jnp.sum(p, -1) + e0 * jnp.einsum('bhntd,bhnd->bhnt', q, n0)
    hout = num / jnp.maximum(jnp.abs(den), jnp.exp(-m_t))[..., None]
    return hout.reshape(b, h, l, d), final


def _mlstm(z, zg, zc, zcg, gate_b, norm_g, rows):
    z = _to_colmajor(z, rows)
    zg = _to_colmajor(zg, rows)
    def prep(t, tg):
        q, k, v, o = jnp.split(t, 4, -1)
        gates = tg.reshape(t.shape[0], t.shape[1], 4, ML_HEADS) + gate_b.astype(F32)
        gates = gates.transpose(2, 0, 3, 1)
        qh, kh, vh = (_heads(a, ML_HEADS) for a in (q, k, v))
        fwd = (qh, kh, vh, gates[0], jax.nn.log_sigmoid(gates[1]))
        bwd = (qh, kh, vh, gates[2], jax.nn.log_sigmoid(gates[3]))
        return fwd, bwd, o
    lat_f, lat_b, o = prep(z, zg)
    ctx_f, ctx_b, oc = prep(zc, zcg)
    bsz = z.shape[0]
    s0 = (jnp.zeros((bsz, ML_HEADS, ML_DH, ML_DH), F32), jnp.zeros((bsz, ML_HEADS, ML_DH), F32),
          jnp.zeros((bsz, ML_HEADS), F32))
    h, h_c = _bidirectional(_mlstm_chunked, ctx_f, ctx_b, lat_f, lat_b, s0)
    out = _from_colmajor(jax.nn.sigmoid(o) * _head_rms(_unheads(h), norm_g, ML_HEADS), rows)
    out_c = jax.nn.sigmoid(oc) * _head_rms(_unheads(h_c), norm_g, ML_HEADS)
    return out, out_c


def _hyena_filters(length, w1, b1, w2, b2, freq, w3):
    t = jnp.linspace(0.0, 1.0, length, dtype=F32)[:, None]
    pos = jnp.arange(length, dtype=F32)[:, None]
    band = jnp.linspace(1e-4, HY_BANDS - 1, HY_BANDS, dtype=F32)[None, :]
    ang = 2.0 * math.pi * pos * band / length
    feats = jnp.concatenate([t, jnp.cos(ang), -jnp.sin(ang)], -1)
    freq = freq.astype(F32)
    hdn = jnp.sin(freq[0] * (feats @ w1.astype(F32) + b1.astype(F32)))
    hdn = jnp.sin(freq[1] * (hdn @ w2.astype(F32) + b2.astype(F32)))
    filt = (hdn @ w3.astype(F32)).reshape(length, 2, HY_ORDER, W_GROUP)
    deltas = jnp.abs(jnp.linspace(math.log(HY_TARGET) / HY_SLOW_DECAY,
                                  math.log(HY_TARGET) / HY_FAST_DECAY, W_GROUP, dtype=F32))
    filt = filt * jnp.exp(-t * deltas)[:, None, None, :]
    fwd, bwd = filt[:, 0], filt[:, 1]
    two_sided = jnp.concatenate([fwd, jnp.zeros_like(fwd[:1]), jnp.flip(bwd[1:], 0)], 0)
    two_sided = two_sided * lax.rsqrt(jnp.sum(two_sided * two_sided, 0, keepdims=True) + EPS)
    return jnp.fft.rfft(two_sided, axis=0)


def _hyena(z, conv_w, conv_b, filt_params, bias):
    length = z.shape[1]
    u = _short_conv(z.astype(F32), conv_w.astype(F32), conv_b.astype(F32))
    v, x1, x2 = jnp.split(u, 3, -1)
    kf = _hyena_filters(length, *filt_params)
    bias = bias.astype(F32)
    y = v
    for n, gate in enumerate((x1, x2)):
        conv = jnp.fft.irfft(jnp.fft.rfft(y, n=2 * length, axis=1) * kf[:, n], n=2 * length, axis=1)[:, :length]
        y = gate * (conv + bias[n] * y)
    return y


def _gated_delta_chunked(q, k, v, log_a, beta, state):
    b, h, l, dk = q.shape
    dv = v.shape[-1]
    c = GD_CHUNK
    n = l // c
    q = q * dk ** -0.5
    q, k = (t.reshape(b, h, n, c, dk) for t in (q, k))
    v = v.reshape(b, h, n, c, dv)
    log_a, beta = (t.reshape(b, h, n, c) for t in (log_a, beta))
    g = jnp.cumsum(log_a, -1)
    causal = jnp.tril(jnp.ones((c, c), bool))
    strict = jnp.tril(jnp.ones((c, c), bool), -1)
    decay = jnp.exp(jnp.where(causal, g[..., :, None] - g[..., None, :], NEG))
    kb = k * beta[..., None]
    m = jnp.eye(c, dtype=q.dtype) + jnp.where(strict, jnp.einsum('bhnid,bhnjd->bhnij', kb, k) * decay, 0.0)
    rhs = jnp.concatenate([v * beta[..., None], kb * jnp.exp(g)[..., None]], -1)
    sol = lax.linalg.triangular_solve(m, rhs, left_side=True, lower=True, unit_diagonal=True)
    u, w = sol[..., :dv], sol[..., dv:]
    qk = jnp.einsum('bhnid,bhnjd->bhnij', q, k) * decay
    q_dec = q * jnp.exp(g)[..., None]
    g_last = g[..., -1]
    k_dec = k * jnp.exp(g_last[..., None] - g)[..., None]

    def step(s, inp):
        u_c, w_c, qk_c, qd_c, kd_c, gl_c = inp
        v_new = u_c - jnp.einsum('bhcd,bhde->bhce', w_c, s)
        o_c = jnp.einsum('bhcd,bhde->bhce', qd_c, s) + jnp.einsum('bhij,bhje->bhie', qk_c, v_new)
        s = jnp.exp(gl_c)[..., None, None] * s + jnp.einsum('bhcd,bhce->bhde', kd_c, v_new)
        return s, o_c

    xs = tuple(jnp.moveaxis(t, 2, 0) for t in (u, w, qk, q_dec, k_dec, g_last))
    s_fin, o = lax.scan(step, state, xs)
    return jnp.moveaxis(o, 0, 2).reshape(b, h, l, dv), s_fin


def _gdn(z, zg, zc, zcg, conv_w, a_log, dt_bias, norm_g):
    a_log = a_log.astype(F32)
    dt_bias = dt_bias.astype(F32)
    def prep(t, tg):
        qkv = jax.nn.silu(_short_conv(t[..., :3 * W_GROUP], conv_w.astype(F32)))
        q, k, v = jnp.split(qkv, 3, -1)
        q, k, v = _l2(_heads(q, GD_HEADS)), _l2(_heads(k, GD_HEADS)), _heads(v, GD_HEADS)
        gate = t[..., 3 * W_GROUP:4 * W_GROUP]
        ab = tg.reshape(t.shape[0], t.shape[1], 4, GD_HEADS).transpose(2, 0, 3, 1)
        dirs = []
        for d in range(2):
            log_a = -jnp.exp(a_log[d])[:, None] * jax.nn.softplus(ab[2 * d] + dt_bias[d][:, None])
            dirs.append((q, k, v, log_a, jax.nn.sigmoid(ab[2 * d + 1])))
        return dirs[0], dirs[1], gate
    lat_f, lat_b, gate = prep(z, zg)
    ctx_f, ctx_b, gate_c = prep(zc, zcg)
    s0 = jnp.zeros((z.shape[0], GD_HEADS, GD_DH, GD_DH), F32)
    o, o_c = _bidirectional(_gated_delta_chunked, ctx_f, ctx_b, lat_f, lat_b, s0)
    out = _head_rms(_unheads(o), norm_g, GD_HEADS) * jax.nn.silu(gate)
    out_c = _head_rms(_unheads(o_c), norm_g, GD_HEADS) * jax.nn.silu(gate_c)
    return out, out_c


def _permute_w_in(w_in):
    o_ml = HG_COLS
    o_hy = o_ml + ML_MAIN + ML_GATES
    o_gd = o_hy + HY_COLS
    parts = [w_in[:, :HG_COLS], w_in[:, o_ml:o_ml + ML_MAIN], w_in[:, o_hy:o_gd],
             w_in[:, o_gd:o_gd + GD_MAIN], w_in[:, o_ml + ML_MAIN:o_hy],
             w_in[:, o_gd + GD_MAIN:],
             jnp.zeros((w_in.shape[0], GATE_PAD - ML_GATES - GD_GATES), w_in.dtype)]
    return jnp.concatenate(parts, 1).astype(BF16)


IN_SPLITS = (HG_COLS, ML_MAIN, HY_COLS, GD_MAIN, GATE_PAD)


def kernel(x, c, ctx, c_ctx, mod_w, mod_b, norm1_g, norm2_g, w_in, w_out, hg_lb_logits, hg_norm_g,
           ml_gate_b, ml_norm_g, hy_conv_w, hy_conv_b, hy_w1, hy_b1, hy_w2, hy_b2, hy_freq, hy_w3,
           hy_bias, gd_conv_w, gd_a_log, gd_dt_bias, gd_norm_g, router_w, router_b, exp_w1, exp_b1,
           exp_w2, exp_b2, final_g):
    n_batch, seq, d = x.shape
    ctx_len = ctx.shape[1]
    depth = mod_w.shape[0]
    rows = seq // GRID_W
    p = jax.nn.softmax(hg_lb_logits.astype(F32), axis=0)
    lower_bounds = jnp.cumsum(p, 0) - p[0]
    s_all = jnp.concatenate([jax.nn.silu(c), jax.nn.silu(c_ctx)[None]], 0)
    xs = jnp.concatenate([x, ctx], 1)
    lat = lambda t: t[:, :seq]
    cx = lambda t: t[:, seq:]
    colmajor = lambda t: jnp.concatenate([_to_colmajor(lat(t), rows), cx(t)], 1)
    rowmajor = lambda t: jnp.concatenate([_from_colmajor(lat(t), rows), cx(t)], 1)
    for l in range(depth):
        mod = (s_all @ mod_w[l] + mod_b[l]).reshape(n_batch + 1, 6, 1, d)
        sh1, sc1, g1, sh2, sc2, g2 = (mod[:, i] for i in range(6))
        z_hg, z_ml, z_hy, z_gd, z_gt = _in_proj(xs, norm1_g[l], sc1, sh1,
                                                _permute_w_in(w_in[l]), IN_SPLITS)
        gd_g = z_gt[..., ML_GATES:ML_GATES + GD_GATES]
        hg_f, hg_b = _hgrn2_scan(z_hg, lower_bounds[l])
        ml_f, ml_b = _mlstm_scan(colmajor(z_ml), colmajor(z_gt), ml_gate_b[l], ctx_len)
        filt = (hy_w1[l], hy_b1[l], hy_w2[l], hy_b2[l], hy_freq[l], hy_w3[l])
        c_lat = _hyena(lat(z_hy), hy_conv_w[l], hy_conv_b[l], filt, hy_bias[l])
        c_ctx = _hyena(cx(z_hy), hy_conv_w[l], hy_conv_b[l], filt, hy_bias[l])
        d_lat, d_ctx = _gdn(lat(z_gd), lat(gd_g), cx(z_gd), cx(gd_g), gd_conv_w[l], gd_a_log[l],
                            gd_dt_bias[l], gd_norm_g[l])
        mixers = [("silu", hg_f, hg_b, (z_hg, 4)),
                  ("sigmoid", rowmajor(ml_f), rowmajor(ml_b), (z_ml, 3)),
                  ("final", jnp.concatenate([c_lat, c_ctx], 1)),
                  ("final", jnp.concatenate([d_lat, d_ctx], 1))]
        gains = jnp.stack([hg_norm_g[l], ml_norm_g[l], jnp.ones_like(ml_norm_g[l]), gd_norm_g[l]]).astype(F32)
        xs = _out_proj(xs, mixers, gains, g1, w_out[l].astype(BF16))
        h, logits = _moe_pre(xs, norm2_g[l], sc2, sh2, router_w[l], router_b[l])
        y = _moe(h.reshape(-1, d), logits.reshape(-1, GATE_PAD)[:, :N_EXPERTS],
                 exp_w1[l], exp_b1[l], exp_w2[l], exp_b2[l]).reshape(xs.shape)
        g2_rows = jnp.concatenate([jnp.broadcast_to(g2[:n_batch], (n_batch, seq, d)),
                                   jnp.broadcast_to(g2[n_batch:], (n_batch, ctx_len, d))], 1)
        xs = xs + g2_rows * y
    return _final_norm(xs, final_g, seq)
```

```python
import functools
import math

import jax
import jax.numpy as jnp
from jax import lax
from jax.experimental import pallas as pl
from jax.experimental.pallas import tpu as pltpu

F32 = jnp.float32
BF16 = jnp.bfloat16

D_MODEL = 1024
GRID_W = 64
N_MIXERS = 4
W_GROUP = D_MODEL // N_MIXERS
HG_HEADS = 4
HG_DK = W_GROUP // HG_HEADS
HG_CHUNK = 16
ML_HEADS = 4
ML_DH = W_GROUP // ML_HEADS
ML_CHUNK = 64
HY_ORDER = 2
HY_BANDS = 8
HY_FAST_DECAY = 0.3
HY_SLOW_DECAY = 1.5
HY_TARGET = 1e-2
GD_HEADS = 4
GD_DH = W_GROUP // GD_HEADS
GD_CHUNK = 64
N_EXPERTS = 32
TOP_K = 4
SWIGLU_LIMIT = 7.0
SWIGLU_ALPHA = 1.702
EPS = 1e-6
NEG = -1e30
HG_COLS = 5 * W_GROUP
ML_MAIN = 4 * W_GROUP
ML_GATES = 4 * ML_HEADS
HY_COLS = 3 * W_GROUP
GD_MAIN = 4 * W_GROUP
GD_GATES = 4 * GD_HEADS
GATE_PAD = 128

ROW_TILE = 256
MOE_TILE = 256
VMEM_LIMIT = 56 * 1024 * 1024


def _norm_mod(x, gain, sc, sh):
    y = x * lax.rsqrt(jnp.mean(x * x, -1, keepdims=True) + EPS)
    return y * gain * (1.0 + sc) + sh


def _in_proj_kernel(x_ref, g_ref, sc_ref, sh_ref, w_ref, *out_refs, splits):
    h = _norm_mod(x_ref[...], g_ref[...], sc_ref[...], sh_ref[...]).astype(BF16)
    off = 0
    for o_ref, n in zip(out_refs, splits):
        o_ref[...] = jnp.dot(h, w_ref[:, off:off + n], preferred_element_type=F32)
        off += n


def _mod_index_map(n_batch, n_tiles):
    return lambda b, j: (jnp.where(j == n_tiles - 1, n_batch, b), 0, 0)


def _in_proj(xs, gain, sc_tab, sh_tab, w, splits):
    n_batch, seq, d = xs.shape
    n = w.shape[1]
    grid = (n_batch, seq // ROW_TILE)
    mod_spec = pl.BlockSpec((None, 1, d), _mod_index_map(n_batch, seq // ROW_TILE))
    return pl.pallas_call(
        functools.partial(_in_proj_kernel, splits=splits),
        grid=grid,
        in_specs=[
            pl.BlockSpec((None, ROW_TILE, d), lambda b, j: (b, j, 0)),
            pl.BlockSpec((1, d), lambda b, j: (0, 0)),
            mod_spec, mod_spec,
            pl.BlockSpec((d, n), lambda b, j: (0, 0)),
        ],
        out_specs=[pl.BlockSpec((None, ROW_TILE, s), lambda b, j: (b, j, 0)) for s in splits],
        out_shape=[jax.ShapeDtypeStruct((n_batch, seq, s), F32) for s in splits],
        compiler_params=pltpu.CompilerParams(
            dimension_semantics=("parallel", "parallel"), vmem_limit_bytes=VMEM_LIMIT),
        name="in_proj",
    )(xs, gain.reshape(1, d), sc_tab, sh_tab, w)


def _moe_pre_kernel(x_ref, g_ref, sc_ref, sh_ref, rw_ref, rb_ref, h_ref, lg_ref):
    h = _norm_mod(x_ref[...], g_ref[...], sc_ref[...], sh_ref[...])
    h_ref[...] = h.astype(BF16)
    lg_ref[...] = jnp.dot(h, rw_ref[...], preferred_element_type=F32,
                          precision=lax.Precision.HIGHEST) + rb_ref[...]


def _moe_pre(xs, gain, sc_tab, sh_tab, router_w, router_b):
    n_batch, seq, d = xs.shape
    pad = GATE_PAD - N_EXPERTS
    rw = jnp.pad(router_w.astype(F32), ((0, 0), (0, pad)))
    rb = jnp.pad(router_b.astype(F32), (0, pad), constant_values=NEG).reshape(1, GATE_PAD)
    mod_spec = pl.BlockSpec((None, 1, d), _mod_index_map(n_batch, seq // ROW_TILE))
    return pl.pallas_call(
        _moe_pre_kernel,
        grid=(n_batch, seq // ROW_TILE),
        in_specs=[
            pl.BlockSpec((None, ROW_TILE, d), lambda b, j: (b, j, 0)),
            pl.BlockSpec((1, d), lambda b, j: (0, 0)),
            mod_spec, mod_spec,
            pl.BlockSpec((d, GATE_PAD), lambda b, j: (0, 0)),
            pl.BlockSpec((1, GATE_PAD), lambda b, j: (0, 0)),
        ],
        out_specs=[pl.BlockSpec((None, ROW_TILE, d), lambda b, j: (b, j, 0)),
                   pl.BlockSpec((None, ROW_TILE, GATE_PAD), lambda b, j: (b, j, 0))],
        out_shape=[jax.ShapeDtypeStruct((n_batch, seq, d), BF16),
                   jax.ShapeDtypeStruct((n_batch, seq, GATE_PAD), F32)],
        compiler_params=pltpu.CompilerParams(
            dimension_semantics=("parallel", "parallel"), vmem_limit_bytes=VMEM_LIMIT),
        name="moe_pre",
    )(xs, gain.reshape(1, d), sc_tab, sh_tab, rw, rb)


def _out_proj_kernel(*refs, kinds):
    x_ref, refs = refs[0], refs[1:]
    gains_ref, g_ref, heads_ref, w_ref, o_ref = refs[-5:]
    heads = heads_ref[...] * (1.0 / HG_DK)
    acc = jnp.zeros(o_ref.shape, F32)
    pos = 0
    for i, kind in enumerate(kinds):
        if kind == "final":
            m = refs[pos][...]
            pos += 1
        else:
            o = refs[pos][...] + refs[pos + 1][...]
            gate = refs[pos + 2][...]
            pos += 3
            ms = _dot_hi(o * o, heads)
            o = o * lax.rsqrt(ms + EPS) * gains_ref[i:i + 1, :]
            m = o * (jax.nn.silu(gate) if kind == "silu" else jax.nn.sigmoid(gate))
        acc += jnp.dot(m.astype(BF16), w_ref[i * W_GROUP:(i + 1) * W_GROUP, :],
                       preferred_element_type=F32)
    o_ref[...] = x_ref[...] + g_ref[...] * acc


def _out_proj(xs, mixers, gains, gate_tab, w_out):
    n_batch, seq, d = xs.shape
    row = lambda w, cb=0: pl.BlockSpec((None, ROW_TILE, w), lambda b, j: (b, j, cb))
    args, specs, kinds = [], [], []
    for m in mixers:
        kinds.append(m[0])
        if m[0] == "final":
            args.append(m[1])
            specs.append(row(W_GROUP))
        else:
            src, cb = m[3]
            args += [m[1], m[2], src]
            specs += [row(W_GROUP), row(W_GROUP), row(W_GROUP, cb)]
    heads = _hg_consts()[3]
    return pl.pallas_call(
        functools.partial(_out_proj_kernel, kinds=tuple(kinds)),
        grid=(n_batch, seq // ROW_TILE),
        in_specs=[row(d)] + specs + [
            pl.BlockSpec((N_MIXERS, W_GROUP), lambda b, j: (0, 0)),
            pl.BlockSpec((None, 1, d), _mod_index_map(n_batch, seq // ROW_TILE)),
            pl.BlockSpec((W_GROUP, W_GROUP), lambda b, j: (0, 0)),
            pl.BlockSpec((d, d), lambda b, j: (0, 0)),
        ],
        out_specs=row(d),
        out_shape=jax.ShapeDtypeStruct((n_batch, seq, d), F32),
        compiler_params=pltpu.CompilerParams(
            dimension_semantics=("parallel", "parallel"), vmem_limit_bytes=VMEM_LIMIT),
        name="out_proj",
    )(xs, *args, gains, gate_tab, heads, w_out)


def _expert_kernel(bexp_ref, x_ref, gate_ref, w1_ref, b1_ref, w2_ref, b2_ref, o_ref, w1b, w2b):
    i = pl.program_id(0)
    changed = jnp.logical_or(i == 0, bexp_ref[i] != bexp_ref[jnp.maximum(i - 1, 0)])

    @pl.when(changed)
    def _():
        w1b[...] = w1_ref[...].astype(BF16)
        w2b[...] = w2_ref[...].astype(BF16)

    f = w2b.shape[0]
    u = jnp.dot(x_ref[...], w1b[...], preferred_element_type=F32) + b1_ref[...]
    gate = jnp.minimum(u[:, :f], SWIGLU_LIMIT)
    lin = jnp.clip(u[:, f:], -SWIGLU_LIMIT, SWIGLU_LIMIT)
    y = (lin + 1.0) * gate * jax.nn.sigmoid(SWIGLU_ALPHA * gate)
    out = jnp.dot(y.astype(BF16), w2b[...], preferred_element_type=F32) + b2_ref[...]
    o_ref[...] = out * gate_ref[...]


def _expert_ffn(block_exp, xg, slot_gate, w1, b1, w2, b2):
    n_slots, d = xg.shape
    n_exp, _, f2 = w1.shape
    f = f2 // 2
    n_blocks = n_slots // MOE_TILE
    return pl.pallas_call(
        _expert_kernel,
        grid_spec=pltpu.PrefetchScalarGridSpec(
            num_scalar_prefetch=1,
            grid=(n_blocks,),
            in_specs=[
                pl.BlockSpec((MOE_TILE, d), lambda i, be: (i, 0)),
                pl.BlockSpec((MOE_TILE, 1), lambda i, be: (i, 0)),
                pl.BlockSpec((None, d, f2), lambda i, be: (be[i], 0, 0)),
                pl.BlockSpec((None, 1, f2), lambda i, be: (be[i], 0, 0)),
                pl.BlockSpec((None, f, d), lambda i, be: (be[i], 0, 0)),
                pl.BlockSpec((None, 1, d), lambda i, be: (be[i], 0, 0)),
            ],
            out_specs=pl.BlockSpec((MOE_TILE, d), lambda i, be: (i, 0)),
            scratch_shapes=[pltpu.VMEM((d, f2), BF16), pltpu.VMEM((f, d), BF16)],
        ),
        out_shape=jax.ShapeDtypeStruct((n_slots, d), F32),
        compiler_params=pltpu.CompilerParams(
            dimension_semantics=("arbitrary",), vmem_limit_bytes=VMEM_LIMIT),
        name="expert_ffn",
    )(block_exp, xg, slot_gate, w1, b1.reshape(n_exp, 1, f2), w2, b2.reshape(n_exp, 1, d))


def _moe(h, logits, w1, b1, w2, b2):
    t_count, d = h.shape
    top_v, top_i = lax.top_k(logits, TOP_K)
    gates = jax.nn.softmax(top_v, -1)
    n_pairs = t_count * TOP_K
    e_flat = top_i.reshape(-1)
    order = jnp.argsort(e_flat)
    e_sorted = e_flat[order]
    counts = jnp.bincount(e_flat, length=N_EXPERTS)
    padded = (counts + MOE_TILE - 1) // MOE_TILE * MOE_TILE
    start = jnp.cumsum(counts) - counts
    pend = jnp.cumsum(padded)
    pstart = pend - padded
    dest = pstart[e_sorted] + (jnp.arange(n_pairs, dtype=jnp.int32) - start[e_sorted])
    n_blocks = -(-(n_pairs + N_EXPERTS * (MOE_TILE - 1)) // MOE_TILE)
    n_slots = n_blocks * MOE_TILE
    pos = jnp.zeros((n_pairs,), jnp.int32).at[order].set(dest.astype(jnp.int32))
    slot_tok = jnp.zeros((n_slots,), jnp.int32).at[pos].set(
        jnp.repeat(jnp.arange(t_count, dtype=jnp.int32), TOP_K))
    slot_gate = jnp.zeros((n_slots,), F32).at[pos].set(gates.reshape(-1))
    block_exp = jnp.clip(jnp.searchsorted(pend, jnp.arange(n_blocks) * MOE_TILE, side='right'),
                         0, N_EXPERTS - 1).astype(jnp.int32)
    y = _expert_ffn(block_exp, h[slot_tok], slot_gate[:, None], w1, b1, w2, b2)
    return jnp.sum(y[pos.reshape(t_count, TOP_K)], 1)


def _final_norm_kernel(x_ref, g_ref, o_ref):
    x = x_ref[...]
    o_ref[...] = x * lax.rsqrt(jnp.mean(x * x, -1, keepdims=True) + EPS) * g_ref[...]


def _final_norm(xs, gain, n_lat):
    n_batch, _, d = xs.shape
    return pl.pallas_call(
        _final_norm_kernel,
        grid=(n_batch, n_lat // ROW_TILE),
        in_specs=[pl.BlockSpec((None, ROW_TILE, d), lambda b, j: (b, j, 0)),
                  pl.BlockSpec((1, d), lambda b, j: (0, 0))],
        out_specs=pl.BlockSpec((None, ROW_TILE, d), lambda b, j: (b, j, 0)),
        out_shape=jax.ShapeDtypeStruct((n_batch, n_lat, d), F32),
        compiler_params=pltpu.CompilerParams(dimension_semantics=("parallel", "parallel")),
        name="final_norm",
    )(xs, gain.reshape(1, d))


HG_SUB = 16


def _hg_consts():
    r = lax.broadcasted_iota(jnp.int32, (ROW_TILE, ROW_TILE), 0)
    c = lax.broadcasted_iota(jnp.int32, (ROW_TILE, ROW_TILE), 1)
    same_chunk = (r // HG_SUB) == (c // HG_SUB)
    tri_f = jnp.where(same_chunk & (c <= r), 1.0, 0.0).astype(F32)
    tri_b = jnp.where(same_chunk & (c >= r), 1.0, 0.0).astype(F32)
    blk = jnp.where(same_chunk, 1.0, 0.0).astype(F32)
    heads = jnp.where((r // HG_DK) == (c // HG_DK), 1.0, 0.0).astype(F32)
    return tri_f, tri_b, blk, heads


def _hg_prepare(z_ref, zf_col, lb, tri, blk, qs_s, key_s, cm_s, qh_s, kh_s, dec_s):
    q = jax.nn.silu(z_ref[:, 0:W_GROUP])
    zf = z_ref[:, zf_col:zf_col + W_GROUP]
    f = lb + (1.0 - lb) * jax.nn.sigmoid(zf)
    key = (1.0 - lb) * jax.nn.sigmoid(-zf)
    logf = jnp.log(f)
    cum = jnp.dot(tri, logf, preferred_element_type=F32, precision=lax.Precision.HIGHEST)
    tot = jnp.dot(blk, logf, preferred_element_type=F32, precision=lax.Precision.HIGHEST)
    qs_s[...] = q
    key_s[...] = key
    cm_s[...] = cum
    qh_s[...] = (q * jnp.exp(cum)).astype(BF16)
    kh_s[...] = (key * jnp.exp(tot - cum)).astype(BF16)
    dec_s[...] = jnp.exp(tot)


def _hg_chunk(c, reverse, z_ref, o_ref, st_ref, heads_bf, heads_f, qs_s, key_s, cm_s, qh_s, kh_s, dec_s):
    sl = pl.ds(pl.multiple_of(c * HG_SUB, HG_SUB), HG_SUB)
    q, k, cm = qs_s[sl, :], key_s[sl, :], cm_s[sl, :]
    v = z_ref[sl, W_GROUP:2 * W_GROUP]
    st = st_ref[...]
    o = lax.dot_general(qh_s[sl, :], st.astype(BF16), (((1,), (1,)), ((), ())),
                        preferred_element_type=F32)
    t_idx = lax.broadcasted_iota(jnp.int32, (HG_SUB, W_GROUP), 0)
    parts = []
    for s in range(HG_SUB):
        live = (t_idx <= s) if reverse else (t_idx >= s)
        e = jnp.exp(jnp.where(live, cm - cm[s:s + 1, :], NEG))
        parts.append((q * k[s:s + 1, :] * e).astype(BF16))
    r = jnp.dot(jnp.concatenate(parts, 0), heads_bf, preferred_element_type=F32)
    for s in range(HG_SUB):
        o += r[s * HG_SUB:(s + 1) * HG_SUB, :] * v[s:s + 1, :]
    o_ref[sl, :] = o
    ut = lax.dot_general(v.astype(BF16), kh_s[sl, :], (((0,), (0,)), ((), ())),
                         preferred_element_type=F32)
    st_ref[...] = st * dec_s[sl, :][0:1, :] + ut * heads_f


def _hgrn2_kernel(zf_ref, zb_ref, lb_ref, trif_ref, trib_ref, blk_ref, heads_ref, of_ref, ob_ref,
                  st_s, qs_s, key_s, cm_s, qh_s, kh_s, dec_s):
    @pl.when(pl.program_id(1) == 0)
    def _():
        st_s[...] = jnp.zeros_like(st_s)

    lb = lb_ref[...]
    blk = blk_ref[...]
    heads_f = heads_ref[...]
    heads_bf = heads_f.astype(BF16)
    _hg_prepare(zf_ref, 2 * W_GROUP, lb, trif_ref[...], blk, qs_s.at[0], key_s.at[0], cm_s.at[0],
                qh_s.at[0], kh_s.at[0], dec_s.at[0])
    _hg_prepare(zb_ref, 3 * W_GROUP, lb, trib_ref[...], blk, qs_s.at[1], key_s.at[1], cm_s.at[1],
                qh_s.at[1], kh_s.at[1], dec_s.at[1])
    n_chunks = ROW_TILE // HG_SUB

    def body(c, carry):
        _hg_chunk(c, False, zf_ref, of_ref, st_s.at[0], heads_bf, heads_f, qs_s.at[0], key_s.at[0],
                  cm_s.at[0], qh_s.at[0], kh_s.at[0], dec_s.at[0])
        _hg_chunk(n_chunks - 1 - c, True, zb_ref, ob_ref, st_s.at[1], heads_bf, heads_f, qs_s.at[1],
                  key_s.at[1], cm_s.at[1], qh_s.at[1], kh_s.at[1], dec_s.at[1])
        return carry

    lax.fori_loop(0, n_chunks, body, 0)


def _scan_tiles(n_lat_tiles):
    fwd = lambda j: jnp.where(j == 0, n_lat_tiles, j - 1)
    bwd = lambda j: jnp.where(j == 0, n_lat_tiles, n_lat_tiles - j)
    return fwd, bwd


def _hgrn2_scan(z_hg, lb):
    n_batch, seq, _ = z_hg.shape
    n_tiles = seq // ROW_TILE
    fwd, bwd = _scan_tiles(n_tiles - 1)
    const = pl.BlockSpec((ROW_TILE, ROW_TILE), lambda b, j: (0, 0))
    scr = lambda dt: pltpu.VMEM((2, ROW_TILE, W_GROUP), dt)
    return pl.pallas_call(
        _hgrn2_kernel,
        grid=(n_batch, n_tiles),
        in_specs=[pl.BlockSpec((None, ROW_TILE, HG_COLS), lambda b, j: (b, fwd(j), 0)),
                  pl.BlockSpec((None, ROW_TILE, HG_COLS), lambda b, j: (b, bwd(j), 0)),
                  pl.BlockSpec((1, W_GROUP), lambda b, j: (0, 0)),
                  const, const, const, const],
        out_specs=[pl.BlockSpec((None, ROW_TILE, W_GROUP), lambda b, j: (b, fwd(j), 0)),
                   pl.BlockSpec((None, ROW_TILE, W_GROUP), lambda b, j: (b, bwd(j), 0))],
        out_shape=[jax.ShapeDtypeStruct((n_batch, seq, W_GROUP), F32)] * 2,
        scratch_shapes=[pltpu.VMEM((2, W_GROUP, W_GROUP), F32), scr(F32), scr(F32), scr(F32),
                        scr(BF16), scr(BF16), scr(F32)],
        compiler_params=pltpu.CompilerParams(
            dimension_semantics=("parallel", "arbitrary"), vmem_limit_bytes=VMEM_LIMIT),
        name="hgrn2_scan",
    )(z_hg, z_hg, lb.reshape(1, W_GROUP), *_hg_consts())


def _dot_hi(a, b):
    return jnp.dot(a, b, preferred_element_type=F32, precision=lax.Precision.HIGHEST)


def _dot_nt(a, b, precision=None):
    return lax.dot_general(a, b, (((1,), (1,)), ((), ())), preferred_element_type=F32,
                           precision=precision)


def _dot_tn(a, b):
    return lax.dot_general(a, b, (((0,), (0,)), ((), ())), preferred_element_type=F32)


def _chunk_consts(chunk, n_heads, gate_cols):
    r = lax.broadcasted_iota(jnp.int32, (chunk, chunk), 0)
    c = lax.broadcasted_iota(jnp.int32, (chunk, chunk), 1)
    low = (c <= r).astype(F32)
    upp = (c >= r).astype(F32)
    dh = W_GROUP // n_heads
    hr = lax.broadcasted_iota(jnp.int32, (W_GROUP, W_GROUP), 0) // dh
    hc = lax.broadcasted_iota(jnp.int32, (W_GROUP, W_GROUP), 1) // dh
    heads = (hr == hc).astype(F32)
    gc = lax.broadcasted_iota(jnp.int32, (GATE_PAD, W_GROUP), 0)
    gh = lax.broadcasted_iota(jnp.int32, (GATE_PAD, W_GROUP), 1) // dh
    expand = jnp.stack([(gc == g0 + gh).astype(F32) for g0 in gate_cols])
    sr = lax.broadcasted_iota(jnp.int32, (8, GATE_PAD), 0)
    sc = lax.broadcasted_iota(jnp.int32, (8, GATE_PAD), 1)
    select = jnp.stack([((sc == g0 + sr) & (sr < n_heads)).astype(F32) for g0 in gate_cols])
    return low, upp, heads, expand, select


def _ml_direction(reverse, z_ref, g_ref, gb, o_ref, c_ref, n_ref, m_ref, low, upp, heads,
                  ex_i, ex_f, sel_i, sel_f):
    q = z_ref[:, 0:W_GROUP]
    k = z_ref[:, W_GROUP:2 * W_GROUP] * (ML_DH ** -0.5)
    v = z_ref[:, 2 * W_GROUP:3 * W_GROUP]
    gates = g_ref[...] + gb
    col_sum, row_sum = (upp, low) if reverse else (low, upp)
    li = _dot_hi(gates, ex_i)
    lf = jax.nn.log_sigmoid(_dot_hi(gates, ex_f))
    li_rows = _dot_nt(sel_i, gates, lax.Precision.HIGHEST)
    lf_rows = jax.nn.log_sigmoid(_dot_nt(sel_f, gates, lax.Precision.HIGHEST))
    fcum = _dot_hi(col_sum, lf)
    fcum_rows = _dot_hi(lf_rows, row_sum)
    last = 0 if reverse else ML_CHUNK - 1
    ftot = fcum[last:last + 1, :]
    g = ftot - fcum + li
    g_max = jnp.max(g, axis=0, keepdims=True)
    kw = k * jnp.exp(g - g_max)
    c0, n0, m0 = c_ref[...], n_ref[...], m_ref[...]
    q_bf = q.astype(BF16)
    k_bf = k.astype(BF16)
    qc = jnp.dot(q_bf, c0.astype(BF16), preferred_element_type=F32)
    qn = jnp.dot((q * n0).astype(BF16), heads.astype(BF16), preferred_element_type=F32)
    a_init = fcum + m0
    rr = lax.broadcasted_iota(jnp.int32, (ML_CHUNK, ML_CHUNK), 0)
    cc = lax.broadcasted_iota(jnp.int32, (ML_CHUNK, ML_CHUNK), 1)
    live = (cc >= rr) if reverse else (cc <= rr)
    lane_head = lax.broadcasted_iota(jnp.int32, (ML_CHUNK, W_GROUP), 1) // ML_DH
    outs = []
    for h in range(ML_HEADS):
        hs = slice(h * ML_DH, (h + 1) * ML_DH)
        a = jnp.where(live, fcum[:, hs] - fcum_rows[h:h + 1, :] + li_rows[h:h + 1, :], NEG)
        a_init_h = a_init[:, hs]
        m_t = jnp.maximum(jnp.max(a, axis=1, keepdims=True), a_init_h)
        qk = _dot_nt(jnp.where(lane_head == h, q, 0.0).astype(BF16), k_bf)
        p = jnp.exp(a - m_t) * qk
        e0 = jnp.exp(a_init_h - m_t)
        num = jnp.dot(p.astype(BF16), v[:, hs].astype(BF16), preferred_element_type=F32) + e0 * qc[:, hs]
        den = jnp.sum(p, axis=1, keepdims=True) + e0 * qn[:, hs]
        outs.append(num / jnp.maximum(jnp.abs(den), jnp.exp(-m_t)))
    o_ref[...] = jnp.concatenate(outs, axis=1)
    m_new = jnp.maximum(ftot + m0, g_max)
    a_old = jnp.exp(ftot + m0 - m_new)
    a_new = jnp.exp(g_max - m_new)
    u_c = _dot_tn(kw.astype(BF16), v.astype(BF16)) * heads
    c_ref[...] = a_old * c0 + a_new * u_c
    n_ref[...] = a_old * n0 + a_new * jnp.sum(kw, axis=0, keepdims=True)
    m_ref[...] = m_new


def _mlstm_kernel(zf_ref, zb_ref, gf_ref, gb_ref, bias_ref, low_ref, upp_ref, heads_ref, ex_ref, sel_ref,
                  of_ref, ob_ref, c_s, n_s, m_s):
    @pl.when(pl.program_id(1) == 0)
    def _():
        c_s[...] = jnp.zeros_like(c_s)
        n_s[...] = jnp.zeros_like(n_s)
        m_s[...] = jnp.zeros_like(m_s)

    low, upp, heads, bias = low_ref[...], upp_ref[...], heads_ref[...], bias_ref[...]
    _ml_direction(False, zf_ref, gf_ref, bias, of_ref, c_s.at[0], n_s.at[0], m_s.at[0], low, upp, heads,
                  ex_ref[0], ex_ref[1], sel_ref[0], sel_ref[1])
    _ml_direction(True, zb_ref, gb_ref, bias, ob_ref, c_s.at[1], n_s.at[1], m_s.at[1], low, upp, heads,
                  ex_ref[2], ex_ref[3], sel_ref[2], sel_ref[3])


def _scan_chunks(n_lat, n_ctx):
    fwd = lambda j: jnp.where(j < n_ctx, n_lat + j, j - n_ctx)
    bwd = lambda j: jnp.where(j < n_ctx, n_lat + n_ctx - 1 - j, n_lat + n_ctx - 1 - j)
    return fwd, bwd


def _mlstm_scan(z_ml, z_gt, gate_b, ctx_len):
    n_batch, seq, _ = z_ml.shape
    n_chunks = seq // ML_CHUNK
    n_ctx = ctx_len // ML_CHUNK
    fwd, bwd = _scan_chunks(n_chunks - n_ctx, n_ctx)
    low, upp, heads, expand, select = _chunk_consts(ML_CHUNK, ML_HEADS, (0, 4, 8, 12))
    bias = jnp.pad(gate_b.astype(F32).reshape(1, ML_GATES), ((0, 0), (0, GATE_PAD - ML_GATES)))
    full = lambda a: pl.BlockSpec(a.shape, lambda b, j: (0,) * a.ndim)
    return pl.pallas_call(
        _mlstm_kernel,
        grid=(n_batch, n_chunks),
        in_specs=[pl.BlockSpec((None, ML_CHUNK, ML_MAIN), lambda b, j: (b, fwd(j), 0)),
                  pl.BlockSpec((None, ML_CHUNK, ML_MAIN), lambda b, j: (b, bwd(j), 0)),
                  pl.BlockSpec((None, ML_CHUNK, GATE_PAD), lambda b, j: (b, fwd(j), 0)),
                  pl.BlockSpec((None, ML_CHUNK, GATE_PAD), lambda b, j: (b, bwd(j), 0)),
                  full(bias), full(low), full(upp), full(heads), full(expand), full(select)],
        out_specs=[pl.BlockSpec((None, ML_CHUNK, W_GROUP), lambda b, j: (b, fwd(j), 0)),
                   pl.BlockSpec((None, ML_CHUNK, W_GROUP), lambda b, j: (b, bwd(j), 0))],
        out_shape=[jax.ShapeDtypeStruct((n_batch, seq, W_GROUP), F32)] * 2,
        scratch_shapes=[pltpu.VMEM((2, W_GROUP, W_GROUP), F32), pltpu.VMEM((2, 1, W_GROUP), F32),
                        pltpu.VMEM((2, 1, W_GROUP), F32)],
        compiler_params=pltpu.CompilerParams(
            dimension_semantics=("parallel", "arbitrary"), vmem_limit_bytes=VMEM_LIMIT),
        name="mlstm_scan",
    )(z_ml, z_ml, z_gt, z_gt, bias, low, upp, heads, expand, select)


HALO = 8


def _short_conv_kernel(x_ref, prev_ref, next_ref, w_ref, b_ref, heads_ref, o_ref, *, n_lat_tiles, gdn):
    j = pl.program_id(1)
    n_tiles = pl.num_programs(1)
    x = x_ref[...]
    has_prev = jnp.logical_and(j != 0, j != n_lat_tiles)
    has_next = jnp.logical_and(j != n_lat_tiles - 1, j != n_tiles - 1)
    prev_row = jnp.where(has_prev, prev_ref[HALO - 1:HALO, :], 0.0)
    next_row = jnp.where(has_next, next_ref[0:1, :], 0.0)
    row = lax.broadcasted_iota(jnp.int32, x.shape, 0)
    x_prev = jnp.where(row == 0, prev_row, pltpu.roll(x, 1, 0))
    x_next = jnp.where(row == ROW_TILE - 1, next_row, pltpu.roll(x, ROW_TILE - 1, 0))
    y = x_prev * w_ref[0:1, :] + x * w_ref[1:2, :] + x_next * w_ref[2:3, :] + b_ref[...]
    if not gdn:
        o_ref[...] = y
        return
    y = jax.nn.silu(y)
    heads = heads_ref[...]
    for i in range(2):
        t = y[:, i * W_GROUP:(i + 1) * W_GROUP]
        o_ref[:, i * W_GROUP:(i + 1) * W_GROUP] = t * lax.rsqrt(_dot_hi(t * t, heads) + EPS)
    o_ref[:, 2 * W_GROUP:] = y[:, 2 * W_GROUP:]


def _head_mask():
    r = lax.broadcasted_iota(jnp.int32, (W_GROUP, W_GROUP), 0) // HG_DK
    c = lax.broadcasted_iota(jnp.int32, (W_GROUP, W_GROUP), 1) // HG_DK
    return (r == c).astype(F32)


def _short_conv_tiles(z, conv_w, conv_b, n_lat, gdn):
    n_batch, seq, _ = z.shape
    c = conv_w.shape[1]
    n_tiles = seq // ROW_TILE
    per = ROW_TILE // HALO
    n_halo = seq // HALO
    return pl.pallas_call(
        functools.partial(_short_conv_kernel, n_lat_tiles=n_lat // ROW_TILE, gdn=gdn),
        grid=(n_batch, n_tiles),
        in_specs=[pl.BlockSpec((None, ROW_TILE, c), lambda b, j: (b, j, 0)),
                  pl.BlockSpec((None, HALO, c), lambda b, j: (b, jnp.maximum(j * per - 1, 0), 0)),
                  pl.BlockSpec((None, HALO, c), lambda b, j: (b, jnp.minimum((j + 1) * per, n_halo - 1), 0)),
                  pl.BlockSpec((3, c), lambda b, j: (0, 0)),
                  pl.BlockSpec((1, c), lambda b, j: (0, 0)),
                  pl.BlockSpec((W_GROUP, W_GROUP), lambda b, j: (0, 0))],
        out_specs=pl.BlockSpec((None, ROW_TILE, c), lambda b, j: (b, j, 0)),
        out_shape=jax.ShapeDtypeStruct((n_batch, seq, c), F32),
        compiler_params=pltpu.CompilerParams(
            dimension_semantics=("parallel", "parallel"), vmem_limit_bytes=VMEM_LIMIT),
        name="gdn_conv" if gdn else "hyena_conv",
    )(z, z, z, conv_w.astype(F32), conv_b.astype(F32).reshape(1, c), _head_mask())


def _gd_consts():
    r = lax.broadcasted_iota(jnp.int32, (GD_CHUNK, GD_CHUNK), 0)
    c = lax.broadcasted_iota(jnp.int32, (GD_CHUNK, GD_CHUNK), 1)
    same = lambda n: (r // n) == (c // n)
    eye = (r == c).astype(F32)
    masks = [same(8).astype(F32)] + [(same(2 * b) & ~same(b)).astype(F32) for b in (8, 16, 32)]
    return eye, jnp.stack(masks)


def _unit_tri_inverses(ns, eye, masks):
    mm = lambda a, b: jnp.dot(a.astype(BF16), b.astype(BF16), preferred_element_type=F32)
    n8 = [n * masks[0] for n in ns]
    n2 = [mm(a, a) for a in n8]
    n4 = [mm(a, a) for a in n2]
    ts = [mm(eye - a, eye + b) for a, b in zip(n8, n2)]
    ts = [mm(t, eye + b) for t, b in zip(ts, n4)]
    for i in range(1, 4):
        lts = [mm(n * masks[i], t) for n, t in zip(ns, ts)]
        ts = [t - mm(t, lt) for t, lt in zip(ts, lts)]
    return ts


class _GdDir:
    def __init__(self, reverse, x_ref, g_ref, s_ref, prm_lane, prm_row, low, upp, ex_a, ex_b, sel_a):
        self.q = x_ref[:, 0:W_GROUP] * (GD_DH ** -0.5)
        self.k = x_ref[:, W_GROUP:2 * W_GROUP]
        self.v = x_ref[:, 2 * W_GROUP:3 * W_GROUP]
        gates = g_ref[...]
        col_sum, row_sum = (upp, low) if reverse else (low, upp)
        log_a = prm_lane[0:1, :] * jax.nn.softplus(_dot_hi(gates, ex_a) + prm_lane[1:2, :])
        self.beta = jax.nn.sigmoid(_dot_hi(gates, ex_b))
        log_a_rows = prm_row[0] * jax.nn.softplus(_dot_nt(sel_a, gates, lax.Precision.HIGHEST) + prm_row[1])
        self.g = _dot_hi(col_sum, log_a)
        self.g_rows = _dot_hi(log_a_rows, row_sum)
        last = 0 if reverse else GD_CHUNK - 1
        self.g_last = self.g[last:last + 1, :]
        self.eg = jnp.exp(self.g)
        self.k_bf = self.k.astype(BF16)
        self.kb = self.k * self.beta
        self.s0 = s_ref[...]
        rr = lax.broadcasted_iota(jnp.int32, (GD_CHUNK, GD_CHUNK), 0)
        cc = lax.broadcasted_iota(jnp.int32, (GD_CHUNK, GD_CHUNK), 1)
        self.live = (cc >= rr) if reverse else (cc <= rr)
        self.strict = (cc > rr) if reverse else (cc < rr)


def _gdn_kernel(xf_ref, xb_ref, gf_ref, gb_ref, prm_lane_ref, prm_row_ref, low_ref, upp_ref, heads_ref,
                ex_ref, sel_ref, eye_ref, masks_ref, of_ref, ob_ref, s_s):
    @pl.when(pl.program_id(1) == 0)
    def _():
        s_s[...] = jnp.zeros_like(s_s)

    low, upp, heads, eye, masks = low_ref[...], upp_ref[...], heads_ref[...], eye_ref[...], masks_ref[...]
    dirs = [_GdDir(False, xf_ref, gf_ref, s_s.at[0], prm_lane_ref[0], prm_row_ref[0], low, upp,
                   ex_ref[0], ex_ref[1], sel_ref[0]),
            _GdDir(True, xb_ref, gb_ref, s_s.at[1], prm_lane_ref[1], prm_row_ref[1], low, upp,
                   ex_ref[2], ex_ref[3], sel_ref[2])]
    lane_head = lax.broadcasted_iota(jnp.int32, (GD_CHUNK, W_GROUP), 1) // GD_DH
    probs = [(d, h) for d in dirs for h in range(GD_HEADS)]
    only = lambda h, t: jnp.where(lane_head == h, t, 0.0).astype(BF16)
    hs = lambda h: slice(h * GD_DH, (h + 1) * GD_DH)
    decay = [jnp.exp(jnp.where(d.live, d.g[:, hs(h)] - d.g_rows[h:h + 1, :], NEG)) for d, h in probs]
    kks = [_dot_nt(only(h, d.kb), d.k_bf) for d, h in probs]
    ns = [jnp.where(d.strict, kk * dc, 0.0) for (d, h), kk, dc in zip(probs, kks, decay)]
    t_inv = [t.astype(BF16) for t in _unit_tri_inverses(ns, eye, masks)]
    us = [jnp.dot(t, only(h, d.v * d.beta), preferred_element_type=F32) for (d, h), t in zip(probs, t_inv)]
    ws = [jnp.dot(t, only(h, d.kb * d.eg), preferred_element_type=F32) for (d, h), t in zip(probs, t_inv)]
    qks = [(_dot_nt(only(h, d.q), d.k_bf) * dc).astype(BF16) for (d, h), dc in zip(probs, decay)]
    v_new, o = [], []
    for i, d in enumerate(dirs):
        sl = slice(i * GD_HEADS, (i + 1) * GD_HEADS)
        s_bf = d.s0.astype(BF16)
        v_new.append(sum(us[sl]) - jnp.dot(sum(ws[sl]).astype(BF16), s_bf, preferred_element_type=F32))
        o.append(jnp.dot((d.q * d.eg).astype(BF16), s_bf, preferred_element_type=F32))
    intra = [jnp.dot(qk, only(h, v_new[i // GD_HEADS]), preferred_element_type=F32)
             for i, ((d, h), qk) in enumerate(zip(probs, qks))]
    for i, (d, o_ref) in enumerate(zip(dirs, (of_ref, ob_ref))):
        o_ref[...] = o[i] + sum(intra[i * GD_HEADS:(i + 1) * GD_HEADS])
        k_dec = d.k * jnp.exp(d.g_last - d.g)
        s_s[i] = jnp.exp(d.g_last) * d.s0 + _dot_tn(k_dec.astype(BF16), v_new[i].astype(BF16)) * heads


def _gdn_scan(qkv, z_gt, a_log, dt_bias, ctx_len):
    n_batch, seq, _ = qkv.shape
    n_chunks = seq // GD_CHUNK
    n_ctx = ctx_len // GD_CHUNK
    fwd, bwd = _scan_chunks(n_chunks - n_ctx, n_ctx)
    g0 = ML_GATES
    low, upp, heads, expand, select = _chunk_consts(GD_CHUNK, GD_HEADS, (g0, g0 + 4, g0 + 8, g0 + 12))
    eye, masks = _gd_consts()
    prm = jnp.stack([-jnp.exp(a_log.astype(F32)), dt_bias.astype(F32)], 1)
    prm_lane = jnp.repeat(prm, GD_DH, axis=2)
    prm_row = jnp.broadcast_to(jnp.pad(prm, ((0, 0), (0, 0), (0, 8 - GD_HEADS)))[..., None],
                               (2, 2, 8, GD_CHUNK))
    full = lambda a: pl.BlockSpec(a.shape, lambda b, j: (0,) * a.ndim)
    return pl.pallas_call(
        _gdn_kernel,
        grid=(n_batch, n_chunks),
        in_specs=[pl.BlockSpec((None, GD_CHUNK, 3 * W_GROUP), lambda b, j: (b, fwd(j), 0)),
                  pl.BlockSpec((None, GD_CHUNK, 3 * W_GROUP), lambda b, j: (b, bwd(j), 0)),
                  pl.BlockSpec((None, GD_CHUNK, GATE_PAD), lambda b, j: (b, fwd(j), 0)),
                  pl.BlockSpec((None, GD_CHUNK, GATE_PAD), lambda b, j: (b, bwd(j), 0)),
                  full(prm_lane), full(prm_row), full(low), full(upp), full(heads), full(expand),
                  full(select), full(eye), full(masks)],
        out_specs=[pl.BlockSpec((None, GD_CHUNK, W_GROUP), lambda b, j: (b, fwd(j), 0)),
                   pl.BlockSpec((None, GD_CHUNK, W_GROUP), lambda b, j: (b, bwd(j), 0))],
        out_shape=[jax.ShapeDtypeStruct((n_batch, seq, W_GROUP), F32)] * 2,
        scratch_shapes=[pltpu.VMEM((2, W_GROUP, W_GROUP), F32)],
        compiler_params=pltpu.CompilerParams(
            dimension_semantics=("parallel", "arbitrary"), vmem_limit_bytes=VMEM_LIMIT),
        name="gdn_scan",
    )(qkv, qkv, z_gt, z_gt, prm_lane, prm_row, low, upp, heads, expand, select, eye, masks)


HY_FILT_TILE = 512
DFT_TILE = 512


def _dft_matrices(length):
    n = 2 * length
    k = lax.broadcasted_iota(jnp.int32, (length, length), 0)
    t = lax.broadcasted_iota(jnp.int32, (length, length), 1)
    ang = ((k * t) % n).astype(F32) * (2.0 * math.pi / n)
    f_re = jnp.cos(ang)
    f_im = jnp.where(k == 0, jnp.where(t % 2 == 0, 1.0, -1.0), -jnp.sin(ang))
    c = jnp.where(k == 0, 1.0 / n, 2.0 / n)
    return (f_re.astype(BF16), f_im.astype(BF16), (c * f_re).T.astype(BF16), (c * f_im).T.astype(BF16))


def _hy_filter_kernel(feat_ref, w1_ref, b1_ref, w2_ref, b2_ref, fr_ref, w3_ref, win_ref, filt_ref, ss_ref):
    i = pl.program_id(0)
    hdn = jnp.sin(fr_ref[0:1, :] * (_dot_hi(feat_ref[...], w1_ref[...]) + b1_ref[...]))
    hdn = jnp.sin(fr_ref[1:2, :] * (_dot_hi(hdn, w2_ref[...]) + b2_ref[...]))
    filt = _dot_hi(hdn, w3_ref[...]) * win_ref[...]
    half = filt.shape[1] // 2
    row = lax.broadcasted_iota(jnp.int32, filt.shape, 0) + i * filt.shape[0]
    col = lax.broadcasted_iota(jnp.int32, filt.shape, 1)
    filt = jnp.where(jnp.logical_and(row == 0, col >= half), 0.0, filt)
    filt_ref[...] = filt.astype(BF16)
    sq = jnp.sum(filt * filt, axis=0, keepdims=True)

    @pl.when(i == 0)
    def _():
        ss_ref[...] = jnp.zeros_like(ss_ref)

    ss_ref[...] += sq[:, :half] + sq[:, half:]


def _hyena_filters(length, w1, b1, w2, b2, freq, w3):
    tile = min(HY_FILT_TILE, length)
    t = jnp.linspace(0.0, 1.0, length, dtype=F32)[:, None]
    pos = jnp.arange(length, dtype=F32)[:, None]
    band = jnp.linspace(1e-4, HY_BANDS - 1, HY_BANDS, dtype=F32)[None, :]
    ang = 2.0 * math.pi * pos * band / length
    feats = jnp.concatenate([t, jnp.cos(ang), -jnp.sin(ang)], -1)
    deltas = jnp.abs(jnp.linspace(math.log(HY_TARGET) / HY_SLOW_DECAY,
                                  math.log(HY_TARGET) / HY_FAST_DECAY, W_GROUP, dtype=F32))
    window = jnp.tile(jnp.exp(-t * deltas), (1, 2 * HY_ORDER))
    n_out = w3.shape[1]
    emb, ffn = w1.shape
    full = lambda a: pl.BlockSpec(a.shape, lambda i: (0,) * a.ndim)
    args = [w1.astype(F32), b1.astype(F32).reshape(1, ffn), w2.astype(F32), b2.astype(F32).reshape(1, ffn),
            freq.astype(F32), w3.astype(F32)]
    filt, ss = pl.pallas_call(
        _hy_filter_kernel,
        grid=(length // tile,),
        in_specs=[pl.BlockSpec((tile, emb), lambda i: (i, 0))] + [full(a) for a in args] +
                 [pl.BlockSpec((tile, n_out), lambda i: (i, 0))],
        out_specs=[pl.BlockSpec((tile, n_out), lambda i: (i, 0)),
                   pl.BlockSpec((1, n_out // 2), lambda i: (0, 0))],
        out_shape=[jax.ShapeDtypeStruct((length, n_out), BF16),
                   jax.ShapeDtypeStruct((1, n_out // 2), F32)],
        compiler_params=pltpu.CompilerParams(
            dimension_semantics=("arbitrary",), vmem_limit_bytes=VMEM_LIMIT),
        name="hyena_filter",
    )(feats, *args, window)
    return filt, ss


def _spec_mul(yr, yi, kr, ki, first):
    row0 = jnp.logical_and(first, lax.broadcasted_iota(jnp.int32, yr.shape, 0) == 0)
    zr = yr * kr - jnp.where(row0, 0.0, yi * ki)
    zi = jnp.where(row0, yi * ki, yr * ki + yi * kr)
    return zr, zi


def _dft_filter_kernel(wre_ref, wim_ref, x_ref, ss_ref, kre_ref, kim_ref):
    x = x_ref[...]
    yr = jnp.dot(wre_ref[...], x, preferred_element_type=F32)
    yi = jnp.dot(wim_ref[...], x, preferred_element_type=F32)
    half = yr.shape[1] // 2
    scale = lax.rsqrt(ss_ref[...] + EPS)
    row0 = jnp.logical_and(pl.program_id(0) == 0,
                           lax.broadcasted_iota(jnp.int32, (yr.shape[0], half), 0) == 0)
    kre_ref[...] = (yr[:, :half] + yr[:, half:]) * scale
    kim_ref[...] = (yi[:, :half] + jnp.where(row0, yi[:, half:], -yi[:, half:])) * scale


def _dft_filter(f_re, f_im, filt, ss):
    length, n = filt.shape
    tile = min(DFT_TILE, length)
    wspec = pl.BlockSpec((tile, length), lambda i: (i, 0))
    return pl.pallas_call(
        _dft_filter_kernel,
        grid=(length // tile,),
        in_specs=[wspec, wspec, pl.BlockSpec((length, n), lambda i: (0, 0)),
                  pl.BlockSpec((1, n // 2), lambda i: (0, 0))],
        out_specs=[pl.BlockSpec((tile, n // 2), lambda i: (i, 0))] * 2,
        out_shape=[jax.ShapeDtypeStruct((length, n // 2), F32)] * 2,
        compiler_params=pltpu.CompilerParams(
            dimension_semantics=("parallel",), vmem_limit_bytes=VMEM_LIMIT),
        name="hyena_filter_dft",
    )(f_re, f_im, filt, ss)


def _dft_fwd_kernel(wre_ref, wim_ref, y_ref, kre_ref, kim_ref, zre_ref, zim_ref):
    y = y_ref[...].astype(BF16)
    yr = jnp.dot(wre_ref[...], y, preferred_element_type=F32)
    yi = jnp.dot(wim_ref[...], y, preferred_element_type=F32)
    zr, zi = _spec_mul(yr, yi, kre_ref[...], kim_ref[...], pl.program_id(0) == 0)
    zre_ref[...] = zr.astype(BF16)
    zim_ref[...] = zi.astype(BF16)


def _dft_fwd(f_re, f_im, y, k_re, k_im, order, row0):
    n_batch = y.shape[0]
    length = f_re.shape[0]
    tile = min(DFT_TILE, length)
    wspec = pl.BlockSpec((tile, length), lambda i, b: (i, 0))
    kspec = pl.BlockSpec((tile, W_GROUP), lambda i, b: (i, order))
    zspec = pl.BlockSpec((None, tile, W_GROUP), lambda i, b: (b, i, 0))
    return pl.pallas_call(
        _dft_fwd_kernel,
        grid=(length // tile, n_batch),
        in_specs=[wspec, wspec,
                  pl.BlockSpec((None, length, W_GROUP), lambda i, b: (b, row0 // length, 0)),
                  kspec, kspec],
        out_specs=[zspec, zspec],
        out_shape=[jax.ShapeDtypeStruct((n_batch, length, W_GROUP), BF16)] * 2,
        compiler_params=pltpu.CompilerParams(
            dimension_semantics=("parallel", "arbitrary"), vmem_limit_bytes=VMEM_LIMIT),
        name="hyena_dft_fwd",
    )(f_re, f_im, y, k_re, k_im)


def _dft_inv_kernel(wre_ref, wim_ref, zre_ref, zim_ref, y_ref, gate_ref, bias_ref, o_ref, obf_ref):
    conv = (jnp.dot(wre_ref[...], zre_ref[...], preferred_element_type=F32) +
            jnp.dot(wim_ref[...], zim_ref[...], preferred_element_type=F32))
    out = gate_ref[...] * (conv + bias_ref[...] * y_ref[...])
    o_ref[...] = out
    obf_ref[...] = out.astype(BF16)


def _dft_inv(i_re, i_im, z_re, z_im, y_src, gate_src, bias):
    n_batch, length, _ = z_re.shape
    tile = min(DFT_TILE, length)
    wspec = pl.BlockSpec((tile, length), lambda i, b: (i, 0))
    zspec = pl.BlockSpec((None, length, W_GROUP), lambda i, b: (b, 0, 0))
    ospec = pl.BlockSpec((None, tile, W_GROUP), lambda i, b: (b, i, 0))
    view = lambda src: pl.BlockSpec((None, tile, W_GROUP), lambda i, b: (b, src[1] // tile + i, src[2]))
    return pl.pallas_call(
        _dft_inv_kernel,
        grid=(length // tile, n_batch),
        in_specs=[wspec, wspec, zspec, zspec, view(y_src), view(gate_src),
                  pl.BlockSpec((1, W_GROUP), lambda i, b: (0, 0))],
        out_specs=[ospec, ospec],
        out_shape=[jax.ShapeDtypeStruct((n_batch, length, W_GROUP), F32),
                   jax.ShapeDtypeStruct((n_batch, length, W_GROUP), BF16)],
        compiler_params=pltpu.CompilerParams(
            dimension_semantics=("parallel", "arbitrary"), vmem_limit_bytes=VMEM_LIMIT),
        name="hyena_dft_inv",
    )(i_re, i_im, z_re, z_im, y_src[0], gate_src[0], bias.astype(F32).reshape(1, W_GROUP))


def _hyena_segment(u, row0, length, dft, filt_params, bias):
    f_re, f_im, i_re, i_im = dft
    filt, ss = _hyena_filters(length, *filt_params)
    k_re, k_im = _dft_filter(f_re, f_im, filt, ss)
    z_re, z_im = _dft_fwd(f_re, f_im, u, k_re, k_im, 0, row0)
    y, y_bf = _dft_inv(i_re, i_im, z_re, z_im, (u, row0, 0), (u, row0, 1), bias[0])
    z_re, z_im = _dft_fwd(f_re, f_im, y_bf, k_re, k_im, 1, 0)
    y, _ = _dft_inv(i_re, i_im, z_re, z_im, (y, 0, 0), (u, row0, 2), bias[1])
    return y


def _to_colmajor(t, rows):
    b, l, w = t.shape
    return t.reshape(b, rows, GRID_W, w).swapaxes(1, 2).reshape(b, l, w)


def _from_colmajor(t, rows):
    b, l, w = t.shape
    return t.reshape(b, GRID_W, rows, w).swapaxes(1, 2).reshape(b, l, w)


def _permute_w_in(w_in):
    o_ml = HG_COLS
    o_hy = o_ml + ML_MAIN + ML_GATES
    o_gd = o_hy + HY_COLS
    parts = [w_in[:, :HG_COLS], w_in[:, o_ml:o_ml + ML_MAIN], w_in[:, o_hy:o_gd],
             w_in[:, o_gd:o_gd + GD_MAIN], w_in[:, o_ml + ML_MAIN:o_hy],
             w_in[:, o_gd + GD_MAIN:],
             jnp.zeros((w_in.shape[0], GATE_PAD - ML_GATES - GD_GATES), w_in.dtype)]
    return jnp.concatenate(parts, 1).astype(BF16)


IN_SPLITS = (HG_COLS, ML_MAIN, HY_COLS, GD_MAIN, GATE_PAD)


def kernel(x, c, ctx, c_ctx, mod_w, mod_b, norm1_g, norm2_g, w_in, w_out, hg_lb_logits, hg_norm_g,
           ml_gate_b, ml_norm_g, hy_conv_w, hy_conv_b, hy_w1, hy_b1, hy_w2, hy_b2, hy_freq, hy_w3,
           hy_bias, gd_conv_w, gd_a_log, gd_dt_bias, gd_norm_g, router_w, router_b, exp_w1, exp_b1,
           exp_w2, exp_b2, final_g):
    n_batch, seq, d = x.shape
    ctx_len = ctx.shape[1]
    depth = mod_w.shape[0]
    rows = seq // GRID_W
    p = jax.nn.softmax(hg_lb_logits.astype(F32), axis=0)
    lower_bounds = jnp.cumsum(p, 0) - p[0]
    s_all = jnp.concatenate([jax.nn.silu(c), jax.nn.silu(c_ctx)[None]], 0)
    xs = jnp.concatenate([x, ctx], 1)
    lat = lambda t: t[:, :seq]
    cx = lambda t: t[:, seq:]
    colmajor = lambda t: jnp.concatenate([_to_colmajor(lat(t), rows), cx(t)], 1)
    rowmajor = lambda t: jnp.concatenate([_from_colmajor(lat(t), rows), cx(t)], 1)
    dft_lat, dft_ctx = _dft_matrices(seq), _dft_matrices(ctx_len)
    for l in range(depth):
        mod = (s_all @ mod_w[l] + mod_b[l]).reshape(n_batch + 1, 6, 1, d)
        sh1, sc1, g1, sh2, sc2, g2 = (mod[:, i] for i in range(6))
        z_hg, z_ml, z_hy, z_gd, z_gt = _in_proj(xs, norm1_g[l], sc1, sh1,
                                                _permute_w_in(w_in[l]), IN_SPLITS)
        hg_f, hg_b = _hgrn2_scan(z_hg, lower_bounds[l])
        ml_f, ml_b = _mlstm_scan(colmajor(z_ml), colmajor(z_gt), ml_gate_b[l], ctx_len)
        filt = (hy_w1[l], hy_b1[l], hy_w2[l], hy_b2[l], hy_freq[l], hy_w3[l])
        u_hy = _short_conv_tiles(z_hy, hy_conv_w[l], hy_conv_b[l], seq, False)
        c_lat = _hyena_segment(u_hy, 0, seq, dft_lat, filt, hy_bias[l])
        if l == depth - 1:
            c_ctx = jnp.zeros((n_batch, ctx_len, W_GROUP), F32)
        else:
            c_ctx = _hyena_segment(u_hy, seq, ctx_len, dft_ctx, filt, hy_bias[l])
        qkv = _short_conv_tiles(z_gd, gd_conv_w[l], jnp.zeros((3 * W_GROUP,), F32), seq, True)
        gd_f, gd_b = _gdn_scan(qkv, z_gt, gd_a_log[l], gd_dt_bias[l], ctx_len)
        mixers = [("silu", hg_f, hg_b, (z_hg, 4)),
                  ("sigmoid", rowmajor(ml_f), rowmajor(ml_b), (z_ml, 3)),
                  ("final", jnp.concatenate([c_lat, c_ctx], 1)),
                  ("silu", gd_f, gd_b, (z_gd, 3))]
        gains = jnp.stack([hg_norm_g[l], ml_norm_g[l], jnp.ones_like(ml_norm_g[l]), gd_norm_g[l]]).astype(F32)
        xs = _out_proj(xs, mixers, gains, g1, w_out[l].astype(BF16))
        h, logits = _moe_pre(xs, norm2_g[l], sc2, sh2, router_w[l], router_b[l])
        y = _moe(h.reshape(-1, d), logits.reshape(-1, GATE_PAD)[:, :N_EXPERTS],
                 exp_w1[l], exp_b1[l], exp_w2[l], exp_b2[l]).reshape(xs.shape)
        g2_rows = jnp.concatenate([jnp.broadcast_to(g2[:n_batch], (n_batch, seq, d)),
                                   jnp.broadcast_to(g2[n_batch:], (n_batch, ctx_len, d))], 1)
        xs = xs + g2_rows * y
    return _final_norm(xs, final_g, seq)
```

```python
import functools
import math

import jax
import jax.numpy as jnp
from jax import lax
from jax.experimental import pallas as pl
from jax.experimental.pallas import tpu as pltpu

F32 = jnp.float32
BF16 = jnp.bfloat16

D_MODEL = 1024
GRID_W = 64
N_MIXERS = 4
W_GROUP = D_MODEL // N_MIXERS
HG_HEADS = 4
HG_DK = W_GROUP // HG_HEADS
HG_CHUNK = 16
ML_HEADS = 4
ML_DH = W_GROUP // ML_HEADS
ML_CHUNK = 64
HY_ORDER = 2
HY_BANDS = 8
HY_FAST_DECAY = 0.3
HY_SLOW_DECAY = 1.5
HY_TARGET = 1e-2
GD_HEADS = 4
GD_DH = W_GROUP // GD_HEADS
GD_CHUNK = 64
N_EXPERTS = 32
TOP_K = 4
SWIGLU_LIMIT = 7.0
SWIGLU_ALPHA = 1.702
EPS = 1e-6
NEG = -1e30
HG_COLS = 5 * W_GROUP
ML_MAIN = 4 * W_GROUP
ML_GATES = 4 * ML_HEADS
HY_COLS = 3 * W_GROUP
GD_MAIN = 4 * W_GROUP
GD_GATES = 4 * GD_HEADS
GATE_PAD = 128

ROW_TILE = 256
MOE_TILE = 256
VMEM_LIMIT = 56 * 1024 * 1024


def _norm_mod(x, gain, sc, sh):
    y = x * lax.rsqrt(jnp.mean(x * x, -1, keepdims=True) + EPS)
    return y * gain * (1.0 + sc) + sh


def _in_proj_kernel(x_ref, g_ref, sc_ref, sh_ref, w_ref, *out_refs, splits):
    h = _norm_mod(x_ref[...], g_ref[...], sc_ref[...], sh_ref[...]).astype(BF16)
    off = 0
    for o_ref, n in zip(out_refs, splits):
        o_ref[...] = jnp.dot(h, w_ref[:, off:off + n], preferred_element_type=F32)
        off += n


def _mod_index_map(n_batch, n_tiles):
    return lambda b, j: (jnp.where(j == n_tiles - 1, n_batch, b), 0, 0)


def _in_proj(xs, gain, sc_tab, sh_tab, w, splits):
    n_batch, seq, d = xs.shape
    n = w.shape[1]
    grid = (n_batch, seq // ROW_TILE)
    mod_spec = pl.BlockSpec((None, 1, d), _mod_index_map(n_batch, seq // ROW_TILE))
    return pl.pallas_call(
        functools.partial(_in_proj_kernel, splits=splits),
        grid=grid,
        in_specs=[
            pl.BlockSpec((None, ROW_TILE, d), lambda b, j: (b, j, 0)),
            pl.BlockSpec((1, d), lambda b, j: (0, 0)),
            mod_spec, mod_spec,
            pl.BlockSpec((d, n), lambda b, j: (0, 0)),
        ],
        out_specs=[pl.BlockSpec((None, ROW_TILE, s), lambda b, j: (b, j, 0)) for s in splits],
        out_shape=[jax.ShapeDtypeStruct((n_batch, seq, s), F32) for s in splits],
        compiler_params=pltpu.CompilerParams(
            dimension_semantics=("parallel", "parallel"), vmem_limit_bytes=VMEM_LIMIT),
        name="in_proj",
    )(xs, gain.reshape(1, d), sc_tab, sh_tab, w)


def _moe_route_kernel(x_ref, g_ref, sc_ref, sh_ref, rw_ref, rb_ref, tri_ref, h_ref, idx_ref, gate_ref,
                      rank_ref, cnt_ref, carry):
    @pl.when(jnp.logical_and(pl.program_id(0) == 0, pl.program_id(1) == 0))
    def _():
        carry[...] = jnp.zeros_like(carry)

    h = _norm_mod(x_ref[...], g_ref[...], sc_ref[...], sh_ref[...])
    h_ref[...] = h
    work = _dot_hi(h, rw_ref[...]) + rb_ref[...]
    lane = lax.broadcasted_iota(jnp.int32, work.shape, 1)
    vals, hits = [], []
    for _ in range(TOP_K):
        m = jnp.max(work, axis=-1, keepdims=True)
        first = jnp.min(jnp.where(work == m, lane, GATE_PAD), axis=-1, keepdims=True)
        hit = lane == first
        vals.append(m)
        hits.append(hit)
        work = jnp.where(hit, -jnp.inf, work)
    exps = [jnp.exp(v - vals[0]) for v in vals]
    total = sum(exps)
    chosen = sum(hit.astype(F32) for hit in hits)
    before = jnp.dot(tri_ref[...], chosen.astype(BF16), preferred_element_type=F32) + carry[...]
    idx = jnp.zeros(work.shape, jnp.int32)
    gate = jnp.zeros(work.shape, F32)
    rank = jnp.zeros(work.shape, jnp.int32)
    for k in range(TOP_K):
        idx = jnp.where(lane == k, jnp.sum(jnp.where(hits[k], lane, 0), axis=-1, keepdims=True), idx)
        gate = jnp.where(lane == k, exps[k] / total, gate)
        r_k = jnp.sum(jnp.where(hits[k], before, 0.0), axis=-1, keepdims=True)
        rank = jnp.where(lane == k, r_k.astype(jnp.int32), rank)
    idx_ref[...] = idx
    gate_ref[...] = gate
    rank_ref[...] = rank
    carry[...] += jnp.sum(chosen, axis=0, keepdims=True)
    cnt_ref[...] = carry[...]


def _moe_route(xs, gain, sc_tab, sh_tab, router_w, router_b):
    n_batch, seq, d = xs.shape
    pad = GATE_PAD - N_EXPERTS
    rw = jnp.pad(router_w.astype(F32), ((0, 0), (0, pad)))
    rb = jnp.pad(router_b.astype(F32), (0, pad), constant_values=NEG).reshape(1, GATE_PAD)
    r = lax.broadcasted_iota(jnp.int32, (ROW_TILE, ROW_TILE), 0)
    c = lax.broadcasted_iota(jnp.int32, (ROW_TILE, ROW_TILE), 1)
    tri = (c < r).astype(BF16)
    mod_spec = pl.BlockSpec((None, 1, d), _mod_index_map(n_batch, seq // ROW_TILE))
    tok = lambda w: pl.BlockSpec((None, ROW_TILE, w), lambda b, j: (b, j, 0))
    return pl.pallas_call(
        _moe_route_kernel,
        grid=(n_batch, seq // ROW_TILE),
        in_specs=[
            tok(d),
            pl.BlockSpec((1, d), lambda b, j: (0, 0)),
            mod_spec, mod_spec,
            pl.BlockSpec((d, GATE_PAD), lambda b, j: (0, 0)),
            pl.BlockSpec((1, GATE_PAD), lambda b, j: (0, 0)),
            pl.BlockSpec((ROW_TILE, ROW_TILE), lambda b, j: (0, 0)),
        ],
        out_specs=[tok(d), tok(GATE_PAD), tok(GATE_PAD), tok(GATE_PAD),
                   pl.BlockSpec((1, GATE_PAD), lambda b, j: (0, 0))],
        out_shape=[jax.ShapeDtypeStruct((n_batch, seq, d), F32),
                   jax.ShapeDtypeStruct((n_batch, seq, GATE_PAD), jnp.int32),
                   jax.ShapeDtypeStruct((n_batch, seq, GATE_PAD), F32),
                   jax.ShapeDtypeStruct((n_batch, seq, GATE_PAD), jnp.int32),
                   jax.ShapeDtypeStruct((1, GATE_PAD), F32)],
        scratch_shapes=[pltpu.VMEM((1, GATE_PAD), F32)],
        compiler_params=pltpu.CompilerParams(
            dimension_semantics=("arbitrary", "arbitrary"), vmem_limit_bytes=VMEM_LIMIT),
        name="moe_route",
    )(xs, gain.reshape(1, d), sc_tab, sh_tab, rw, rb, tri)


def _out_proj_kernel(*refs, kinds):
    x_ref, refs = refs[0], refs[1:]
    gains_ref, g_ref, heads_ref, w_ref, o_ref = refs[-5:]
    heads = heads_ref[...] * (1.0 / HG_DK)
    acc = jnp.zeros(o_ref.shape, F32)
    pos = 0
    for i, kind in enumerate(kinds):
        if kind == "final":
            m = refs[pos][...]
            pos += 1
        else:
            o = refs[pos][...] + refs[pos + 1][...]
            gate = refs[pos + 2][...]
            pos += 3
            ms = _dot_hi(o * o, heads)
            o = o * lax.rsqrt(ms + EPS) * gains_ref[i:i + 1, :]
            m = o * (jax.nn.silu(gate) if kind == "silu" else jax.nn.sigmoid(gate))
        acc += jnp.dot(m.astype(BF16), w_ref[i * W_GROUP:(i + 1) * W_GROUP, :],
                       preferred_element_type=F32)
    o_ref[...] = x_ref[...] + g_ref[...] * acc


def _out_proj(xs, mixers, gains, gate_tab, w_out):
    n_batch, seq, d = xs.shape
    row = lambda w, cb=0: pl.BlockSpec((None, ROW_TILE, w), lambda b, j: (b, j, cb))
    args, specs, kinds = [], [], []
    for m in mixers:
        kinds.append(m[0])
        if m[0] == "final":
            args.append(m[1])
            specs.append(row(W_GROUP))
        else:
            src, cb = m[3]
            args += [m[1], m[2], src]
            specs += [row(W_GROUP), row(W_GROUP), row(W_GROUP, cb)]
    heads = _hg_consts()[3]
    return pl.pallas_call(
        functools.partial(_out_proj_kernel, kinds=tuple(kinds)),
        grid=(n_batch, seq // ROW_TILE),
        in_specs=[row(d)] + specs + [
            pl.BlockSpec((N_MIXERS, W_GROUP), lambda b, j: (0, 0)),
            pl.BlockSpec((None, 1, d), _mod_index_map(n_batch, seq // ROW_TILE)),
            pl.BlockSpec((W_GROUP, W_GROUP), lambda b, j: (0, 0)),
            pl.BlockSpec((d, d), lambda b, j: (0, 0)),
        ],
        out_specs=row(d),
        out_shape=jax.ShapeDtypeStruct((n_batch, seq, d), F32),
        compiler_params=pltpu.CompilerParams(
            dimension_semantics=("parallel", "parallel"), vmem_limit_bytes=VMEM_LIMIT),
        name="out_proj",
    )(xs, *args, gains, gate_tab, heads, w_out)


def _row_copy(src, src_row, dst, dst_row, sem):
    return pltpu.make_async_copy(src.at[pl.ds(src_row, 1), :], dst.at[pl.ds(dst_row, 1), :], sem)


def _dispatch_kernel(dest_ref, h_hbm, slots_in_hbm, slots_hbm, sem):
    del slots_in_hbm
    base = pl.program_id(0) * ROW_TILE

    def issue(r, carry):
        for k in range(TOP_K):
            _row_copy(h_hbm, base + r, slots_hbm, dest_ref[0, r * TOP_K + k], sem).start()
        return carry

    lax.fori_loop(0, ROW_TILE, issue, 0)
    def drain(r, carry):
        for k in range(TOP_K):
            _row_copy(h_hbm, base + r, slots_hbm, dest_ref[0, r * TOP_K + k], sem).wait()
        return carry

    lax.fori_loop(0, ROW_TILE, drain, 0)


def _moe_dispatch(h, dest, n_slots):
    t_count, d = h.shape
    return pl.pallas_call(
        _dispatch_kernel,
        grid=(t_count // ROW_TILE,),
        in_specs=[pl.BlockSpec((None, 1, ROW_TILE * TOP_K), lambda i: (i, 0, 0), memory_space=pltpu.SMEM),
                  pl.BlockSpec(memory_space=pl.ANY),
                  pl.BlockSpec(memory_space=pl.ANY)],
        out_specs=pl.BlockSpec(memory_space=pl.ANY),
        out_shape=jax.ShapeDtypeStruct((n_slots, d), F32),
        scratch_shapes=[pltpu.SemaphoreType.DMA(())],
        input_output_aliases={2: 0},
        compiler_params=pltpu.CompilerParams(dimension_semantics=("arbitrary",)),
        name="moe_dispatch",
    )(dest, h, jnp.zeros((n_slots, d), F32))


def _expert_kernel(bexp_ref, nused_ref, x_ref, w1_ref, b1_ref, w2_ref, b2_ref, o_ref, w1b, w2b):
    i = pl.program_id(0)
    changed = jnp.logical_or(i == 0, bexp_ref[i] != bexp_ref[jnp.maximum(i - 1, 0)])
    used = i < nused_ref[0]

    @pl.when(jnp.logical_and(changed, used))
    def _():
        w1b[...] = w1_ref[...].astype(BF16)
        w2b[...] = w2_ref[...].astype(BF16)

    @pl.when(used)
    def _():
        f = w2b.shape[0]
        u = jnp.dot(x_ref[...].astype(BF16), w1b[...], preferred_element_type=F32) + b1_ref[...]
        gate = jnp.minimum(u[:, :f], SWIGLU_LIMIT)
        lin = jnp.clip(u[:, f:], -SWIGLU_LIMIT, SWIGLU_LIMIT)
        y = (lin + 1.0) * gate * jax.nn.sigmoid(SWIGLU_ALPHA * gate)
        o_ref[...] = jnp.dot(y.astype(BF16), w2b[...], preferred_element_type=F32) + b2_ref[...]

    @pl.when(jnp.logical_not(used))
    def _():
        o_ref[...] = jnp.zeros_like(o_ref)


def _expert_ffn(block_exp, n_used, xg, w1, b1, w2, b2):
    n_slots, d = xg.shape
    n_exp, _, f2 = w1.shape
    f = f2 // 2
    n_blocks = n_slots // MOE_TILE
    return pl.pallas_call(
        _expert_kernel,
        grid_spec=pltpu.PrefetchScalarGridSpec(
            num_scalar_prefetch=2,
            grid=(n_blocks,),
            in_specs=[
                pl.BlockSpec((MOE_TILE, d), lambda i, be, nu: (i, 0)),
                pl.BlockSpec((None, d, f2), lambda i, be, nu: (be[i], 0, 0)),
                pl.BlockSpec((None, 1, f2), lambda i, be, nu: (be[i], 0, 0)),
                pl.BlockSpec((None, f, d), lambda i, be, nu: (be[i], 0, 0)),
                pl.BlockSpec((None, 1, d), lambda i, be, nu: (be[i], 0, 0)),
            ],
            out_specs=pl.BlockSpec((MOE_TILE, d), lambda i, be, nu: (i, 0)),
            scratch_shapes=[pltpu.VMEM((d, f2), BF16), pltpu.VMEM((f, d), BF16)],
        ),
        out_shape=jax.ShapeDtypeStruct((n_slots, d), F32),
        compiler_params=pltpu.CompilerParams(
            dimension_semantics=("arbitrary",), vmem_limit_bytes=VMEM_LIMIT),
        name="expert_ffn",
    )(block_exp, n_used, xg, w1, b1.reshape(n_exp, 1, f2), w2, b2.reshape(n_exp, 1, d))


def _combine_kernel(dest_ref, y_hbm, x_ref, gate_ref, g_ref, o_ref, ybuf, sem):
    def issue(r, carry):
        for k in range(TOP_K):
            pltpu.make_async_copy(y_hbm.at[pl.ds(dest_ref[0, r * TOP_K + k], 1), :],
                                  ybuf.at[k, pl.ds(r, 1), :], sem).start()
        return carry

    lax.fori_loop(0, ROW_TILE, issue, 0)

    def drain(r, carry):
        for k in range(TOP_K):
            pltpu.make_async_copy(y_hbm.at[pl.ds(dest_ref[0, r * TOP_K + k], 1), :],
                                  ybuf.at[k, pl.ds(r, 1), :], sem).wait()
        return carry

    lax.fori_loop(0, ROW_TILE, drain, 0)
    acc = jnp.zeros(o_ref.shape, F32)
    for k in range(TOP_K):
        acc += gate_ref[:, k:k + 1] * ybuf[k]
    o_ref[...] = x_ref[...] + g_ref[...] * acc


def _moe_combine(xs, y_slots, dest, gates, gate_tab):
    n_batch, seq, d = xs.shape
    n_tiles = seq // ROW_TILE
    tok = lambda w: pl.BlockSpec((None, ROW_TILE, w), lambda b, j: (b, j, 0))
    return pl.pallas_call(
        _combine_kernel,
        grid=(n_batch, n_tiles),
        in_specs=[pl.BlockSpec((None, 1, ROW_TILE * TOP_K), lambda b, j: (b * n_tiles + j, 0, 0),
                               memory_space=pltpu.SMEM),
                  pl.BlockSpec(memory_space=pl.ANY),
                  tok(d), tok(GATE_PAD),
                  pl.BlockSpec((None, 1, d), _mod_index_map(n_batch, n_tiles))],
        out_specs=tok(d),
        out_shape=jax.ShapeDtypeStruct((n_batch, seq, d), F32),
        scratch_shapes=[pltpu.VMEM((TOP_K, ROW_TILE, d), F32), pltpu.SemaphoreType.DMA(())],
        compiler_params=pltpu.CompilerParams(
            dimension_semantics=("arbitrary", "arbitrary"), vmem_limit_bytes=VMEM_LIMIT),
        name="moe_combine",
    )(dest, y_slots, xs, gates, gate_tab)


def _moe_layer(xs, gain, sc_tab, sh_tab, gate_tab, router_w, router_b, w1, b1, w2, b2):
    n_batch, seq, d = xs.shape
    t_count = n_batch * seq
    h, idx, gates, rank, counts = _moe_route(xs, gain, sc_tab, sh_tab, router_w, router_b)
    counts = counts[0, :N_EXPERTS].astype(jnp.int32)
    padded = (counts + MOE_TILE - 1) // MOE_TILE * MOE_TILE
    pend = jnp.cumsum(padded)
    pstart = pend - padded
    top_i = idx.reshape(t_count, GATE_PAD)[:, :TOP_K]
    hit = top_i[:, :, None] == jnp.arange(N_EXPERTS, dtype=jnp.int32)
    dest = rank.reshape(t_count, GATE_PAD)[:, :TOP_K] + jnp.sum(jnp.where(hit, pstart, 0), -1)
    dest = dest.astype(jnp.int32).reshape(t_count // ROW_TILE, 1, ROW_TILE * TOP_K)
    n_blocks = -(-(t_count * TOP_K + N_EXPERTS * (MOE_TILE - 1)) // MOE_TILE)
    block_exp = jnp.clip(jnp.searchsorted(pend, jnp.arange(n_blocks) * MOE_TILE, side='right'),
                         0, N_EXPERTS - 1).astype(jnp.int32)
    n_used = (pend[-1:] // MOE_TILE).astype(jnp.int32)
    slots = _moe_dispatch(h.reshape(t_count, d), dest, n_blocks * MOE_TILE)
    y_slots = _expert_ffn(block_exp, n_used, slots, w1, b1, w2, b2)
    return _moe_combine(xs, y_slots, dest, gates, gate_tab)


def _final_norm_kernel(x_ref, g_ref, o_ref):
    x = x_ref[...]
    o_ref[...] = x * lax.rsqrt(jnp.mean(x * x, -1, keepdims=True) + EPS) * g_ref[...]


def _final_norm(xs, gain, n_lat):
    n_batch, _, d = xs.shape
    return pl.pallas_call(
        _final_norm_kernel,
        grid=(n_batch, n_lat // ROW_TILE),
        in_specs=[pl.BlockSpec((None, ROW_TILE, d), lambda b, j: (b, j, 0)),
                  pl.BlockSpec((1, d), lambda b, j: (0, 0))],
        out_specs=pl.BlockSpec((None, ROW_TILE, d), lambda b, j: (b, j, 0)),
        out_shape=jax.ShapeDtypeStruct((n_batch, n_lat, d), F32),
        compiler_params=pltpu.CompilerParams(dimension_semantics=("parallel", "parallel")),
        name="final_norm",
    )(xs, gain.reshape(1, d))


HG_SUB = 16


def _hg_consts():
    r = lax.broadcasted_iota(jnp.int32, (ROW_TILE, ROW_TILE), 0)
    c = lax.broadcasted_iota(jnp.int32, (ROW_TILE, ROW_TILE), 1)
    same_chunk = (r // HG_SUB) == (c // HG_SUB)
    tri_f = jnp.where(same_chunk & (c <= r), 1.0, 0.0).astype(F32)
    tri_b = jnp.where(same_chunk & (c >= r), 1.0, 0.0).astype(F32)
    blk = jnp.where(same_chunk, 1.0, 0.0).astype(F32)
    heads = jnp.where((r // HG_DK) == (c // HG_DK), 1.0, 0.0).astype(F32)
    return tri_f, tri_b, blk, heads


def _hg_prepare(z_ref, zf_col, lb, tri, blk, qs_s, key_s, cm_s, qh_s, kh_s, dec_s):
    q = jax.nn.silu(z_ref[:, 0:W_GROUP])
    zf = z_ref[:, zf_col:zf_col + W_GROUP]
    f = lb + (1.0 - lb) * jax.nn.sigmoid(zf)
    key = (1.0 - lb) * jax.nn.sigmoid(-zf)
    logf = jnp.log(f)
    cum = jnp.dot(tri, logf, preferred_element_type=F32, precision=lax.Precision.HIGHEST)
    tot = jnp.dot(blk, logf, preferred_element_type=F32, precision=lax.Precision.HIGHEST)
    qs_s[...] = q
    key_s[...] = key
    cm_s[...] = cum
    qh_s[...] = (q * jnp.exp(cum)).astype(BF16)
    kh_s[...] = (key * jnp.exp(tot - cum)).astype(BF16)
    dec_s[...] = jnp.exp(tot)


def _hg_chunk(c, reverse, z_ref, o_ref, st_ref, heads_bf, heads_f, qs_s, key_s, cm_s, qh_s, kh_s, dec_s):
    sl = pl.ds(pl.multiple_of(c * HG_SUB, HG_SUB), HG_SUB)
    q, k, cm = qs_s[sl, :], key_s[sl, :], cm_s[sl, :]
    v = z_ref[sl, W_GROUP:2 * W_GROUP]
    st = st_ref[...]
    o = lax.dot_general(qh_s[sl, :], st.astype(BF16), (((1,), (1,)), ((), ())),
                        preferred_element_type=F32)
    t_idx = lax.broadcasted_iota(jnp.int32, (HG_SUB, W_GROUP), 0)
    parts = []
    for s in range(HG_SUB):
        live = (t_idx <= s) if reverse else (t_idx >= s)
        e = jnp.exp(jnp.where(live, cm - cm[s:s + 1, :], NEG))
        parts.append((q * k[s:s + 1, :] * e).astype(BF16))
    r = jnp.dot(jnp.concatenate(parts, 0), heads_bf, preferred_element_type=F32)
    for s in range(HG_SUB):
        o += r[s * HG_SUB:(s + 1) * HG_SUB, :] * v[s:s + 1, :]
    o_ref[sl, :] = o
    ut = lax.dot_general(v.astype(BF16), kh_s[sl, :], (((0,), (0,)), ((), ())),
                         preferred_element_type=F32)
    st_ref[...] = st * dec_s[sl, :][0:1, :] + ut * heads_f


def _hgrn2_kernel(zf_ref, zb_ref, lb_ref, trif_ref, trib_ref, blk_ref, heads_ref, of_ref, ob_ref,
                  st_s, qs_s, key_s, cm_s, qh_s, kh_s, dec_s):
    @pl.when(pl.program_id(1) == 0)
    def _():
        st_s[...] = jnp.zeros_like(st_s)

    lb = lb_ref[...]
    blk = blk_ref[...]
    heads_f = heads_ref[...]
    heads_bf = heads_f.astype(BF16)
    _hg_prepare(zf_ref, 2 * W_GROUP, lb, trif_ref[...], blk, qs_s.at[0], key_s.at[0], cm_s.at[0],
                qh_s.at[0], kh_s.at[0], dec_s.at[0])
    _hg_prepare(zb_ref, 3 * W_GROUP, lb, trib_ref[...], blk, qs_s.at[1], key_s.at[1], cm_s.at[1],
                qh_s.at[1], kh_s.at[1], dec_s.at[1])
    n_chunks = ROW_TILE // HG_SUB

    def body(c, carry):
        _hg_chunk(c, False, zf_ref, of_ref, st_s.at[0], heads_bf, heads_f, qs_s.at[0], key_s.at[0],
                  cm_s.at[0], qh_s.at[0], kh_s.at[0], dec_s.at[0])
        _hg_chunk(n_chunks - 1 - c, True, zb_ref, ob_ref, st_s.at[1], heads_bf, heads_f, qs_s.at[1],
                  key_s.at[1], cm_s.at[1], qh_s.at[1], kh_s.at[1], dec_s.at[1])
        return carry

    lax.fori_loop(0, n_chunks, body, 0)


def _scan_tiles(n_lat_tiles):
    fwd = lambda j: jnp.where(j == 0, n_lat_tiles, j - 1)
    bwd = lambda j: jnp.where(j == 0, n_lat_tiles, n_lat_tiles - j)
    return fwd, bwd


def _hgrn2_scan(z_hg, lb):
    n_batch, seq, _ = z_hg.shape
    n_tiles = seq // ROW_TILE
    fwd, bwd = _scan_tiles(n_tiles - 1)
    const = pl.BlockSpec((ROW_TILE, ROW_TILE), lambda b, j: (0, 0))
    scr = lambda dt: pltpu.VMEM((2, ROW_TILE, W_GROUP), dt)
    return pl.pallas_call(
        _hgrn2_kernel,
        grid=(n_batch, n_tiles),
        in_specs=[pl.BlockSpec((None, ROW_TILE, HG_COLS), lambda b, j: (b, fwd(j), 0)),
                  pl.BlockSpec((None, ROW_TILE, HG_COLS), lambda b, j: (b, bwd(j), 0)),
                  pl.BlockSpec((1, W_GROUP), lambda b, j: (0, 0)),
                  const, const, const, const],
        out_specs=[pl.BlockSpec((None, ROW_TILE, W_GROUP), lambda b, j: (b, fwd(j), 0)),
                   pl.BlockSpec((None, ROW_TILE, W_GROUP), lambda b, j: (b, bwd(j), 0))],
        out_shape=[jax.ShapeDtypeStruct((n_batch, seq, W_GROUP), F32)] * 2,
        scratch_shapes=[pltpu.VMEM((2, W_GROUP, W_GROUP), F32), scr(F32), scr(F32), scr(F32),
                        scr(BF16), scr(BF16), scr(F32)],
        compiler_params=pltpu.CompilerParams(
            dimension_semantics=("parallel", "arbitrary"), vmem_limit_bytes=VMEM_LIMIT),
        name="hgrn2_scan",
    )(z_hg, z_hg, lb.reshape(1, W_GROUP), *_hg_consts())


def _dot_hi(a, b):
    return jnp.dot(a, b, preferred_element_type=F32, precision=lax.Precision.HIGHEST)


def _dot_nt(a, b, precision=None):
    return lax.dot_general(a, b, (((1,), (1,)), ((), ())), preferred_element_type=F32,
                           precision=precision)


def _dot_tn(a, b):
    return lax.dot_general(a, b, (((0,), (0,)), ((), ())), preferred_element_type=F32)


def _chunk_consts(chunk, n_heads, gate_cols):
    r = lax.broadcasted_iota(jnp.int32, (chunk, chunk), 0)
    c = lax.broadcasted_iota(jnp.int32, (chunk, chunk), 1)
    low = (c <= r).astype(F32)
    upp = (c >= r).astype(F32)
    dh = W_GROUP // n_heads
    hr = lax.broadcasted_iota(jnp.int32, (W_GROUP, W_GROUP), 0) // dh
    hc = lax.broadcasted_iota(jnp.int32, (W_GROUP, W_GROUP), 1) // dh
    heads = (hr == hc).astype(F32)
    gc = lax.broadcasted_iota(jnp.int32, (GATE_PAD, W_GROUP), 0)
    gh = lax.broadcasted_iota(jnp.int32, (GATE_PAD, W_GROUP), 1) // dh
    expand = jnp.stack([(gc == g0 + gh).astype(F32) for g0 in gate_cols])
    sr = lax.broadcasted_iota(jnp.int32, (8, GATE_PAD), 0)
    sc = lax.broadcasted_iota(jnp.int32, (8, GATE_PAD), 1)
    select = jnp.stack([((sc == g0 + sr) & (sr < n_heads)).astype(F32) for g0 in gate_cols])
    return low, upp, heads, expand, select


def _ml_direction(reverse, z_ref, g_ref, gb, o_ref, c_ref, n_ref, m_ref, low, upp, heads,
                  ex_i, ex_f, sel_i, sel_f):
    q = z_ref[:, 0:W_GROUP]
    k = z_ref[:, W_GROUP:2 * W_GROUP] * (ML_DH ** -0.5)
    v = z_ref[:, 2 * W_GROUP:3 * W_GROUP]
    gates = g_ref[...] + gb
    col_sum, row_sum = (upp, low) if reverse else (low, upp)
    li = _dot_hi(gates, ex_i)
    lf = jax.nn.log_sigmoid(_dot_hi(gates, ex_f))
    li_rows = _dot_nt(sel_i, gates, lax.Precision.HIGHEST)
    lf_rows = jax.nn.log_sigmoid(_dot_nt(sel_f, gates, lax.Precision.HIGHEST))
    fcum = _dot_hi(col_sum, lf)
    fcum_rows = _dot_hi(lf_rows, row_sum)
    last = 0 if reverse else ML_CHUNK - 1
    ftot = fcum[last:last + 1, :]
    g = ftot - fcum + li
    g_max = jnp.max(g, axis=0, keepdims=True)
    kw = k * jnp.exp(g - g_max)
    c0, n0, m0 = c_ref[...], n_ref[...], m_ref[...]
    q_bf = q.astype(BF16)
    k_bf = k.astype(BF16)
    qc = jnp.dot(q_bf, c0.astype(BF16), preferred_element_type=F32)
    qn = jnp.dot((q * n0).astype(BF16), heads.astype(BF16), preferred_element_type=F32)
    a_init = fcum + m0
    rr = lax.broadcasted_iota(jnp.int32, (ML_CHUNK, ML_CHUNK), 0)
    cc = lax.broadcasted_iota(jnp.int32, (ML_CHUNK, ML_CHUNK), 1)
    live = (cc >= rr) if reverse else (cc <= rr)
    lane_head = lax.broadcasted_iota(jnp.int32, (ML_CHUNK, W_GROUP), 1) // ML_DH
    outs = []
    for h in range(ML_HEADS):
        hs = slice(h * ML_DH, (h + 1) * ML_DH)
        a = jnp.where(live, fcum[:, hs] - fcum_rows[h:h + 1, :] + li_rows[h:h + 1, :], NEG)
        a_init_h = a_init[:, hs]
        m_t = jnp.maximum(jnp.max(a, axis=1, keepdims=True), a_init_h)
        qk = _dot_nt(jnp.where(lane_head == h, q, 0.0).astype(BF16), k_bf)
        p = jnp.exp(a - m_t) * qk
        e0 = jnp.exp(a_init_h - m_t)
        num = jnp.dot(p.astype(BF16), v[:, hs].astype(BF16), preferred_element_type=F32) + e0 * qc[:, hs]
        den = jnp.sum(p, axis=1, keepdims=True) + e0 * qn[:, hs]
        outs.append(num / jnp.maximum(jnp.abs(den), jnp.exp(-m_t)))
    o_ref[...] = jnp.concatenate(outs, axis=1)
    m_new = jnp.maximum(ftot + m0, g_max)
    a_old = jnp.exp(ftot + m0 - m_new)
    a_new = jnp.exp(g_max - m_new)
    u_c = _dot_tn(kw.astype(BF16), v.astype(BF16)) * heads
    c_ref[...] = a_old * c0 + a_new * u_c
    n_ref[...] = a_old * n0 + a_new * jnp.sum(kw, axis=0, keepdims=True)
    m_ref[...] = m_new


def _mlstm_kernel(zf_ref, zb_ref, gf_ref, gb_ref, bias_ref, low_ref, upp_ref, heads_ref, ex_ref, sel_ref,
                  of_ref, ob_ref, c_s, n_s, m_s):
    @pl.when(pl.program_id(1) == 0)
    def _():
        c_s[...] = jnp.zeros_like(c_s)
        n_s[...] = jnp.zeros_like(n_s)
        m_s[...] = jnp.zeros_like(m_s)

    low, upp, heads, bias = low_ref[...], upp_ref[...], heads_ref[...], bias_ref[...]
    _ml_direction(False, zf_ref, gf_ref, bias, of_ref, c_s.at[0], n_s.at[0], m_s.at[0], low, upp, heads,
                  ex_ref[0], ex_ref[1], sel_ref[0], sel_ref[1])
    _ml_direction(True, zb_ref, gb_ref, bias, ob_ref, c_s.at[1], n_s.at[1], m_s.at[1], low, upp, heads,
                  ex_ref[2], ex_ref[3], sel_ref[2], sel_ref[3])


def _scan_chunks(n_lat, n_ctx):
    fwd = lambda j: jnp.where(j < n_ctx, n_lat + j, j - n_ctx)
    bwd = lambda j: jnp.where(j < n_ctx, n_lat + n_ctx - 1 - j, n_lat + n_ctx - 1 - j)
    return fwd, bwd


def _mlstm_scan(z_ml, z_gt, gate_b, ctx_len):
    n_batch, seq, _ = z_ml.shape
    n_chunks = seq // ML_CHUNK
    n_ctx = ctx_len // ML_CHUNK
    fwd, bwd = _scan_chunks(n_chunks - n_ctx, n_ctx)
    low, upp, heads, expand, select = _chunk_consts(ML_CHUNK, ML_HEADS, (0, 4, 8, 12))
    bias = jnp.pad(gate_b.astype(F32).reshape(1, ML_GATES), ((0, 0), (0, GATE_PAD - ML_GATES)))
    full = lambda a: pl.BlockSpec(a.shape, lambda b, j: (0,) * a.ndim)
    return pl.pallas_call(
        _mlstm_kernel,
        grid=(n_batch, n_chunks),
        in_specs=[pl.BlockSpec((None, ML_CHUNK, ML_MAIN), lambda b, j: (b, fwd(j), 0)),
                  pl.BlockSpec((None, ML_CHUNK, ML_MAIN), lambda b, j: (b, bwd(j), 0)),
                  pl.BlockSpec((None, ML_CHUNK, GATE_PAD), lambda b, j: (b, fwd(j), 0)),
                  pl.BlockSpec((None, ML_CHUNK, GATE_PAD), lambda b, j: (b, bwd(j), 0)),
                  full(bias), full(low), full(upp), full(heads), full(expand), full(select)],
        out_specs=[pl.BlockSpec((None, ML_CHUNK, W_GROUP), lambda b, j: (b, fwd(j), 0)),
                   pl.BlockSpec((None, ML_CHUNK, W_GROUP), lambda b, j: (b, bwd(j), 0))],
        out_shape=[jax.ShapeDtypeStruct((n_batch, seq, W_GROUP), F32)] * 2,
        scratch_shapes=[pltpu.VMEM((2, W_GROUP, W_GROUP), F32), pltpu.VMEM((2, 1, W_GROUP), F32),
                        pltpu.VMEM((2, 1, W_GROUP), F32)],
        compiler_params=pltpu.CompilerParams(
            dimension_semantics=("parallel", "arbitrary"), vmem_limit_bytes=VMEM_LIMIT),
        name="mlstm_scan",
    )(z_ml, z_ml, z_gt, z_gt, bias, low, upp, heads, expand, select)


HALO = 8


def _short_conv_kernel(x_ref, prev_ref, next_ref, w_ref, b_ref, heads_ref, o_ref, *, n_lat_tiles, gdn):
    j = pl.program_id(1)
    n_tiles = pl.num_programs(1)
    x = x_ref[...]
    has_prev = jnp.logical_and(j != 0, j != n_lat_tiles)
    has_next = jnp.logical_and(j != n_lat_tiles - 1, j != n_tiles - 1)
    prev_row = jnp.where(has_prev, prev_ref[HALO - 1:HALO, :], 0.0)
    next_row = jnp.where(has_next, next_ref[0:1, :], 0.0)
    row = lax.broadcasted_iota(jnp.int32, x.shape, 0)
    x_prev = jnp.where(row == 0, prev_row, pltpu.roll(x, 1, 0))
    x_next = jnp.where(row == ROW_TILE - 1, next_row, pltpu.roll(x, ROW_TILE - 1, 0))
    y = x_prev * w_ref[0:1, :] + x * w_ref[1:2, :] + x_next * w_ref[2:3, :] + b_ref[...]
    if not gdn:
        o_ref[...] = y
        return
    y = jax.nn.silu(y)
    heads = heads_ref[...]
    for i in range(2):
        t = y[:, i * W_GROUP:(i + 1) * W_GROUP]
        o_ref[:, i * W_GROUP:(i + 1) * W_GROUP] = t * lax.rsqrt(_dot_hi(t * t, heads) + EPS)
    o_ref[:, 2 * W_GROUP:] = y[:, 2 * W_GROUP:]


def _head_mask():
    r = lax.broadcasted_iota(jnp.int32, (W_GROUP, W_GROUP), 0) // HG_DK
    c = lax.broadcasted_iota(jnp.int32, (W_GROUP, W_GROUP), 1) // HG_DK
    return (r == c).astype(F32)


def _short_conv_tiles(z, conv_w, conv_b, n_lat, gdn):
    n_batch, seq, _ = z.shape
    c = conv_w.shape[1]
    n_tiles = seq // ROW_TILE
    per = ROW_TILE // HALO
    n_halo = seq // HALO
    return pl.pallas_call(
        functools.partial(_short_conv_kernel, n_lat_tiles=n_lat // ROW_TILE, gdn=gdn),
        grid=(n_batch, n_tiles),
        in_specs=[pl.BlockSpec((None, ROW_TILE, c), lambda b, j: (b, j, 0)),
                  pl.BlockSpec((None, HALO, c), lambda b, j: (b, jnp.maximum(j * per - 1, 0), 0)),
                  pl.BlockSpec((None, HALO, c), lambda b, j: (b, jnp.minimum((j + 1) * per, n_halo - 1), 0)),
                  pl.BlockSpec((3, c), lambda b, j: (0, 0)),
                  pl.BlockSpec((1, c), lambda b, j: (0, 0)),
                  pl.BlockSpec((W_GROUP, W_GROUP), lambda b, j: (0, 0))],
        out_specs=pl.BlockSpec((None, ROW_TILE, c), lambda b, j: (b, j, 0)),
        out_shape=jax.ShapeDtypeStruct((n_batch, seq, c), F32),
        compiler_params=pltpu.CompilerParams(
            dimension_semantics=("parallel", "parallel"), vmem_limit_bytes=VMEM_LIMIT),
        name="gdn_conv" if gdn else "hyena_conv",
    )(z, z, z, conv_w.astype(F32), conv_b.astype(F32).reshape(1, c), _head_mask())


def _gd_consts():
    r = lax.broadcasted_iota(jnp.int32, (GD_CHUNK, GD_CHUNK), 0)
    c = lax.broadcasted_iota(jnp.int32, (GD_CHUNK, GD_CHUNK), 1)
    same = lambda n: (r // n) == (c // n)
    eye = (r == c).astype(F32)
    masks = [same(8).astype(F32)] + [(same(2 * b) & ~same(b)).astype(F32) for b in (8, 16, 32)]
    return eye, jnp.stack(masks)


def _unit_tri_inverses(ns, eye, masks):
    mm = lambda a, b: jnp.dot(a.astype(BF16), b.astype(BF16), preferred_element_type=F32)
    n8 = [n * masks[0] for n in ns]
    n2 = [mm(a, a) for a in n8]
    n4 = [mm(a, a) for a in n2]
    ts = [mm(eye - a, eye + b) for a, b in zip(n8, n2)]
    ts = [mm(t, eye + b) for t, b in zip(ts, n4)]
    for i in range(1, 4):
        lts = [mm(n * masks[i], t) for n, t in zip(ns, ts)]
        ts = [t - mm(t, lt) for t, lt in zip(ts, lts)]
    return ts


class _GdDir:
    def __init__(self, reverse, x_ref, g_ref, s_ref, prm_lane, prm_row, low, upp, ex_a, ex_b, sel_a):
        self.q = x_ref[:, 0:W_GROUP] * (GD_DH ** -0.5)
        self.k = x_ref[:, W_GROUP:2 * W_GROUP]
        self.v = x_ref[:, 2 * W_GROUP:3 * W_GROUP]
        gates = g_ref[...]
        col_sum, row_sum = (upp, low) if reverse else (low, upp)
        log_a = prm_lane[0:1, :] * jax.nn.softplus(_dot_hi(gates, ex_a) + prm_lane[1:2, :])
        self.beta = jax.nn.sigmoid(_dot_hi(gates, ex_b))
        log_a_rows = prm_row[0] * jax.nn.softplus(_dot_nt(sel_a, gates, lax.Precision.HIGHEST) + prm_row[1])
        self.g = _dot_hi(col_sum, log_a)
        self.g_rows = _dot_hi(log_a_rows, row_sum)
        last = 0 if reverse else GD_CHUNK - 1
        self.g_last = self.g[last:last + 1, :]
        self.eg = jnp.exp(self.g)
        self.k_bf = self.k.astype(BF16)
        self.kb = self.k * self.beta
        self.s0 = s_ref[...]
        rr = lax.broadcasted_iota(jnp.int32, (GD_CHUNK, GD_CHUNK), 0)
        cc = lax.broadcasted_iota(jnp.int32, (GD_CHUNK, GD_CHUNK), 1)
        self.live = (cc >= rr) if reverse else (cc <= rr)
        self.strict = (cc > rr) if reverse else (cc < rr)


def _gdn_kernel(xf_ref, xb_ref, gf_ref, gb_ref, prm_lane_ref, prm_row_ref, low_ref, upp_ref, heads_ref,
                ex_ref, sel_ref, eye_ref, masks_ref, of_ref, ob_ref, s_s):
    @pl.when(pl.program_id(1) == 0)
    def _():
        s_s[...] = jnp.zeros_like(s_s)

    low, upp, heads, eye, masks = low_ref[...], upp_ref[...], heads_ref[...], eye_ref[...], masks_ref[...]
    dirs = [_GdDir(False, xf_ref, gf_ref, s_s.at[0], prm_lane_ref[0], prm_row_ref[0], low, upp,
                   ex_ref[0], ex_ref[1], sel_ref[0]),
            _GdDir(True, xb_ref, gb_ref, s_s.at[1], prm_lane_ref[1], prm_row_ref[1], low, upp,
                   ex_ref[2], ex_ref[3], sel_ref[2])]
    lane_head = lax.broadcasted_iota(jnp.int32, (GD_CHUNK, W_GROUP), 1) // GD_DH
    probs = [(d, h) for d in dirs for h in range(GD_HEADS)]
    only = lambda h, t: jnp.where(lane_head == h, t, 0.0).astype(BF16)
    hs = lambda h: slice(h * GD_DH, (h + 1) * GD_DH)
    decay = [jnp.exp(jnp.where(d.live, d.g[:, hs(h)] - d.g_rows[h:h + 1, :], NEG)) for d, h in probs]
    kks = [_dot_nt(only(h, d.kb), d.k_bf) for d, h in probs]
    ns = [jnp.where(d.strict, kk * dc, 0.0) for (d, h), kk, dc in zip(probs, kks, decay)]
    t_inv = [t.astype(BF16) for t in _unit_tri_inverses(ns, eye, masks)]
    us = [jnp.dot(t, only(h, d.v * d.beta), preferred_element_type=F32) for (d, h), t in zip(probs, t_inv)]
    ws = [jnp.dot(t, only(h, d.kb * d.eg), preferred_element_type=F32) for (d, h), t in zip(probs, t_inv)]
    qks = [(_dot_nt(only(h, d.q), d.k_bf) * dc).astype(BF16) for (d, h), dc in zip(probs, decay)]
    v_new, o = [], []
    for i, d in enumerate(dirs):
        sl = slice(i * GD_HEADS, (i + 1) * GD_HEADS)
        s_bf = d.s0.astype(BF16)
        v_new.append(sum(us[sl]) - jnp.dot(sum(ws[sl]).astype(BF16), s_bf, preferred_element_type=F32))
        o.append(jnp.dot((d.q * d.eg).astype(BF16), s_bf, preferred_element_type=F32))
    intra = [jnp.dot(qk, only(h, v_new[i // GD_HEADS]), preferred_element_type=F32)
             for i, ((d, h), qk) in enumerate(zip(probs, qks))]
    for i, (d, o_ref) in enumerate(zip(dirs, (of_ref, ob_ref))):
        o_ref[...] = o[i] + sum(intra[i * GD_HEADS:(i + 1) * GD_HEADS])
        k_dec = d.k * jnp.exp(d.g_last - d.g)
        s_s[i] = jnp.exp(d.g_last) * d.s0 + _dot_tn(k_dec.astype(BF16), v_new[i].astype(BF16)) * heads


def _gdn_scan(qkv, z_gt, a_log, dt_bias, ctx_len):
    n_batch, seq, _ = qkv.shape
    n_chunks = seq // GD_CHUNK
    n_ctx = ctx_len // GD_CHUNK
    fwd, bwd = _scan_chunks(n_chunks - n_ctx, n_ctx)
    g0 = ML_GATES
    low, upp, heads, expand, select = _chunk_consts(GD_CHUNK, GD_HEADS, (g0, g0 + 4, g0 + 8, g0 + 12))
    eye, masks = _gd_consts()
    prm = jnp.stack([-jnp.exp(a_log.astype(F32)), dt_bias.astype(F32)], 1)
    prm_lane = jnp.repeat(prm, GD_DH, axis=2)
    prm_row = jnp.broadcast_to(jnp.pad(prm, ((0, 0), (0, 0), (0, 8 - GD_HEADS)))[..., None],
                               (2, 2, 8, GD_CHUNK))
    full = lambda a: pl.BlockSpec(a.shape, lambda b, j: (0,) * a.ndim)
    return pl.pallas_call(
        _gdn_kernel,
        grid=(n_batch, n_chunks),
        in_specs=[pl.BlockSpec((None, GD_CHUNK, 3 * W_GROUP), lambda b, j: (b, fwd(j), 0)),
                  pl.BlockSpec((None, GD_CHUNK, 3 * W_GROUP), lambda b, j: (b, bwd(j), 0)),
                  pl.BlockSpec((None, GD_CHUNK, GATE_PAD), lambda b, j: (b, fwd(j), 0)),
                  pl.BlockSpec((None, GD_CHUNK, GATE_PAD), lambda b, j: (b, bwd(j), 0)),
                  full(prm_lane), full(prm_row), full(low), full(upp), full(heads), full(expand),
                  full(select), full(eye), full(masks)],
        out_specs=[pl.BlockSpec((None, GD_CHUNK, W_GROUP), lambda b, j: (b, fwd(j), 0)),
                   pl.BlockSpec((None, GD_CHUNK, W_GROUP), lambda b, j: (b, bwd(j), 0))],
        out_shape=[jax.ShapeDtypeStruct((n_batch, seq, W_GROUP), F32)] * 2,
        scratch_shapes=[pltpu.VMEM((2, W_GROUP, W_GROUP), F32)],
        compiler_params=pltpu.CompilerParams(
            dimension_semantics=("parallel", "arbitrary"), vmem_limit_bytes=VMEM_LIMIT),
        name="gdn_scan",
    )(qkv, qkv, z_gt, z_gt, prm_lane, prm_row, low, upp, heads, expand, select, eye, masks)


HY_FILT_TILE = 512
DFT_TILE = 512


def _dft_matrices(length):
    n = 2 * length
    k = lax.broadcasted_iota(jnp.int32, (length, length), 0)
    t = lax.broadcasted_iota(jnp.int32, (length, length), 1)
    ang = ((k * t) % n).astype(F32) * (2.0 * math.pi / n)
    f_re = jnp.cos(ang)
    f_im = jnp.where(k == 0, jnp.where(t % 2 == 0, 1.0, -1.0), -jnp.sin(ang))
    c = jnp.where(k == 0, 1.0 / n, 2.0 / n)
    return (f_re.astype(BF16), f_im.astype(BF16), (c * f_re).T.astype(BF16), (c * f_im).T.astype(BF16))


def _hy_filter_kernel(feat_ref, w1_ref, b1_ref, w2_ref, b2_ref, fr_ref, w3_ref, win_ref, filt_ref, ss_ref):
    i = pl.program_id(0)
    hdn = jnp.sin(fr_ref[0:1, :] * (_dot_hi(feat_ref[...], w1_ref[...]) + b1_ref[...]))
    hdn = jnp.sin(fr_ref[1:2, :] * (_dot_hi(hdn, w2_ref[...]) + b2_ref[...]))
    filt = _dot_hi(hdn, w3_ref[...]) * win_ref[...]
    half = filt.shape[1] // 2
    row = lax.broadcasted_iota(jnp.int32, filt.shape, 0) + i * filt.shape[0]
    col = lax.broadcasted_iota(jnp.int32, filt.shape, 1)
    filt = jnp.where(jnp.logical_and(row == 0, col >= half), 0.0, filt)
    filt_ref[...] = filt.astype(BF16)
    sq = jnp.sum(filt * filt, axis=0, keepdims=True)

    @pl.when(i == 0)
    def _():
        ss_ref[...] = jnp.zeros_like(ss_ref)

    ss_ref[...] += sq[:, :half] + sq[:, half:]


def _hyena_filters(length, w1, b1, w2, b2, freq, w3):
    tile = min(HY_FILT_TILE, length)
    t = jnp.linspace(0.0, 1.0, length, dtype=F32)[:, None]
    pos = jnp.arange(length, dtype=F32)[:, None]
    band = jnp.linspace(1e-4, HY_BANDS - 1, HY_BANDS, dtype=F32)[None, :]
    ang = 2.0 * math.pi * pos * band / length
    feats = jnp.concatenate([t, jnp.cos(ang), -jnp.sin(ang)], -1)
    deltas = jnp.abs(jnp.linspace(math.log(HY_TARGET) / HY_SLOW_DECAY,
                                  math.log(HY_TARGET) / HY_FAST_DECAY, W_GROUP, dtype=F32))
    window = jnp.tile(jnp.exp(-t * deltas), (1, 2 * HY_ORDER))
    n_out = w3.shape[1]
    emb, ffn = w1.shape
    full = lambda a: pl.BlockSpec(a.shape, lambda i: (0,) * a.ndim)
    args = [w1.astype(F32), b1.astype(F32).reshape(1, ffn), w2.astype(F32), b2.astype(F32).reshape(1, ffn),
            freq.astype(F32), w3.astype(F32)]
    filt, ss = pl.pallas_call(
        _hy_filter_kernel,
        grid=(length // tile,),
        in_specs=[pl.BlockSpec((tile, emb), lambda i: (i, 0))] + [full(a) for a in args] +
                 [pl.BlockSpec((tile, n_out), lambda i: (i, 0))],
        out_specs=[pl.BlockSpec((tile, n_out), lambda i: (i, 0)),
                   pl.BlockSpec((1, n_out // 2), lambda i: (0, 0))],
        out_shape=[jax.ShapeDtypeStruct((length, n_out), BF16),
                   jax.ShapeDtypeStruct((1, n_out // 2), F32)],
        compiler_params=pltpu.CompilerParams(
            dimension_semantics=("arbitrary",), vmem_limit_bytes=VMEM_LIMIT),
        name="hyena_filter",
    )(feats, *args, window)
    return filt, ss


def _spec_mul(yr, yi, kr, ki, first):
    row0 = jnp.logical_and(first, lax.broadcasted_iota(jnp.int32, yr.shape, 0) == 0)
    zr = yr * kr - jnp.where(row0, 0.0, yi * ki)
    zi = jnp.where(row0, yi * ki, yr * ki + yi * kr)
    return zr, zi


def _dft_filter_kernel(wre_ref, wim_ref, x_ref, ss_ref, kre_ref, kim_ref):
    x = x_ref[...]
    yr = jnp.dot(wre_ref[...], x, preferred_element_type=F32)
    yi = jnp.dot(wim_ref[...], x, preferred_element_type=F32)
    half = yr.shape[1] // 2
    scale = lax.rsqrt(ss_ref[...] + EPS)
    row0 = jnp.logical_and(pl.program_id(0) == 0,
                           lax.broadcasted_iota(jnp.int32, (yr.shape[0], half), 0) == 0)
    kre_ref[...] = (yr[:, :half] + yr[:, half:]) * scale
    kim_ref[...] = (yi[:, :half] + jnp.where(row0, yi[:, half:], -yi[:, half:])) * scale


def _dft_filter(f_re, f_im, filt, ss):
    length, n = filt.shape
    tile = min(DFT_TILE, length)
    wspec = pl.BlockSpec((tile, length), lambda i: (i, 0))
    return pl.pallas_call(
        _dft_filter_kernel,
        grid=(length // tile,),
        in_specs=[wspec, wspec, pl.BlockSpec((length, n), lambda i: (0, 0)),
                  pl.BlockSpec((1, n // 2), lambda i: (0, 0))],
        out_specs=[pl.BlockSpec((tile, n // 2), lambda i: (i, 0))] * 2,
        out_shape=[jax.ShapeDtypeStruct((length, n // 2), F32)] * 2,
        compiler_params=pltpu.CompilerParams(
            dimension_semantics=("parallel",), vmem_limit_bytes=VMEM_LIMIT),
        name="hyena_filter_dft",
    )(f_re, f_im, filt, ss)


def _dft_fwd_kernel(wre_ref, wim_ref, y_ref, kre_ref, kim_ref, zre_ref, zim_ref):
    y = y_ref[...].astype(BF16)
    yr = jnp.dot(wre_ref[...], y, preferred_element_type=F32)
    yi = jnp.dot(wim_ref[...], y, preferred_element_type=F32)
    zr, zi = _spec_mul(yr, yi, kre_ref[...], kim_ref[...], pl.program_id(0) == 0)
    zre_ref[...] = zr.astype(BF16)
    zim_ref[...] = zi.astype(BF16)


def _dft_fwd(f_re, f_im, y, k_re, k_im, order, row0):
    n_batch = y.shape[0]
    length = f_re.shape[0]
    tile = min(DFT_TILE, length)
    wspec = pl.BlockSpec((tile, length), lambda i, b: (i, 0))
    kspec = pl.BlockSpec((tile, W_GROUP), lambda i, b: (i, order))
    zspec = pl.BlockSpec((None, tile, W_GROUP), lambda i, b: (b, i, 0))
    return pl.pallas_call(
        _dft_fwd_kernel,
        grid=(length // tile, n_batch),
        in_specs=[wspec, wspec,
                  pl.BlockSpec((None, length, W_GROUP), lambda i, b: (b, row0 // length, 0)),
                  kspec, kspec],
        out_specs=[zspec, zspec],
        out_shape=[jax.ShapeDtypeStruct((n_batch, length, W_GROUP), BF16)] * 2,
        compiler_params=pltpu.CompilerParams(
            dimension_semantics=("parallel", "arbitrary"), vmem_limit_bytes=VMEM_LIMIT),
        name="hyena_dft_fwd",
    )(f_re, f_im, y, k_re, k_im)


def _dft_inv_kernel(wre_ref, wim_ref, zre_ref, zim_ref, y_ref, gate_ref, bias_ref, o_ref, obf_ref):
    conv = (jnp.dot(wre_ref[...], zre_ref[...], preferred_element_type=F32) +
            jnp.dot(wim_ref[...], zim_ref[...], preferred_element_type=F32))
    out = gate_ref[...] * (conv + bias_ref[...] * y_ref[...])
    o_ref[...] = out
    obf_ref[...] = out.astype(BF16)


def _dft_inv(i_re, i_im, z_re, z_im, y_src, gate_src, bias):
    n_batch, length, _ = z_re.shape
    tile = min(DFT_TILE, length)
    wspec = pl.BlockSpec((tile, length), lambda i, b: (i, 0))
    zspec = pl.BlockSpec((None, length, W_GROUP), lambda i, b: (b, 0, 0))
    ospec = pl.BlockSpec((None, tile, W_GROUP), lambda i, b: (b, i, 0))
    view = lambda src: pl.BlockSpec((None, tile, W_GROUP), lambda i, b: (b, src[1] // tile + i, src[2]))
    return pl.pallas_call(
        _dft_inv_kernel,
        grid=(length // tile, n_batch),
        in_specs=[wspec, wspec, zspec, zspec, view(y_src), view(gate_src),
                  pl.BlockSpec((1, W_GROUP), lambda i, b: (0, 0))],
        out_specs=[ospec, ospec],
        out_shape=[jax.ShapeDtypeStruct((n_batch, length, W_GROUP), F32),
                   jax.ShapeDtypeStruct((n_batch, length, W_GROUP), BF16)],
        compiler_params=pltpu.CompilerParams(
            dimension_semantics=("parallel", "arbitrary"), vmem_limit_bytes=VMEM_LIMIT),
        name="hyena_dft_inv",
    )(i_re, i_im, z_re, z_im, y_src[0], gate_src[0], bias.astype(F32).reshape(1, W_GROUP))


def _hyena_segment(u, row0, length, dft, filt_params, bias):
    f_re, f_im, i_re, i_im = dft
    filt, ss = _hyena_filters(length, *filt_params)
    k_re, k_im = _dft_filter(f_re, f_im, filt, ss)
    z_re, z_im = _dft_fwd(f_re, f_im, u, k_re, k_im, 0, row0)
    y, y_bf = _dft_inv(i_re, i_im, z_re, z_im, (u, row0, 0), (u, row0, 1), bias[0])
    z_re, z_im = _dft_fwd(f_re, f_im, y_bf, k_re, k_im, 1, 0)
    y, _ = _dft_inv(i_re, i_im, z_re, z_im, (y, 0, 0), (u, row0, 2), bias[1])
    return y


def _to_colmajor(t, rows):
    b, l, w = t.shape
    return t.reshape(b, rows, GRID_W, w).swapaxes(1, 2).reshape(b, l, w)


def _from_colmajor(t, rows):
    b, l, w = t.shape
    return t.reshape(b, GRID_W, rows, w).swapaxes(1, 2).reshape(b, l, w)


def _permute_w_in(w_in):
    o_ml = HG_COLS
    o_hy = o_ml + ML_MAIN + ML_GATES
    o_gd = o_hy + HY_COLS
    parts = [w_in[:, :HG_COLS], w_in[:, o_ml:o_ml + ML_MAIN], w_in[:, o_hy:o_gd],
             w_in[:, o_gd:o_gd + GD_MAIN], w_in[:, o_ml + ML_MAIN:o_hy],
             w_in[:, o_gd + GD_MAIN:],
             jnp.zeros((w_in.shape[0], GATE_PAD - ML_GATES - GD_GATES), w_in.dtype)]
    return jnp.concatenate(parts, 1).astype(BF16)


IN_SPLITS = (HG_COLS, ML_MAIN, HY_COLS, GD_MAIN, GATE_PAD)


def kernel(x, c, ctx, c_ctx, mod_w, mod_b, norm1_g, norm2_g, w_in, w_out, hg_lb_logits, hg_norm_g,
           ml_gate_b, ml_norm_g, hy_conv_w, hy_conv_b, hy_w1, hy_b1, hy_w2, hy_b2, hy_freq, hy_w3,
           hy_bias, gd_conv_w, gd_a_log, gd_dt_bias, gd_norm_g, router_w, router_b, exp_w1, exp_b1,
           exp_w2, exp_b2, final_g):
    n_batch, seq, d = x.shape
    ctx_len = ctx.shape[1]
    depth = mod_w.shape[0]
    rows = seq // GRID_W
    p = jax.nn.softmax(hg_lb_logits.astype(F32), axis=0)
    lower_bounds = jnp.cumsum(p, 0) - p[0]
    s_all = jnp.concatenate([jax.nn.silu(c), jax.nn.silu(c_ctx)[None]], 0)
    xs = jnp.concatenate([x, ctx], 1)
    lat = lambda t: t[:, :seq]
    cx = lambda t: t[:, seq:]
    colmajor = lambda t: jnp.concatenate([_to_colmajor(lat(t), rows), cx(t)], 1)
    rowmajor = lambda t: jnp.concatenate([_from_colmajor(lat(t), rows), cx(t)], 1)
    dft_lat, dft_ctx = _dft_matrices(seq), _dft_matrices(ctx_len)
    for l in range(depth):
        mod = (s_all @ mod_w[l] + mod_b[l]).reshape(n_batch + 1, 6, 1, d)
        sh1, sc1, g1, sh2, sc2, g2 = (mod[:, i] for i in range(6))
        z_hg, z_ml, z_hy, z_gd, z_gt = _in_proj(xs, norm1_g[l], sc1, sh1,
                                                _permute_w_in(w_in[l]), IN_SPLITS)
        hg_f, hg_b = _hgrn2_scan(z_hg, lower_bounds[l])
        ml_f, ml_b = _mlstm_scan(colmajor(z_ml), colmajor(z_gt), ml_gate_b[l], ctx_len)
        filt = (hy_w1[l], hy_b1[l], hy_w2[l], hy_b2[l], hy_freq[l], hy_w3[l])
        u_hy = _short_conv_tiles(z_hy, hy_conv_w[l], hy_conv_b[l], seq, False)
        c_lat = _hyena_segment(u_hy, 0, seq, dft_lat, filt, hy_bias[l])
        if l == depth - 1:
            c_ctx = jnp.zeros((n_batch, ctx_len, W_GROUP), F32)
        else:
            c_ctx = _hyena_segment(u_hy, seq, ctx_len, dft_ctx, filt, hy_bias[l])
        qkv = _short_conv_tiles(z_gd, gd_conv_w[l], jnp.zeros((3 * W_GROUP,), F32), seq, True)
        gd_f, gd_b = _gdn_scan(qkv, z_gt, gd_a_log[l], gd_dt_bias[l], ctx_len)
        mixers = [("silu", hg_f, hg_b, (z_hg, 4)),
                  ("sigmoid", rowmajor(ml_f), rowmajor(ml_b), (z_ml, 3)),
                  ("final", jnp.concatenate([c_lat, c_ctx], 1)),
                  ("silu", gd_f, gd_b, (z_gd, 3))]
        gains = jnp.stack([hg_norm_g[l], ml_norm_g[l], jnp.ones_like(ml_norm_g[l]), gd_norm_g[l]]).astype(F32)
        xs = _out_proj(xs, mixers, gains, g1, w_out[l].astype(BF16))
        xs = _moe_layer(xs, norm2_g[l], sc2, sh2, g2, router_w[l], router_b[l],
                        exp_w1[l], exp_b1[l], exp_w2[l], exp_b2[l])
    return _final_norm(xs, final_g, seq)
```

```python
import functools
import math

import jax
import jax.numpy as jnp
from jax import lax
from jax.experimental import pallas as pl
from jax.experimental.pallas import tpu as pltpu

F32 = jnp.float32
BF16 = jnp.bfloat16

D_MODEL = 1024
GRID_W = 64
N_MIXERS = 4
W_GROUP = D_MODEL // N_MIXERS
HG_HEADS = 4
HG_DK = W_GROUP // HG_HEADS
HG_CHUNK = 16
ML_HEADS = 4
ML_DH = W_GROUP // ML_HEADS
ML_CHUNK = 64
HY_ORDER = 2
HY_BANDS = 8
HY_FAST_DECAY = 0.3
HY_SLOW_DECAY = 1.5
HY_TARGET = 1e-2
GD_HEADS = 4
GD_DH = W_GROUP // GD_HEADS
GD_CHUNK = 64
N_EXPERTS = 32
TOP_K = 4
SWIGLU_LIMIT = 7.0
SWIGLU_ALPHA = 1.702
EPS = 1e-6
NEG = -1e30
HG_COLS = 5 * W_GROUP
ML_MAIN = 4 * W_GROUP
ML_GATES = 4 * ML_HEADS
HY_COLS = 3 * W_GROUP
GD_MAIN = 4 * W_GROUP
GD_GATES = 4 * GD_HEADS
GATE_PAD = 128

ROW_TILE = 256
MOE_TILE = 256
VMEM_LIMIT = 56 * 1024 * 1024


def _norm_mod(x, gain, sc, sh):
    y = x * lax.rsqrt(jnp.mean(x * x, -1, keepdims=True) + EPS)
    return y * gain * (1.0 + sc) + sh


def _in_proj_kernel(x_ref, g_ref, sc_ref, sh_ref, w_ref, *out_refs, splits):
    h = _norm_mod(x_ref[...], g_ref[...], sc_ref[...], sh_ref[...]).astype(BF16)
    off = 0
    for o_ref, n in zip(out_refs, splits):
        o_ref[...] = jnp.dot(h, w_ref[:, off:off + n], preferred_element_type=F32)
        off += n


def _mod_index_map(n_batch, n_tiles):
    return lambda b, j: (jnp.where(j == n_tiles - 1, n_batch, b), 0, 0)


def _in_proj(xs, gain, sc_tab, sh_tab, w, splits):
    n_batch, seq, d = xs.shape
    n = w.shape[1]
    grid = (n_batch, seq // ROW_TILE)
    mod_spec = pl.BlockSpec((None, 1, d), _mod_index_map(n_batch, seq // ROW_TILE))
    return pl.pallas_call(
        functools.partial(_in_proj_kernel, splits=splits),
        grid=grid,
        in_specs=[
            pl.BlockSpec((None, ROW_TILE, d), lambda b, j: (b, j, 0)),
            pl.BlockSpec((1, d), lambda b, j: (0, 0)),
            mod_spec, mod_spec,
            pl.BlockSpec((d, n), lambda b, j: (0, 0)),
        ],
        out_specs=[pl.BlockSpec((None, ROW_TILE, s), lambda b, j: (b, j, 0)) for s in splits],
        out_shape=[jax.ShapeDtypeStruct((n_batch, seq, s), F32) for s in splits],
        compiler_params=pltpu.CompilerParams(
            dimension_semantics=("parallel", "parallel"), vmem_limit_bytes=VMEM_LIMIT),
        name="in_proj",
    )(xs, gain.reshape(1, d), sc_tab, sh_tab, w)


def _moe_route_kernel(x_ref, g_ref, sc_ref, sh_ref, rw_ref, rb_ref, tri_ref, h_ref, idx_ref, gate_ref,
                      rank_ref, cnt_ref, carry):
    @pl.when(jnp.logical_and(pl.program_id(0) == 0, pl.program_id(1) == 0))
    def _():
        carry[...] = jnp.zeros_like(carry)

    h = _norm_mod(x_ref[...], g_ref[...], sc_ref[...], sh_ref[...])
    h_ref[...] = h
    work = _dot_hi(h, rw_ref[...]) + rb_ref[...]
    lane = lax.broadcasted_iota(jnp.int32, work.shape, 1)
    vals, hits = [], []
    for _ in range(TOP_K):
        m = jnp.max(work, axis=-1, keepdims=True)
        first = jnp.min(jnp.where(work == m, lane, GATE_PAD), axis=-1, keepdims=True)
        hit = lane == first
        vals.append(m)
        hits.append(hit)
        work = jnp.where(hit, -jnp.inf, work)
    exps = [jnp.exp(v - vals[0]) for v in vals]
    total = sum(exps)
    chosen = sum(hit.astype(F32) for hit in hits)
    before = jnp.dot(tri_ref[...], chosen.astype(BF16), preferred_element_type=F32) + carry[...]
    idx = jnp.zeros(work.shape, jnp.int32)
    gate = jnp.zeros(work.shape, F32)
    rank = jnp.zeros(work.shape, jnp.int32)
    for k in range(TOP_K):
        idx = jnp.where(lane == k, jnp.sum(jnp.where(hits[k], lane, 0), axis=-1, keepdims=True), idx)
        gate = jnp.where(lane == k, exps[k] / total, gate)
        r_k = jnp.sum(jnp.where(hits[k], before, 0.0), axis=-1, keepdims=True)
        rank = jnp.where(lane == k, r_k.astype(jnp.int32), rank)
    idx_ref[...] = idx
    gate_ref[...] = gate
    rank_ref[...] = rank
    carry[...] += jnp.sum(chosen, axis=0, keepdims=True)
    cnt_ref[...] = carry[...]


def _moe_route(xs, gain, sc_tab, sh_tab, router_w, router_b):
    n_batch, seq, d = xs.shape
    pad = GATE_PAD - N_EXPERTS
    rw = jnp.pad(router_w.astype(F32), ((0, 0), (0, pad)))
    rb = jnp.pad(router_b.astype(F32), (0, pad), constant_values=NEG).reshape(1, GATE_PAD)
    r = lax.broadcasted_iota(jnp.int32, (ROW_TILE, ROW_TILE), 0)
    c = lax.broadcasted_iota(jnp.int32, (ROW_TILE, ROW_TILE), 1)
    tri = (c < r).astype(BF16)
    mod_spec = pl.BlockSpec((None, 1, d), _mod_index_map(n_batch, seq // ROW_TILE))
    tok = lambda w: pl.BlockSpec((None, ROW_TILE, w), lambda b, j: (b, j, 0))
    return pl.pallas_call(
        _moe_route_kernel,
        grid=(n_batch, seq // ROW_TILE),
        in_specs=[
            tok(d),
            pl.BlockSpec((1, d), lambda b, j: (0, 0)),
            mod_spec, mod_spec,
            pl.BlockSpec((d, GATE_PAD), lambda b, j: (0, 0)),
            pl.BlockSpec((1, GATE_PAD), lambda b, j: (0, 0)),
            pl.BlockSpec((ROW_TILE, ROW_TILE), lambda b, j: (0, 0)),
        ],
        out_specs=[tok(d), tok(GATE_PAD), tok(GATE_PAD), tok(GATE_PAD),
                   pl.BlockSpec((1, GATE_PAD), lambda b, j: (0, 0))],
        out_shape=[jax.ShapeDtypeStruct((n_batch, seq, d), F32),
                   jax.ShapeDtypeStruct((n_batch, seq, GATE_PAD), jnp.int32),
                   jax.ShapeDtypeStruct((n_batch, seq, GATE_PAD), F32),
                   jax.ShapeDtypeStruct((n_batch, seq, GATE_PAD), jnp.int32),
                   jax.ShapeDtypeStruct((1, GATE_PAD), F32)],
        scratch_shapes=[pltpu.VMEM((1, GATE_PAD), F32)],
        compiler_params=pltpu.CompilerParams(
            dimension_semantics=("arbitrary", "arbitrary"), vmem_limit_bytes=VMEM_LIMIT),
        name="moe_route",
    )(xs, gain.reshape(1, d), sc_tab, sh_tab, rw, rb, tri)


def _out_proj_kernel(*refs, kinds):
    x_ref, refs = refs[0], refs[1:]
    gains_ref, g_ref, heads_ref, w_ref, o_ref = refs[-5:]
    heads = heads_ref[...] * (1.0 / HG_DK)
    acc = jnp.zeros(o_ref.shape, F32)
    pos = 0
    for i, kind in enumerate(kinds):
        if kind == "final":
            m = refs[pos][...]
            pos += 1
        else:
            o = refs[pos][...] + refs[pos + 1][...]
            gate = refs[pos + 2][...]
            pos += 3
            ms = _dot_hi(o * o, heads)
            o = o * lax.rsqrt(ms + EPS) * gains_ref[i:i + 1, :]
            m = o * (jax.nn.silu(gate) if kind == "silu" else jax.nn.sigmoid(gate))
        acc += jnp.dot(m.astype(BF16), w_ref[i * W_GROUP:(i + 1) * W_GROUP, :],
                       preferred_element_type=F32)
    o_ref[...] = x_ref[...] + g_ref[...] * acc


def _out_proj(xs, mixers, gains, gate_tab, w_out):
    n_batch, seq, d = xs.shape
    row = lambda w, cb=0: pl.BlockSpec((None, ROW_TILE, w), lambda b, j: (b, j, cb))
    args, specs, kinds = [], [], []
    for m in mixers:
        kinds.append(m[0])
        if m[0] == "final":
            args.append(m[1])
            specs.append(row(W_GROUP))
        else:
            src, cb = m[3]
            args += [m[1], m[2], src]
            specs += [row(W_GROUP), row(W_GROUP), row(W_GROUP, cb)]
    heads = _hg_consts()[3]
    return pl.pallas_call(
        functools.partial(_out_proj_kernel, kinds=tuple(kinds)),
        grid=(n_batch, seq // ROW_TILE),
        in_specs=[row(d)] + specs + [
            pl.BlockSpec((N_MIXERS, W_GROUP), lambda b, j: (0, 0)),
            pl.BlockSpec((None, 1, d), _mod_index_map(n_batch, seq // ROW_TILE)),
            pl.BlockSpec((W_GROUP, W_GROUP), lambda b, j: (0, 0)),
            pl.BlockSpec((d, d), lambda b, j: (0, 0)),
        ],
        out_specs=row(d),
        out_shape=jax.ShapeDtypeStruct((n_batch, seq, d), F32),
        compiler_params=pltpu.CompilerParams(
            dimension_semantics=("parallel", "parallel"), vmem_limit_bytes=VMEM_LIMIT),
        name="out_proj",
    )(xs, *args, gains, gate_tab, heads, w_out)


def _row_copy(src, src_row, dst, dst_row, sem):
    return pltpu.make_async_copy(src.at[pl.ds(src_row, 1), :], dst.at[pl.ds(dst_row, 1), :], sem)


def _dispatch_kernel(dest_ref, h_ref, slots_in_hbm, slots_hbm, sem):
    del slots_in_hbm

    def issue(r, carry):
        for k in range(TOP_K):
            _row_copy(h_ref, r, slots_hbm, dest_ref[0, r * TOP_K + k], sem).start()
        return carry

    lax.fori_loop(0, ROW_TILE, issue, 0)
    def drain(r, carry):
        for k in range(TOP_K):
            _row_copy(h_ref, r, slots_hbm, dest_ref[0, r * TOP_K + k], sem).wait()
        return carry

    lax.fori_loop(0, ROW_TILE, drain, 0)


def _moe_dispatch(h, dest, n_slots):
    t_count, d = h.shape
    return pl.pallas_call(
        _dispatch_kernel,
        grid=(t_count // ROW_TILE,),
        in_specs=[pl.BlockSpec((None, 1, ROW_TILE * TOP_K), lambda i: (i, 0, 0), memory_space=pltpu.SMEM),
                  pl.BlockSpec((ROW_TILE, d), lambda i: (i, 0)),
                  pl.BlockSpec(memory_space=pl.ANY)],
        out_specs=pl.BlockSpec(memory_space=pl.ANY),
        out_shape=jax.ShapeDtypeStruct((n_slots, d), F32),
        scratch_shapes=[pltpu.SemaphoreType.DMA(())],
        input_output_aliases={2: 0},
        compiler_params=pltpu.CompilerParams(
            dimension_semantics=("arbitrary",), vmem_limit_bytes=VMEM_LIMIT),
        name="moe_dispatch",
    )(dest, h, jnp.zeros((n_slots, d), F32))


def _expert_kernel(bexp_ref, nused_ref, x_ref, w1_ref, b1_ref, w2_ref, b2_ref, o_ref, w1b, w2b):
    i = pl.program_id(0)
    changed = jnp.logical_or(i == 0, bexp_ref[i] != bexp_ref[jnp.maximum(i - 1, 0)])
    used = i < nused_ref[0]

    @pl.when(jnp.logical_and(changed, used))
    def _():
        w1b[...] = w1_ref[...].astype(BF16)
        w2b[...] = w2_ref[...].astype(BF16)

    @pl.when(used)
    def _():
        f = w2b.shape[0]
        u = jnp.dot(x_ref[...].astype(BF16), w1b[...], preferred_element_type=F32) + b1_ref[...]
        gate = jnp.minimum(u[:, :f], SWIGLU_LIMIT)
        lin = jnp.clip(u[:, f:], -SWIGLU_LIMIT, SWIGLU_LIMIT)
        y = (lin + 1.0) * gate * jax.nn.sigmoid(SWIGLU_ALPHA * gate)
        o_ref[...] = jnp.dot(y.astype(BF16), w2b[...], preferred_element_type=F32) + b2_ref[...]

    @pl.when(jnp.logical_not(used))
    def _():
        o_ref[...] = jnp.zeros_like(o_ref)


def _expert_ffn(block_exp, n_used, xg, w1, b1, w2, b2):
    n_slots, d = xg.shape
    n_exp, _, f2 = w1.shape
    f = f2 // 2
    n_blocks = n_slots // MOE_TILE
    return pl.pallas_call(
        _expert_kernel,
        grid_spec=pltpu.PrefetchScalarGridSpec(
            num_scalar_prefetch=2,
            grid=(n_blocks,),
            in_specs=[
                pl.BlockSpec((MOE_TILE, d), lambda i, be, nu: (i, 0)),
                pl.BlockSpec((None, d, f2), lambda i, be, nu: (be[i], 0, 0)),
                pl.BlockSpec((None, 1, f2), lambda i, be, nu: (be[i], 0, 0)),
                pl.BlockSpec((None, f, d), lambda i, be, nu: (be[i], 0, 0)),
                pl.BlockSpec((None, 1, d), lambda i, be, nu: (be[i], 0, 0)),
            ],
            out_specs=pl.BlockSpec((MOE_TILE, d), lambda i, be, nu: (i, 0)),
            scratch_shapes=[pltpu.VMEM((d, f2), BF16), pltpu.VMEM((f, d), BF16)],
        ),
        out_shape=jax.ShapeDtypeStruct((n_slots, d), F32),
        compiler_params=pltpu.CompilerParams(
            dimension_semantics=("arbitrary",), vmem_limit_bytes=VMEM_LIMIT),
        name="expert_ffn",
    )(block_exp, n_used, xg, w1, b1.reshape(n_exp, 1, f2), w2, b2.reshape(n_exp, 1, d))


def _combine_kernel(dest_ref, y_hbm, x_ref, gate_ref, g_ref, o_ref, ybuf, sem):
    def issue(r, carry):
        for k in range(TOP_K):
            pltpu.make_async_copy(y_hbm.at[pl.ds(dest_ref[0, r * TOP_K + k], 1), :],
                                  ybuf.at[k, pl.ds(r, 1), :], sem).start()
        return carry

    lax.fori_loop(0, ROW_TILE, issue, 0)

    def drain(r, carry):
        for k in range(TOP_K):
            pltpu.make_async_copy(y_hbm.at[pl.ds(dest_ref[0, r * TOP_K + k], 1), :],
                                  ybuf.at[k, pl.ds(r, 1), :], sem).wait()
        return carry

    lax.fori_loop(0, ROW_TILE, drain, 0)
    acc = jnp.zeros(o_ref.shape, F32)
    for k in range(TOP_K):
        acc += gate_ref[:, k:k + 1] * ybuf[k]
    o_ref[...] = x_ref[...] + g_ref[...] * acc


def _moe_combine(xs, y_slots, dest, gates, gate_tab):
    n_batch, seq, d = xs.shape
    n_tiles = seq // ROW_TILE
    tok = lambda w: pl.BlockSpec((None, ROW_TILE, w), lambda b, j: (b, j, 0))
    return pl.pallas_call(
        _combine_kernel,
        grid=(n_batch, n_tiles),
        in_specs=[pl.BlockSpec((None, 1, ROW_TILE * TOP_K), lambda b, j: (b * n_tiles + j, 0, 0),
                               memory_space=pltpu.SMEM),
                  pl.BlockSpec(memory_space=pl.ANY),
                  tok(d), tok(GATE_PAD),
                  pl.BlockSpec((None, 1, d), _mod_index_map(n_batch, n_tiles))],
        out_specs=tok(d),
        out_shape=jax.ShapeDtypeStruct((n_batch, seq, d), F32),
        scratch_shapes=[pltpu.VMEM((TOP_K, ROW_TILE, d), F32), pltpu.SemaphoreType.DMA(())],
        compiler_params=pltpu.CompilerParams(
            dimension_semantics=("arbitrary", "arbitrary"), vmem_limit_bytes=VMEM_LIMIT),
        name="moe_combine",
    )(dest, y_slots, xs, gates, gate_tab)


def _moe_layer(xs, gain, sc_tab, sh_tab, gate_tab, router_w, router_b, w1, b1, w2, b2):
    n_batch, seq, d = xs.shape
    t_count = n_batch * seq
    h, idx, gates, rank, counts = _moe_route(xs, gain, sc_tab, sh_tab, router_w, router_b)
    counts = counts[0, :N_EXPERTS].astype(jnp.int32)
    padded = (counts + MOE_TILE - 1) // MOE_TILE * MOE_TILE
    pend = jnp.cumsum(padded)
    pstart = pend - padded
    top_i = idx.reshape(t_count, GATE_PAD)[:, :TOP_K]
    hit = top_i[:, :, None] == jnp.arange(N_EXPERTS, dtype=jnp.int32)
    dest = rank.reshape(t_count, GATE_PAD)[:, :TOP_K] + jnp.sum(jnp.where(hit, pstart, 0), -1)
    dest = dest.astype(jnp.int32).reshape(t_count // ROW_TILE, 1, ROW_TILE * TOP_K)
    n_blocks = -(-(t_count * TOP_K + N_EXPERTS * (MOE_TILE - 1)) // MOE_TILE)
    tile_start = jnp.arange(n_blocks, dtype=jnp.int32)[:, None] * MOE_TILE
    block_exp = jnp.minimum(jnp.sum((pend[None, :] <= tile_start).astype(jnp.int32), -1), N_EXPERTS - 1)
    n_used = (pend[-1:] // MOE_TILE).astype(jnp.int32)
    slots = _moe_dispatch(h.reshape(t_count, d), dest, n_blocks * MOE_TILE)
    y_slots = _expert_ffn(block_exp, n_used, slots, w1, b1, w2, b2)
    return _moe_combine(xs, y_slots, dest, gates, gate_tab)


def _final_norm_kernel(x_ref, g_ref, o_ref):
    x = x_ref[...]
    o_ref[...] = x * lax.rsqrt(jnp.mean(x * x, -1, keepdims=True) + EPS) * g_ref[...]


def _final_norm(xs, gain, n_lat):
    n_batch, _, d = xs.shape
    return pl.pallas_call(
        _final_norm_kernel,
        grid=(n_batch, n_lat // ROW_TILE),
        in_specs=[pl.BlockSpec((None, ROW_TILE, d), lambda b, j: (b, j, 0)),
                  pl.BlockSpec((1, d), lambda b, j: (0, 0))],
        out_specs=pl.BlockSpec((None, ROW_TILE, d), lambda b, j: (b, j, 0)),
        out_shape=jax.ShapeDtypeStruct((n_batch, n_lat, d), F32),
        compiler_params=pltpu.CompilerParams(dimension_semantics=("parallel", "parallel")),
        name="final_norm",
    )(xs, gain.reshape(1, d))


HG_SUB = 16


def _hg_consts():
    r = lax.broadcasted_iota(jnp.int32, (ROW_TILE, ROW_TILE), 0)
    c = lax.broadcasted_iota(jnp.int32, (ROW_TILE, ROW_TILE), 1)
    same_chunk = (r // HG_SUB) == (c // HG_SUB)
    tri_f = jnp.where(same_chunk & (c <= r), 1.0, 0.0).astype(F32)
    tri_b = jnp.where(same_chunk & (c >= r), 1.0, 0.0).astype(F32)
    blk = jnp.where(same_chunk, 1.0, 0.0).astype(F32)
    heads = jnp.where((r // HG_DK) == (c // HG_DK), 1.0, 0.0).astype(F32)
    return tri_f, tri_b, blk, heads


def _hg_prepare(z_ref, zf_col, lb, tri, blk, qs_s, key_s, cm_s, qh_s, kh_s, dec_s):
    q = jax.nn.silu(z_ref[:, 0:W_GROUP])
    zf = z_ref[:, zf_col:zf_col + W_GROUP]
    f = lb + (1.0 - lb) * jax.nn.sigmoid(zf)
    key = (1.0 - lb) * jax.nn.sigmoid(-zf)
    logf = jnp.log(f)
    cum = jnp.dot(tri, logf, preferred_element_type=F32, precision=lax.Precision.HIGHEST)
    tot = jnp.dot(blk, logf, preferred_element_type=F32, precision=lax.Precision.HIGHEST)
    qs_s[...] = q
    key_s[...] = key
    cm_s[...] = cum
    qh_s[...] = (q * jnp.exp(cum)).astype(BF16)
    kh_s[...] = (key * jnp.exp(tot - cum)).astype(BF16)
    dec_s[...] = jnp.exp(tot)


def _hg_chunk(c, reverse, z_ref, o_ref, st_ref, heads_bf, heads_f, qs_s, key_s, cm_s, qh_s, kh_s, dec_s):
    sl = pl.ds(pl.multiple_of(c * HG_SUB, HG_SUB), HG_SUB)
    q, k, cm = qs_s[sl, :], key_s[sl, :], cm_s[sl, :]
    v = z_ref[sl, W_GROUP:2 * W_GROUP]
    st = st_ref[...]
    o = lax.dot_general(qh_s[sl, :], st.astype(BF16), (((1,), (1,)), ((), ())),
                        preferred_element_type=F32)
    t_idx = lax.broadcasted_iota(jnp.int32, (HG_SUB, W_GROUP), 0)
    parts = []
    for s in range(HG_SUB):
        live = (t_idx <= s) if reverse else (t_idx >= s)
        e = jnp.exp(jnp.where(live, cm - cm[s:s + 1, :], NEG))
        parts.append((q * k[s:s + 1, :] * e).astype(BF16))
    r = jnp.dot(jnp.concatenate(parts, 0), heads_bf, preferred_element_type=F32)
    for s in range(HG_SUB):
        o += r[s * HG_SUB:(s + 1) * HG_SUB, :] * v[s:s + 1, :]
    o_ref[sl, :] = o
    ut = lax.dot_general(v.astype(BF16), kh_s[sl, :], (((0,), (0,)), ((), ())),
                         preferred_element_type=F32)
    st_ref[...] = st * dec_s[sl, :][0:1, :] + ut * heads_f


def _hgrn2_kernel(zf_ref, zb_ref, lb_ref, trif_ref, trib_ref, blk_ref, heads_ref, of_ref, ob_ref,
                  st_s, qs_s, key_s, cm_s, qh_s, kh_s, dec_s):
    @pl.when(pl.program_id(1) == 0)
    def _():
        st_s[...] = jnp.zeros_like(st_s)

    lb = lb_ref[...]
    blk = blk_ref[...]
    heads_f = heads_ref[...]
    heads_bf = heads_f.astype(BF16)
    _hg_prepare(zf_ref, 2 * W_GROUP, lb, trif_ref[...], blk, qs_s.at[0], key_s.at[0], cm_s.at[0],
                qh_s.at[0], kh_s.at[0], dec_s.at[0])
    _hg_prepare(zb_ref, 3 * W_GROUP, lb, trib_ref[...], blk, qs_s.at[1], key_s.at[1], cm_s.at[1],
                qh_s.at[1], kh_s.at[1], dec_s.at[1])
    n_chunks = ROW_TILE // HG_SUB

    def body(c, carry):
        _hg_chunk(c, False, zf_ref, of_ref, st_s.at[0], heads_bf, heads_f, qs_s.at[0], key_s.at[0],
                  cm_s.at[0], qh_s.at[0], kh_s.at[0], dec_s.at[0])
        _hg_chunk(n_chunks - 1 - c, True, zb_ref, ob_ref, st_s.at[1], heads_bf, heads_f, qs_s.at[1],
                  key_s.at[1], cm_s.at[1], qh_s.at[1], kh_s.at[1], dec_s.at[1])
        return carry

    lax.fori_loop(0, n_chunks, body, 0)


def _scan_tiles(n_lat_tiles):
    fwd = lambda j: jnp.where(j == 0, n_lat_tiles, j - 1)
    bwd = lambda j: jnp.where(j == 0, n_lat_tiles, n_lat_tiles - j)
    return fwd, bwd


def _hgrn2_scan(z_hg, lb):
    n_batch, seq, _ = z_hg.shape
    n_tiles = seq // ROW_TILE
    fwd, bwd = _scan_tiles(n_tiles - 1)
    const = pl.BlockSpec((ROW_TILE, ROW_TILE), lambda b, j: (0, 0))
    scr = lambda dt: pltpu.VMEM((2, ROW_TILE, W_GROUP), dt)
    return pl.pallas_call(
        _hgrn2_kernel,
        grid=(n_batch, n_tiles),
        in_specs=[pl.BlockSpec((None, ROW_TILE, HG_COLS), lambda b, j: (b, fwd(j), 0)),
                  pl.BlockSpec((None, ROW_TILE, HG_COLS), lambda b, j: (b, bwd(j), 0)),
                  pl.BlockSpec((1, W_GROUP), lambda b, j: (0, 0)),
                  const, const, const, const],
        out_specs=[pl.BlockSpec((None, ROW_TILE, W_GROUP), lambda b, j: (b, fwd(j), 0)),
                   pl.BlockSpec((None, ROW_TILE, W_GROUP), lambda b, j: (b, bwd(j), 0))],
        out_shape=[jax.ShapeDtypeStruct((n_batch, seq, W_GROUP), F32)] * 2,
        scratch_shapes=[pltpu.VMEM((2, W_GROUP, W_GROUP), F32), scr(F32), scr(F32), scr(F32),
                        scr(BF16), scr(BF16), scr(F32)],
        compiler_params=pltpu.CompilerParams(
            dimension_semantics=("parallel", "arbitrary"), vmem_limit_bytes=VMEM_LIMIT),
        name="hgrn2_scan",
    )(z_hg, z_hg, lb.reshape(1, W_GROUP), *_hg_consts())


def _dot_hi(a, b):
    return jnp.dot(a, b, preferred_element_type=F32, precision=lax.Precision.HIGHEST)


def _dot_nt(a, b, precision=None):
    return lax.dot_general(a, b, (((1,), (1,)), ((), ())), preferred_element_type=F32,
                           precision=precision)


def _dot_tn(a, b):
    return lax.dot_general(a, b, (((0,), (0,)), ((), ())), preferred_element_type=F32)


def _chunk_consts(chunk, n_heads, gate_cols):
    r = lax.broadcasted_iota(jnp.int32, (chunk, chunk), 0)
    c = lax.broadcasted_iota(jnp.int32, (chunk, chunk), 1)
    low = (c <= r).astype(F32)
    upp = (c >= r).astype(F32)
    dh = W_GROUP // n_heads
    hr = lax.broadcasted_iota(jnp.int32, (W_GROUP, W_GROUP), 0) // dh
    hc = lax.broadcasted_iota(jnp.int32, (W_GROUP, W_GROUP), 1) // dh
    heads = (hr == hc).astype(F32)
    gc = lax.broadcasted_iota(jnp.int32, (GATE_PAD, W_GROUP), 0)
    gh = lax.broadcasted_iota(jnp.int32, (GATE_PAD, W_GROUP), 1) // dh
    expand = jnp.stack([(gc == g0 + gh).astype(F32) for g0 in gate_cols])
    sr = lax.broadcasted_iota(jnp.int32, (8, GATE_PAD), 0)
    sc = lax.broadcasted_iota(jnp.int32, (8, GATE_PAD), 1)
    select = jnp.stack([((sc == g0 + sr) & (sr < n_heads)).astype(F32) for g0 in gate_cols])
    return low, upp, heads, expand, select


class _MlDir:
    def __init__(self, reverse, z_ref, g_ref, gb, c_ref, n_ref, m_ref, low, upp, heads,
                 ex_i, ex_f, sel_i, sel_f):
        self.q = z_ref[:, 0:W_GROUP]
        self.k = z_ref[:, W_GROUP:2 * W_GROUP] * (ML_DH ** -0.5)
        self.v = z_ref[:, 2 * W_GROUP:3 * W_GROUP]
        gates = g_ref[...] + gb
        col_sum, row_sum = (upp, low) if reverse else (low, upp)
        self.li = _dot_hi(gates, ex_i)
        lf = jax.nn.log_sigmoid(_dot_hi(gates, ex_f))
        self.li_rows = _dot_nt(sel_i, gates, lax.Precision.HIGHEST)
        lf_rows = jax.nn.log_sigmoid(_dot_nt(sel_f, gates, lax.Precision.HIGHEST))
        self.fcum = _dot_hi(col_sum, lf)
        self.fcum_rows = _dot_hi(lf_rows, row_sum)
        last = 0 if reverse else ML_CHUNK - 1
        self.ftot = self.fcum[last:last + 1, :]
        g = self.ftot - self.fcum + self.li
        self.g_max = jnp.max(g, axis=0, keepdims=True)
        self.kw = self.k * jnp.exp(g - self.g_max)
        self.c0, self.n0, self.m0 = c_ref[...], n_ref[...], m_ref[...]
        self.k_bf = self.k.astype(BF16)
        self.qc = jnp.dot(self.q.astype(BF16), self.c0.astype(BF16), preferred_element_type=F32)
        self.qn = jnp.dot((self.q * self.n0).astype(BF16), heads.astype(BF16), preferred_element_type=F32)
        self.a_init = self.fcum + self.m0
        rr = lax.broadcasted_iota(jnp.int32, (ML_CHUNK, ML_CHUNK), 0)
        cc = lax.broadcasted_iota(jnp.int32, (ML_CHUNK, ML_CHUNK), 1)
        self.live = (cc >= rr) if reverse else (cc <= rr)


def _mlstm_kernel(zf_ref, zb_ref, gf_ref, gb_ref, bias_ref, low_ref, upp_ref, heads_ref, ex_ref, sel_ref,
                  of_ref, ob_ref, c_s, n_s, m_s):
    @pl.when(pl.program_id(1) == 0)
    def _():
        c_s[...] = jnp.zeros_like(c_s)
        n_s[...] = jnp.zeros_like(n_s)
        m_s[...] = jnp.zeros_like(m_s)

    low, upp, heads, bias = low_ref[...], upp_ref[...], heads_ref[...], bias_ref[...]
    dirs = [_MlDir(False, zf_ref, gf_ref, bias, c_s.at[0], n_s.at[0], m_s.at[0], low, upp, heads,
                   ex_ref[0], ex_ref[1], sel_ref[0], sel_ref[1]),
            _MlDir(True, zb_ref, gb_ref, bias, c_s.at[1], n_s.at[1], m_s.at[1], low, upp, heads,
                   ex_ref[2], ex_ref[3], sel_ref[2], sel_ref[3])]
    lane_head = lax.broadcasted_iota(jnp.int32, (ML_CHUNK, W_GROUP), 1) // ML_DH
    hs = lambda h: slice(h * ML_DH, (h + 1) * ML_DH)
    probs = [(d, h) for d in dirs for h in range(ML_HEADS)]
    qks = [_dot_nt(jnp.where(lane_head == h, d.q, 0.0).astype(BF16), d.k_bf) for d, h in probs]
    a = [jnp.where(d.live, d.fcum[:, hs(h)] - d.fcum_rows[h:h + 1, :] + d.li_rows[h:h + 1, :], NEG)
         for d, h in probs]
    m_t = [jnp.maximum(jnp.max(a_h, axis=1, keepdims=True), d.a_init[:, hs(h)])
           for (d, h), a_h in zip(probs, a)]
    p = [jnp.exp(a_h - m) * qk for a_h, m, qk in zip(a, m_t, qks)]
    pv = [jnp.dot(p_h.astype(BF16), d.v[:, hs(h)].astype(BF16), preferred_element_type=F32)
          for (d, h), p_h in zip(probs, p)]
    outs = []
    for (d, h), p_h, pv_h, m in zip(probs, p, pv, m_t):
        e0 = jnp.exp(d.a_init[:, hs(h)] - m)
        num = pv_h + e0 * d.qc[:, hs(h)]
        den = jnp.sum(p_h, axis=1, keepdims=True) + e0 * d.qn[:, hs(h)]
        outs.append(num / jnp.maximum(jnp.abs(den), jnp.exp(-m)))
    u_c = [_dot_tn(d.kw.astype(BF16), d.v.astype(BF16)) * heads for d in dirs]
    for i, (d, o_ref) in enumerate(zip(dirs, (of_ref, ob_ref))):
        o_ref[...] = jnp.concatenate(outs[i * ML_HEADS:(i + 1) * ML_HEADS], axis=1)
        m_new = jnp.maximum(d.ftot + d.m0, d.g_max)
        a_old = jnp.exp(d.ftot + d.m0 - m_new)
        a_new = jnp.exp(d.g_max - m_new)
        c_s[i] = a_old * d.c0 + a_new * u_c[i]
        n_s[i] = a_old * d.n0 + a_new * jnp.sum(d.kw, axis=0, keepdims=True)
        m_s[i] = m_new


def _scan_chunks(n_lat, n_ctx):
    fwd = lambda j: jnp.where(j < n_ctx, n_lat + j, j - n_ctx)
    bwd = lambda j: jnp.where(j < n_ctx, n_lat + n_ctx - 1 - j, n_lat + n_ctx - 1 - j)
    return fwd, bwd


def _mlstm_scan(z_ml, z_gt, gate_b, ctx_len):
    n_batch, seq, _ = z_ml.shape
    n_chunks = seq // ML_CHUNK
    n_ctx = ctx_len // ML_CHUNK
    fwd, bwd = _scan_chunks(n_chunks - n_ctx, n_ctx)
    low, upp, heads, expand, select = _chunk_consts(ML_CHUNK, ML_HEADS, (0, 4, 8, 12))
    bias = jnp.pad(gate_b.astype(F32).reshape(1, ML_GATES), ((0, 0), (0, GATE_PAD - ML_GATES)))
    full = lambda a: pl.BlockSpec(a.shape, lambda b, j: (0,) * a.ndim)
    return pl.pallas_call(
        _mlstm_kernel,
        grid=(n_batch, n_chunks),
        in_specs=[pl.BlockSpec((None, ML_CHUNK, ML_MAIN), lambda b, j: (b, fwd(j), 0)),
                  pl.BlockSpec((None, ML_CHUNK, ML_MAIN), lambda b, j: (b, bwd(j), 0)),
                  pl.BlockSpec((None, ML_CHUNK, GATE_PAD), lambda b, j: (b, fwd(j), 0)),
                  pl.BlockSpec((None, ML_CHUNK, GATE_PAD), lambda b, j: (b, bwd(j), 0)),
                  full(bias), full(low), full(upp), full(heads), full(expand), full(select)],
        out_specs=[pl.BlockSpec((None, ML_CHUNK, W_GROUP), lambda b, j: (b, fwd(j), 0)),
                   pl.BlockSpec((None, ML_CHUNK, W_GROUP), lambda b, j: (b, bwd(j), 0))],
        out_shape=[jax.ShapeDtypeStruct((n_batch, seq, W_GROUP), F32)] * 2,
        scratch_shapes=[pltpu.VMEM((2, W_GROUP, W_GROUP), F32), pltpu.VMEM((2, 1, W_GROUP), F32),
                        pltpu.VMEM((2, 1, W_GROUP), F32)],
        compiler_params=pltpu.CompilerParams(
            dimension_semantics=("parallel", "arbitrary"), vmem_limit_bytes=VMEM_LIMIT),
        name="mlstm_scan",
    )(z_ml, z_ml, z_gt, z_gt, bias, low, upp, heads, expand, select)


HALO = 8


def _short_conv_kernel(x_ref, prev_ref, next_ref, w_ref, b_ref, heads_ref, o_ref, *, n_lat_tiles, gdn):
    j = pl.program_id(1)
    n_tiles = pl.num_programs(1)
    x = x_ref[...]
    has_prev = jnp.logical_and(j != 0, j != n_lat_tiles)
    has_next = jnp.logical_and(j != n_lat_tiles - 1, j != n_tiles - 1)
    prev_row = jnp.where(has_prev, prev_ref[HALO - 1:HALO, :], 0.0)
    next_row = jnp.where(has_next, next_ref[0:1, :], 0.0)
    row = lax.broadcasted_iota(jnp.int32, x.shape, 0)
    x_prev = jnp.where(row == 0, prev_row, pltpu.roll(x, 1, 0))
    x_next = jnp.where(row == ROW_TILE - 1, next_row, pltpu.roll(x, ROW_TILE - 1, 0))
    y = x_prev * w_ref[0:1, :] + x * w_ref[1:2, :] + x_next * w_ref[2:3, :] + b_ref[...]
    if not gdn:
        o_ref[...] = y
        return
    y = jax.nn.silu(y)
    heads = heads_ref[...]
    for i in range(2):
        t = y[:, i * W_GROUP:(i + 1) * W_GROUP]
        o_ref[:, i * W_GROUP:(i + 1) * W_GROUP] = t * lax.rsqrt(_dot_hi(t * t, heads) + EPS)
    o_ref[:, 2 * W_GROUP:] = y[:, 2 * W_GROUP:]


def _head_mask():
    r = lax.broadcasted_iota(jnp.int32, (W_GROUP, W_GROUP), 0) // HG_DK
    c = lax.broadcasted_iota(jnp.int32, (W_GROUP, W_GROUP), 1) // HG_DK
    return (r == c).astype(F32)


def _short_conv_tiles(z, conv_w, conv_b, n_lat, gdn):
    n_batch, seq, _ = z.shape
    c = conv_w.shape[1]
    n_tiles = seq // ROW_TILE
    per = ROW_TILE // HALO
    n_halo = seq // HALO
    return pl.pallas_call(
        functools.partial(_short_conv_kernel, n_lat_tiles=n_lat // ROW_TILE, gdn=gdn),
        grid=(n_batch, n_tiles),
        in_specs=[pl.BlockSpec((None, ROW_TILE, c), lambda b, j: (b, j, 0)),
                  pl.BlockSpec((None, HALO, c), lambda b, j: (b, jnp.maximum(j * per - 1, 0), 0)),
                  pl.BlockSpec((None, HALO, c), lambda b, j: (b, jnp.minimum((j + 1) * per, n_halo - 1), 0)),
                  pl.BlockSpec((3, c), lambda b, j: (0, 0)),
                  pl.BlockSpec((1, c), lambda b, j: (0, 0)),
                  pl.BlockSpec((W_GROUP, W_GROUP), lambda b, j: (0, 0))],
        out_specs=pl.BlockSpec((None, ROW_TILE, c), lambda b, j: (b, j, 0)),
        out_shape=jax.ShapeDtypeStruct((n_batch, seq, c), F32),
        compiler_params=pltpu.CompilerParams(
            dimension_semantics=("parallel", "parallel"), vmem_limit_bytes=VMEM_LIMIT),
        name="gdn_conv" if gdn else "hyena_conv",
    )(z, z, z, conv_w.astype(F32), conv_b.astype(F32).reshape(1, c), _head_mask())


def _gd_consts():
    r = lax.broadcasted_iota(jnp.int32, (GD_CHUNK, GD_CHUNK), 0)
    c = lax.broadcasted_iota(jnp.int32, (GD_CHUNK, GD_CHUNK), 1)
    same = lambda n: (r // n) == (c // n)
    eye = (r == c).astype(F32)
    masks = [same(8).astype(F32)] + [(same(2 * b) & ~same(b)).astype(F32) for b in (8, 16, 32)]
    return eye, jnp.stack(masks)


def _unit_tri_inverses(ns, eye, masks):
    mm = lambda a, b: jnp.dot(a.astype(BF16), b.astype(BF16), preferred_element_type=F32)
    n8 = [n * masks[0] for n in ns]
    n2 = [mm(a, a) for a in n8]
    n4 = [mm(a, a) for a in n2]
    ts = [mm(eye - a, eye + b) for a, b in zip(n8, n2)]
    ts = [mm(t, eye + b) for t, b in zip(ts, n4)]
    for i in range(1, 4):
        lts = [mm(n * masks[i], t) for n, t in zip(ns, ts)]
        ts = [t - mm(t, lt) for t, lt in zip(ts, lts)]
    return ts


class _GdDir:
    def __init__(self, reverse, x_ref, g_ref, s_ref, prm_lane, prm_row, low, upp, ex_a, ex_b, sel_a):
        self.q = x_ref[:, 0:W_GROUP] * (GD_DH ** -0.5)
        self.k = x_ref[:, W_GROUP:2 * W_GROUP]
        self.v = x_ref[:, 2 * W_GROUP:3 * W_GROUP]
        gates = g_ref[...]
        col_sum, row_sum = (upp, low) if reverse else (low, upp)
        log_a = prm_lane[0:1, :] * jax.nn.softplus(_dot_hi(gates, ex_a) + prm_lane[1:2, :])
        self.beta = jax.nn.sigmoid(_dot_hi(gates, ex_b))
        log_a_rows = prm_row[0] * jax.nn.softplus(_dot_nt(sel_a, gates, lax.Precision.HIGHEST) + prm_row[1])
        self.g = _dot_hi(col_sum, log_a)
        self.g_rows = _dot_hi(log_a_rows, row_sum)
        last = 0 if reverse else GD_CHUNK - 1
        self.g_last = self.g[last:last + 1, :]
        self.eg = jnp.exp(self.g)
        self.k_bf = self.k.astype(BF16)
        self.kb = self.k * self.beta
        self.s0 = s_ref[...]
        rr = lax.broadcasted_iota(jnp.int32, (GD_CHUNK, GD_CHUNK), 0)
        cc = lax.broadcasted_iota(jnp.int32, (GD_CHUNK, GD_CHUNK), 1)
        self.live = (cc >= rr) if reverse else (cc <= rr)
        self.strict = (cc > rr) if reverse else (cc < rr)


def _gdn_kernel(xf_ref, xb_ref, gf_ref, gb_ref, prm_lane_ref, prm_row_ref, low_ref, upp_ref, heads_ref,
                ex_ref, sel_ref, eye_ref, masks_ref, of_ref, ob_ref, s_s):
    @pl.when(pl.program_id(1) == 0)
    def _():
        s_s[...] = jnp.zeros_like(s_s)

    low, upp, heads, eye, masks = low_ref[...], upp_ref[...], heads_ref[...], eye_ref[...], masks_ref[...]
    dirs = [_GdDir(False, xf_ref, gf_ref, s_s.at[0], prm_lane_ref[0], prm_row_ref[0], low, upp,
                   ex_ref[0], ex_ref[1], sel_ref[0]),
            _GdDir(True, xb_ref, gb_ref, s_s.at[1], prm_lane_ref[1], prm_row_ref[1], low, upp,
                   ex_ref[2], ex_ref[3], sel_ref[2])]
    lane_head = lax.broadcasted_iota(jnp.int32, (GD_CHUNK, W_GROUP), 1) // GD_DH
    probs = [(d, h) for d in dirs for h in range(GD_HEADS)]
    only = lambda h, t: jnp.where(lane_head == h, t, 0.0).astype(BF16)
    hs = lambda h: slice(h * GD_DH, (h + 1) * GD_DH)
    decay = [jnp.exp(jnp.where(d.live, d.g[:, hs(h)] - d.g_rows[h:h + 1, :], NEG)) for d, h in probs]
    kks = [_dot_nt(only(h, d.kb), d.k_bf) for d, h in probs]
    ns = [jnp.where(d.strict, kk * dc, 0.0) for (d, h), kk, dc in zip(probs, kks, decay)]
    t_inv = [t.astype(BF16) for t in _unit_tri_inverses(ns, eye, masks)]
    us = [jnp.dot(t, only(h, d.v * d.beta), preferred_element_type=F32) for (d, h), t in zip(probs, t_inv)]
    ws = [jnp.dot(t, only(h, d.kb * d.eg), preferred_element_type=F32) for (d, h), t in zip(probs, t_inv)]
    qks = [(_dot_nt(only(h, d.q), d.k_bf) * dc).astype(BF16) for (d, h), dc in zip(probs, decay)]
    v_new, o = [], []
    for i, d in enumerate(dirs):
        sl = slice(i * GD_HEADS, (i + 1) * GD_HEADS)
        s_bf = d.s0.astype(BF16)
        v_new.append(sum(us[sl]) - jnp.dot(sum(ws[sl]).astype(BF16), s_bf, preferred_element_type=F32))
        o.append(jnp.dot((d.q * d.eg).astype(BF16), s_bf, preferred_element_type=F32))
    intra = [jnp.dot(qk, only(h, v_new[i // GD_HEADS]), preferred_element_type=F32)
             for i, ((d, h), qk) in enumerate(zip(probs, qks))]
    for i, (d, o_ref) in enumerate(zip(dirs, (of_ref, ob_ref))):
        o_ref[...] = o[i] + sum(intra[i * GD_HEADS:(i + 1) * GD_HEADS])
        k_dec = d.k * jnp.exp(d.g_last - d.g)
        s_s[i] = jnp.exp(d.g_last) * d.s0 + _dot_tn(k_dec.astype(BF16), v_new[i].astype(BF16)) * heads


def _gdn_scan(qkv, z_gt, a_log, dt_bias, ctx_len):
    n_batch, seq, _ = qkv.shape
    n_chunks = seq // GD_CHUNK
    n_ctx = ctx_len // GD_CHUNK
    fwd, bwd = _scan_chunks(n_chunks - n_ctx, n_ctx)
    g0 = ML_GATES
    low, upp, heads, expand, select = _chunk_consts(GD_CHUNK, GD_HEADS, (g0, g0 + 4, g0 + 8, g0 + 12))
    eye, masks = _gd_consts()
    prm = jnp.stack([-jnp.exp(a_log.astype(F32)), dt_bias.astype(F32)], 1)
    prm_lane = jnp.repeat(prm, GD_DH, axis=2)
    prm_row = jnp.broadcast_to(jnp.pad(prm, ((0, 0), (0, 0), (0, 8 - GD_HEADS)))[..., None],
                               (2, 2, 8, GD_CHUNK))
    full = lambda a: pl.BlockSpec(a.shape, lambda b, j: (0,) * a.ndim)
    return pl.pallas_call(
        _gdn_kernel,
        grid=(n_batch, n_chunks),
        in_specs=[pl.BlockSpec((None, GD_CHUNK, 3 * W_GROUP), lambda b, j: (b, fwd(j), 0)),
                  pl.BlockSpec((None, GD_CHUNK, 3 * W_GROUP), lambda b, j: (b, bwd(j), 0)),
                  pl.BlockSpec((None, GD_CHUNK, GATE_PAD), lambda b, j: (b, fwd(j), 0)),
                  pl.BlockSpec((None, GD_CHUNK, GATE_PAD), lambda b, j: (b, bwd(j), 0)),
                  full(prm_lane), full(prm_row), full(low), full(upp), full(heads), full(expand),
                  full(select), full(eye), full(masks)],
        out_specs=[pl.BlockSpec((None, GD_CHUNK, W_GROUP), lambda b, j: (b, fwd(j), 0)),
                   pl.BlockSpec((None, GD_CHUNK, W_GROUP), lambda b, j: (b, bwd(j), 0))],
        out_shape=[jax.ShapeDtypeStruct((n_batch, seq, W_GROUP), F32)] * 2,
        scratch_shapes=[pltpu.VMEM((2, W_GROUP, W_GROUP), F32)],
        compiler_params=pltpu.CompilerParams(
            dimension_semantics=("parallel", "arbitrary"), vmem_limit_bytes=VMEM_LIMIT),
        name="gdn_scan",
    )(qkv, qkv, z_gt, z_gt, prm_lane, prm_row, low, upp, heads, expand, select, eye, masks)


HY_FILT_TILE = 512
DFT_TILE = 512


def _dft_matrices(length):
    n = 2 * length
    k = lax.broadcasted_iota(jnp.int32, (length, length), 0)
    t = lax.broadcasted_iota(jnp.int32, (length, length), 1)
    ang = ((k * t) % n).astype(F32) * (2.0 * math.pi / n)
    f_re = jnp.cos(ang)
    f_im = jnp.where(k == 0, jnp.where(t % 2 == 0, 1.0, -1.0), -jnp.sin(ang))
    c = jnp.where(k == 0, 1.0 / n, 2.0 / n)
    return (f_re.astype(BF16), f_im.astype(BF16), (c * f_re).T.astype(BF16), (c * f_im).T.astype(BF16))


def _hy_filter_kernel(feat_ref, w1_ref, b1_ref, w2_ref, b2_ref, fr_ref, w3_ref, win_ref, filt_ref, ss_ref):
    i = pl.program_id(0)
    hdn = jnp.sin(fr_ref[0:1, :] * (_dot_hi(feat_ref[...], w1_ref[...]) + b1_ref[...]))
    hdn = jnp.sin(fr_ref[1:2, :] * (_dot_hi(hdn, w2_ref[...]) + b2_ref[...]))
    filt = _dot_hi(hdn, w3_ref[...]) * win_ref[...]
    half = filt.shape[1] // 2
    row = lax.broadcasted_iota(jnp.int32, filt.shape, 0) + i * filt.shape[0]
    col = lax.broadcasted_iota(jnp.int32, filt.shape, 1)
    filt = jnp.where(jnp.logical_and(row == 0, col >= half), 0.0, filt)
    filt_ref[...] = filt.astype(BF16)
    sq = jnp.sum(filt * filt, axis=0, keepdims=True)

    @pl.when(i == 0)
    def _():
        ss_ref[...] = jnp.zeros_like(ss_ref)

    ss_ref[...] += sq[:, :half] + sq[:, half:]


def _hyena_filters(length, w1, b1, w2, b2, freq, w3):
    tile = min(HY_FILT_TILE, length)
    t = jnp.linspace(0.0, 1.0, length, dtype=F32)[:, None]
    pos = jnp.arange(length, dtype=F32)[:, None]
    band = jnp.linspace(1e-4, HY_BANDS - 1, HY_BANDS, dtype=F32)[None, :]
    ang = 2.0 * math.pi * pos * band / length
    feats = jnp.concatenate([t, jnp.cos(ang), -jnp.sin(ang)], -1)
    deltas = jnp.abs(jnp.linspace(math.log(HY_TARGET) / HY_SLOW_DECAY,
                                  math.log(HY_TARGET) / HY_FAST_DECAY, W_GROUP, dtype=F32))
    window = jnp.tile(jnp.exp(-t * deltas), (1, 2 * HY_ORDER))
    n_out = w3.shape[1]
    emb, ffn = w1.shape
    full = lambda a: pl.BlockSpec(a.shape, lambda i: (0,) * a.ndim)
    args = [w1.astype(F32), b1.astype(F32).reshape(1, ffn), w2.astype(F32), b2.astype(F32).reshape(1, ffn),
            freq.astype(F32), w3.astype(F32)]
    filt, ss = pl.pallas_call(
        _hy_filter_kernel,
        grid=(length // tile,),
        in_specs=[pl.BlockSpec((tile, emb), lambda i: (i, 0))] + [full(a) for a in args] +
                 [pl.BlockSpec((tile, n_out), lambda i: (i, 0))],
        out_specs=[pl.BlockSpec((tile, n_out), lambda i: (i, 0)),
                   pl.BlockSpec((1, n_out // 2), lambda i: (0, 0))],
        out_shape=[jax.ShapeDtypeStruct((length, n_out), BF16),
                   jax.ShapeDtypeStruct((1, n_out // 2), F32)],
        compiler_params=pltpu.CompilerParams(
            dimension_semantics=("arbitrary",), vmem_limit_bytes=VMEM_LIMIT),
        name="hyena_filter",
    )(feats, *args, window)
    return filt, ss


def _spec_mul(yr, yi, kr, ki, first):
    row0 = jnp.logical_and(first, lax.broadcasted_iota(jnp.int32, yr.shape, 0) == 0)
    zr = yr * kr - jnp.where(row0, 0.0, yi * ki)
    zi = jnp.where(row0, yi * ki, yr * ki + yi * kr)
    return zr, zi


def _dft_filter_kernel(wre_ref, wim_ref, x_ref, ss_ref, kre_ref, kim_ref):
    x = x_ref[...]
    yr = jnp.dot(wre_ref[...], x, preferred_element_type=F32)
    yi = jnp.dot(wim_ref[...], x, preferred_element_type=F32)
    half = yr.shape[1] // 2
    scale = lax.rsqrt(ss_ref[...] + EPS)
    row0 = jnp.logical_and(pl.program_id(0) == 0,
                           lax.broadcasted_iota(jnp.int32, (yr.shape[0], half), 0) == 0)
    kre_ref[...] = (yr[:, :half] + yr[:, half:]) * scale
    kim_ref[...] = (yi[:, :half] + jnp.where(row0, yi[:, half:], -yi[:, half:])) * scale


def _dft_filter(f_re, f_im, filt, ss):
    length, n = filt.shape
    tile = min(DFT_TILE, length)
    wspec = pl.BlockSpec((tile, length), lambda i: (i, 0))
    return pl.pallas_call(
        _dft_filter_kernel,
        grid=(length // tile,),
        in_specs=[wspec, wspec, pl.BlockSpec((length, n), lambda i: (0, 0)),
                  pl.BlockSpec((1, n // 2), lambda i: (0, 0))],
        out_specs=[pl.BlockSpec((tile, n // 2), lambda i: (i, 0))] * 2,
        out_shape=[jax.ShapeDtypeStruct((length, n // 2), F32)] * 2,
        compiler_params=pltpu.CompilerParams(
            dimension_semantics=("parallel",), vmem_limit_bytes=VMEM_LIMIT),
        name="hyena_filter_dft",
    )(f_re, f_im, filt, ss)


def _dft_fwd_kernel(wre_ref, wim_ref, y_ref, kre_ref, kim_ref, zre_ref, zim_ref):
    y = y_ref[...].astype(BF16)
    yr = jnp.dot(wre_ref[...], y, preferred_element_type=F32)
    yi = jnp.dot(wim_ref[...], y, preferred_element_type=F32)
    zr, zi = _spec_mul(yr, yi, kre_ref[...], kim_ref[...], pl.program_id(0) == 0)
    zre_ref[...] = zr.astype(BF16)
    zim_ref[...] = zi.astype(BF16)


def _dft_fwd(f_re, f_im, y, k_re, k_im, order, row0):
    n_batch = y.shape[0]
    length = f_re.shape[0]
    tile = min(DFT_TILE, length)
    wspec = pl.BlockSpec((tile, length), lambda i, b: (i, 0))
    kspec = pl.BlockSpec((tile, W_GROUP), lambda i, b: (i, order))
    zspec = pl.BlockSpec((None, tile, W_GROUP), lambda i, b: (b, i, 0))
    return pl.pallas_call(
        _dft_fwd_kernel,
        grid=(length // tile, n_batch),
        in_specs=[wspec, wspec,
                  pl.BlockSpec((None, length, W_GROUP), lambda i, b: (b, row0 // length, 0)),
                  kspec, kspec],
        out_specs=[zspec, zspec],
        out_shape=[jax.ShapeDtypeStruct((n_batch, length, W_GROUP), BF16)] * 2,
        compiler_params=pltpu.CompilerParams(
            dimension_semantics=("parallel", "arbitrary"), vmem_limit_bytes=VMEM_LIMIT),
        name="hyena_dft_fwd",
    )(f_re, f_im, y, k_re, k_im)


def _dft_inv_kernel(wre_ref, wim_ref, zre_ref, zim_ref, y_ref, gate_ref, bias_ref, o_ref, obf_ref):
    conv = (jnp.dot(wre_ref[...], zre_ref[...], preferred_element_type=F32) +
            jnp.dot(wim_ref[...], zim_ref[...], preferred_element_type=F32))
    out = gate_ref[...] * (conv + bias_ref[...] * y_ref[...])
    o_ref[...] = out
    obf_ref[...] = out.astype(BF16)


def _dft_inv(i_re, i_im, z_re, z_im, y_src, gate_src, bias):
    n_batch, length, _ = z_re.shape
    tile = min(DFT_TILE, length)
    wspec = pl.BlockSpec((tile, length), lambda i, b: (i, 0))
    zspec = pl.BlockSpec((None, length, W_GROUP), lambda i, b: (b, 0, 0))
    ospec = pl.BlockSpec((None, tile, W_GROUP), lambda i, b: (b, i, 0))
    view = lambda src: pl.BlockSpec((None, tile, W_GROUP), lambda i, b: (b, src[1] // tile + i, src[2]))
    return pl.pallas_call(
        _dft_inv_kernel,
        grid=(length // tile, n_batch),
        in_specs=[wspec, wspec, zspec, zspec, view(y_src), view(gate_src),
                  pl.BlockSpec((1, W_GROUP), lambda i, b: (0, 0))],
        out_specs=[ospec, ospec],
        out_shape=[jax.ShapeDtypeStruct((n_batch, length, W_GROUP), F32),
                   jax.ShapeDtypeStruct((n_batch, length, W_GROUP), BF16)],
        compiler_params=pltpu.CompilerParams(
            dimension_semantics=("parallel", "arbitrary"), vmem_limit_bytes=VMEM_LIMIT),
        name="hyena_dft_inv",
    )(i_re, i_im, z_re, z_im, y_src[0], gate_src[0], bias.astype(F32).reshape(1, W_GROUP))


def _hyena_segment(u, row0, length, dft, filt_params, bias):
    f_re, f_im, i_re, i_im = dft
    filt, ss = _hyena_filters(length, *filt_params)
    k_re, k_im = _dft_filter(f_re, f_im, filt, ss)
    z_re, z_im = _dft_fwd(f_re, f_im, u, k_re, k_im, 0, row0)
    y, y_bf = _dft_inv(i_re, i_im, z_re, z_im, (u, row0, 0), (u, row0, 1), bias[0])
    z_re, z_im = _dft_fwd(f_re, f_im, y_bf, k_re, k_im, 1, 0)
    y, _ = _dft_inv(i_re, i_im, z_re, z_im, (y, 0, 0), (u, row0, 2), bias[1])
    return y


def _to_colmajor(t, rows):
    b, l, w = t.shape
    return t.reshape(b, rows, GRID_W, w).swapaxes(1, 2).reshape(b, l, w)


def _from_colmajor(t, rows):
    b, l, w = t.shape
    return t.reshape(b, GRID_W, rows, w).swapaxes(1, 2).reshape(b, l, w)


def _permute_w_in(w_in):
    o_ml = HG_COLS
    o_hy = o_ml + ML_MAIN + ML_GATES
    o_gd = o_hy + HY_COLS
    parts = [w_in[:, :HG_COLS], w_in[:, o_ml:o_ml + ML_MAIN], w_in[:, o_hy:o_gd],
             w_in[:, o_gd:o_gd + GD_MAIN], w_in[:, o_ml + ML_MAIN:o_hy],
             w_in[:, o_gd + GD_MAIN:],
             jnp.zeros((w_in.shape[0], GATE_PAD - ML_GATES - GD_GATES), w_in.dtype)]
    return jnp.concatenate(parts, 1).astype(BF16)


IN_SPLITS = (HG_COLS, ML_MAIN, HY_COLS, GD_MAIN, GATE_PAD)


def kernel(x, c, ctx, c_ctx, mod_w, mod_b, norm1_g, norm2_g, w_in, w_out, hg_lb_logits, hg_norm_g,
           ml_gate_b, ml_norm_g, hy_conv_w, hy_conv_b, hy_w1, hy_b1, hy_w2, hy_b2, hy_freq, hy_w3,
           hy_bias, gd_conv_w, gd_a_log, gd_dt_bias, gd_norm_g, router_w, router_b, exp_w1, exp_b1,
           exp_w2, exp_b2, final_g):
    n_batch, seq, d = x.shape
    ctx_len = ctx.shape[1]
    depth = mod_w.shape[0]
    rows = seq // GRID_W
    p = jax.nn.softmax(hg_lb_logits.astype(F32), axis=0)
    lower_bounds = jnp.cumsum(p, 0) - p[0]
    s_all = jnp.concatenate([jax.nn.silu(c), jax.nn.silu(c_ctx)[None]], 0)
    xs = jnp.concatenate([x, ctx], 1)
    lat = lambda t: t[:, :seq]
    cx = lambda t: t[:, seq:]
    colmajor = lambda t: jnp.concatenate([_to_colmajor(lat(t), rows), cx(t)], 1)
    rowmajor = lambda t: jnp.concatenate([_from_colmajor(lat(t), rows), cx(t)], 1)
    dft_lat, dft_ctx = _dft_matrices(seq), _dft_matrices(ctx_len)
    for l in range(depth):
        mod = (s_all @ mod_w[l] + mod_b[l]).reshape(n_batch + 1, 6, 1, d)
        sh1, sc1, g1, sh2, sc2, g2 = (mod[:, i] for i in range(6))
        z_hg, z_ml, z_hy, z_gd, z_gt = _in_proj(xs, norm1_g[l], sc1, sh1,
                                                _permute_w_in(w_in[l]), IN_SPLITS)
        hg_f, hg_b = _hgrn2_scan(z_hg, lower_bounds[l])
        ml_f, ml_b = _mlstm_scan(colmajor(z_ml), colmajor(z_gt), ml_gate_b[l], ctx_len)
        filt = (hy_w1[l], hy_b1[l], hy_w2[l], hy_b2[l], hy_freq[l], hy_w3[l])
        u_hy = _short_conv_tiles(z_hy, hy_conv_w[l], hy_conv_b[l], seq, False)
        c_lat = _hyena_segment(u_hy, 0, seq, dft_lat, filt, hy_bias[l])
        if l == depth - 1:
            c_ctx = jnp.zeros((n_batch, ctx_len, W_GROUP), F32)
        else:
            c_ctx = _hyena_segment(u_hy, seq, ctx_len, dft_ctx, filt, hy_bias[l])
        qkv = _short_conv_tiles(z_gd, gd_conv_w[l], jnp.zeros((3 * W_GROUP,), F32), seq, True)
        gd_f, gd_b = _gdn_scan(qkv, z_gt, gd_a_log[l], gd_dt_bias[l], ctx_len)
        mixers = [("silu", hg_f, hg_b, (z_hg, 4)),
                  ("sigmoid", rowmajor(ml_f), rowmajor(ml_b), (z_ml, 3)),
                  ("final", jnp.concatenate([c_lat, c_ctx], 1)),
                  ("silu", gd_f, gd_b, (z_gd, 3))]
        gains = jnp.stack([hg_norm_g[l], ml_norm_g[l], jnp.ones_like(ml_norm_g[l]), gd_norm_g[l]]).astype(F32)
        xs = _out_proj(xs, mixers, gains, g1, w_out[l].astype(BF16))
        xs = _moe_layer(xs, norm2_g[l], sc2, sh2, g2, router_w[l], router_b[l],
                        exp_w1[l], exp_b1[l], exp_w2[l], exp_b2[l])
    return _final_norm(xs, final_g, seq)
```

```python
import functools
import math

import jax
import jax.numpy as jnp
from jax import lax
from jax.experimental import pallas as pl
from jax.experimental.pallas import tpu as pltpu

F32 = jnp.float32
BF16 = jnp.bfloat16

D_MODEL = 1024
GRID_W = 64
N_MIXERS = 4
W_GROUP = D_MODEL // N_MIXERS
HG_HEADS = 4
HG_DK = W_GROUP // HG_HEADS
HG_CHUNK = 16
ML_HEADS = 4
ML_DH = W_GROUP // ML_HEADS
ML_CHUNK = 64
HY_ORDER = 2
HY_BANDS = 8
HY_FAST_DECAY = 0.3
HY_SLOW_DECAY = 1.5
HY_TARGET = 1e-2
GD_HEADS = 4
GD_DH = W_GROUP // GD_HEADS
GD_CHUNK = 64
N_EXPERTS = 32
TOP_K = 4
SWIGLU_LIMIT = 7.0
SWIGLU_ALPHA = 1.702
EPS = 1e-6
NEG = -1e30
HG_COLS = 5 * W_GROUP
ML_MAIN = 4 * W_GROUP
ML_GATES = 4 * ML_HEADS
HY_COLS = 3 * W_GROUP
GD_MAIN = 4 * W_GROUP
GD_GATES = 4 * GD_HEADS
GATE_PAD = 128

ROW_TILE = 256
MOE_TILE = 256
VMEM_LIMIT = 56 * 1024 * 1024


def _norm_mod(x, gain, sc, sh):
    y = x * lax.rsqrt(jnp.mean(x * x, -1, keepdims=True) + EPS)
    return y * gain * (1.0 + sc) + sh


def _in_proj_kernel(x_ref, g_ref, sc_ref, sh_ref, w_ref, *out_refs, splits):
    h = _norm_mod(x_ref[...], g_ref[...], sc_ref[...], sh_ref[...]).astype(BF16)
    off = 0
    for o_ref, n in zip(out_refs, splits):
        o_ref[...] = jnp.dot(h, w_ref[:, off:off + n], preferred_element_type=F32)
        off += n


def _mod_index_map(n_batch, n_tiles):
    return lambda b, j: (jnp.where(j == n_tiles - 1, n_batch, b), 0, 0)


def _in_proj(xs, gain, sc_tab, sh_tab, w, splits):
    n_batch, seq, d = xs.shape
    n = w.shape[1]
    grid = (n_batch, seq // ROW_TILE)
    mod_spec = pl.BlockSpec((None, 1, d), _mod_index_map(n_batch, seq // ROW_TILE))
    return pl.pallas_call(
        functools.partial(_in_proj_kernel, splits=splits),
        grid=grid,
        in_specs=[
            pl.BlockSpec((None, ROW_TILE, d), lambda b, j: (b, j, 0)),
            pl.BlockSpec((1, d), lambda b, j: (0, 0)),
            mod_spec, mod_spec,
            pl.BlockSpec((d, n), lambda b, j: (0, 0)),
        ],
        out_specs=[pl.BlockSpec((None, ROW_TILE, s), lambda b, j: (b, j, 0)) for s in splits],
        out_shape=[jax.ShapeDtypeStruct((n_batch, seq, s), F32) for s in splits],
        compiler_params=pltpu.CompilerParams(
            dimension_semantics=("parallel", "parallel"), vmem_limit_bytes=VMEM_LIMIT),
        name="in_proj",
    )(xs, gain.reshape(1, d), sc_tab, sh_tab, w)


def _moe_route_kernel(x_ref, g_ref, sc_ref, sh_ref, rw_ref, rb_ref, tri_ref, h_ref, idx_ref, gate_ref,
                      rank_ref, cnt_ref, carry):
    @pl.when(jnp.logical_and(pl.program_id(0) == 0, pl.program_id(1) == 0))
    def _():
        carry[...] = jnp.zeros_like(carry)

    h = _norm_mod(x_ref[...], g_ref[...], sc_ref[...], sh_ref[...])
    h_ref[...] = h
    work = _dot_hi(h, rw_ref[...]) + rb_ref[...]
    lane = lax.broadcasted_iota(jnp.int32, work.shape, 1)
    vals, hits = [], []
    for _ in range(TOP_K):
        m = jnp.max(work, axis=-1, keepdims=True)
        first = jnp.min(jnp.where(work == m, lane, GATE_PAD), axis=-1, keepdims=True)
        hit = lane == first
        vals.append(m)
        hits.append(hit)
        work = jnp.where(hit, -jnp.inf, work)
    exps = [jnp.exp(v - vals[0]) for v in vals]
    total = sum(exps)
    chosen = sum(hit.astype(F32) for hit in hits)
    before = jnp.dot(tri_ref[...], chosen.astype(BF16), preferred_element_type=F32) + carry[...]
    idx = jnp.zeros(work.shape, jnp.int32)
    gate = jnp.zeros(work.shape, F32)
    rank = jnp.zeros(work.shape, jnp.int32)
    for k in range(TOP_K):
        idx = jnp.where(lane == k, jnp.sum(jnp.where(hits[k], lane, 0), axis=-1, keepdims=True), idx)
        gate = jnp.where(lane == k, exps[k] / total, gate)
        r_k = jnp.sum(jnp.where(hits[k], before, 0.0), axis=-1, keepdims=True)
        rank = jnp.where(lane == k, r_k.astype(jnp.int32), rank)
    idx_ref[...] = idx
    gate_ref[...] = gate
    rank_ref[...] = rank
    carry[...] += jnp.sum(chosen, axis=0, keepdims=True)
    cnt_ref[...] = carry[...]


def _moe_route(xs, gain, sc_tab, sh_tab, router_w, router_b):
    n_batch, seq, d = xs.shape
    pad = GATE_PAD - N_EXPERTS
    rw = jnp.pad(router_w.astype(F32), ((0, 0), (0, pad)))
    rb = jnp.pad(router_b.astype(F32), (0, pad), constant_values=NEG).reshape(1, GATE_PAD)
    r = lax.broadcasted_iota(jnp.int32, (ROW_TILE, ROW_TILE), 0)
    c = lax.broadcasted_iota(jnp.int32, (ROW_TILE, ROW_TILE), 1)
    tri = (c < r).astype(BF16)
    mod_spec = pl.BlockSpec((None, 1, d), _mod_index_map(n_batch, seq // ROW_TILE))
    tok = lambda w: pl.BlockSpec((None, ROW_TILE, w), lambda b, j: (b, j, 0))
    return pl.pallas_call(
        _moe_route_kernel,
        grid=(n_batch, seq // ROW_TILE),
        in_specs=[
            tok(d),
            pl.BlockSpec((1, d), lambda b, j: (0, 0)),
            mod_spec, mod_spec,
            pl.BlockSpec((d, GATE_PAD), lambda b, j: (0, 0)),
            pl.BlockSpec((1, GATE_PAD), lambda b, j: (0, 0)),
            pl.BlockSpec((ROW_TILE, ROW_TILE), lambda b, j: (0, 0)),
        ],
        out_specs=[tok(d), tok(GATE_PAD), tok(GATE_PAD), tok(GATE_PAD),
                   pl.BlockSpec((1, GATE_PAD), lambda b, j: (0, 0))],
        out_shape=[jax.ShapeDtypeStruct((n_batch, seq, d), F32),
                   jax.ShapeDtypeStruct((n_batch, seq, GATE_PAD), jnp.int32),
                   jax.ShapeDtypeStruct((n_batch, seq, GATE_PAD), F32),
                   jax.ShapeDtypeStruct((n_batch, seq, GATE_PAD), jnp.int32),
                   jax.ShapeDtypeStruct((1, GATE_PAD), F32)],
        scratch_shapes=[pltpu.VMEM((1, GATE_PAD), F32)],
        compiler_params=pltpu.CompilerParams(
            dimension_semantics=("arbitrary", "arbitrary"), vmem_limit_bytes=VMEM_LIMIT),
        name="moe_route",
    )(xs, gain.reshape(1, d), sc_tab, sh_tab, rw, rb, tri)


def _out_proj_kernel(*refs, kinds):
    x_ref, refs = refs[0], refs[1:]
    gains_ref, g_ref, heads_ref, w_ref, o_ref = refs[-5:]
    heads = heads_ref[...] * (1.0 / HG_DK)
    acc = jnp.zeros(o_ref.shape, F32)
    pos = 0
    for i, kind in enumerate(kinds):
        if kind == "final":
            m = refs[pos][...]
            pos += 1
        else:
            o = refs[pos][...] + refs[pos + 1][...]
            gate = refs[pos + 2][...]
            pos += 3
            ms = _dot_sel(o * o, heads, 1, 3)
            o = o * lax.rsqrt(ms + EPS) * gains_ref[i:i + 1, :]
            m = o * (jax.nn.silu(gate) if kind == "silu" else jax.nn.sigmoid(gate))
        acc += jnp.dot(m.astype(BF16), w_ref[i * W_GROUP:(i + 1) * W_GROUP, :],
                       preferred_element_type=F32)
    o_ref[...] = x_ref[...] + g_ref[...] * acc


def _out_proj(xs, mixers, gains, gate_tab, w_out):
    n_batch, seq, d = xs.shape
    row = lambda w, cb=0: pl.BlockSpec((None, ROW_TILE, w), lambda b, j: (b, j, cb))
    args, specs, kinds = [], [], []
    for m in mixers:
        kinds.append(m[0])
        if m[0] == "final":
            args.append(m[1])
            specs.append(row(W_GROUP))
        else:
            src, cb = m[3]
            args += [m[1], m[2], src]
            specs += [row(W_GROUP), row(W_GROUP), row(W_GROUP, cb)]
    heads = _hg_consts()[3]
    return pl.pallas_call(
        functools.partial(_out_proj_kernel, kinds=tuple(kinds)),
        grid=(n_batch, seq // ROW_TILE),
        in_specs=[row(d)] + specs + [
            pl.BlockSpec((N_MIXERS, W_GROUP), lambda b, j: (0, 0)),
            pl.BlockSpec((None, 1, d), _mod_index_map(n_batch, seq // ROW_TILE)),
            pl.BlockSpec((W_GROUP, W_GROUP), lambda b, j: (0, 0)),
            pl.BlockSpec((d, d), lambda b, j: (0, 0)),
        ],
        out_specs=row(d),
        out_shape=jax.ShapeDtypeStruct((n_batch, seq, d), F32),
        compiler_params=pltpu.CompilerParams(
            dimension_semantics=("parallel", "parallel"), vmem_limit_bytes=VMEM_LIMIT),
        name="out_proj",
    )(xs, *args, gains, gate_tab, heads, w_out)


def _row_copy(src, src_row, dst, dst_row, sem):
    return pltpu.make_async_copy(src.at[pl.ds(src_row, 1), :], dst.at[pl.ds(dst_row, 1), :], sem)


def _dispatch_kernel(dest_ref, h_ref, slots_in_hbm, slots_hbm, sem):
    del slots_in_hbm

    def issue(r, carry):
        for k in range(TOP_K):
            _row_copy(h_ref, r, slots_hbm, dest_ref[0, r * TOP_K + k], sem).start(priority=k % 2)
        return carry

    lax.fori_loop(0, ROW_TILE, issue, 0, unroll=4)
    for k in range(TOP_K):
        pltpu.make_async_copy(h_ref, slots_hbm.at[pl.ds(0, ROW_TILE), :], sem).wait()


def _moe_dispatch(h, dest, n_slots):
    t_count, d = h.shape
    return pl.pallas_call(
        _dispatch_kernel,
        grid=(t_count // ROW_TILE,),
        in_specs=[pl.BlockSpec((None, 1, ROW_TILE * TOP_K), lambda i: (i, 0, 0), memory_space=pltpu.SMEM),
                  pl.BlockSpec((ROW_TILE, d), lambda i: (i, 0)),
                  pl.BlockSpec(memory_space=pl.ANY)],
        out_specs=pl.BlockSpec(memory_space=pl.ANY),
        out_shape=jax.ShapeDtypeStruct((n_slots, d), F32),
        scratch_shapes=[pltpu.SemaphoreType.DMA(())],
        input_output_aliases={2: 0},
        compiler_params=pltpu.CompilerParams(
            dimension_semantics=("arbitrary",), vmem_limit_bytes=VMEM_LIMIT),
        name="moe_dispatch",
    )(dest, h, jnp.zeros((n_slots, d), F32))


def _expert_kernel(bexp_ref, nused_ref, x_ref, w1_ref, b1_ref, w2_ref, b2_ref, o_ref, w1b, w2b):
    i = pl.program_id(0)
    changed = jnp.logical_or(i == 0, bexp_ref[i] != bexp_ref[jnp.maximum(i - 1, 0)])
    used = i < nused_ref[0]

    @pl.when(jnp.logical_and(changed, used))
    def _():
        w1b[...] = w1_ref[...].astype(BF16)
        w2b[...] = w2_ref[...].astype(BF16)

    @pl.when(used)
    def _():
        f = w2b.shape[0]
        u = jnp.dot(x_ref[...].astype(BF16), w1b[...], preferred_element_type=F32) + b1_ref[...]
        gate = jnp.minimum(u[:, :f], SWIGLU_LIMIT)
        lin = jnp.clip(u[:, f:], -SWIGLU_LIMIT, SWIGLU_LIMIT)
        y = (lin + 1.0) * gate * jax.nn.sigmoid(SWIGLU_ALPHA * gate)
        o_ref[...] = jnp.dot(y.astype(BF16), w2b[...], preferred_element_type=F32) + b2_ref[...]

    @pl.when(jnp.logical_not(used))
    def _():
        o_ref[...] = jnp.zeros_like(o_ref)


def _expert_ffn(block_exp, n_used, xg, layer, w1, b1, w2, b2):
    n_slots, d = xg.shape
    _, n_exp, _, f2 = w1.shape
    f = f2 // 2
    n_blocks = n_slots // MOE_TILE
    return pl.pallas_call(
        _expert_kernel,
        grid_spec=pltpu.PrefetchScalarGridSpec(
            num_scalar_prefetch=2,
            grid=(n_blocks,),
            in_specs=[
                pl.BlockSpec((MOE_TILE, d), lambda i, be, nu: (i, 0)),
                pl.BlockSpec((None, None, d, f2), lambda i, be, nu: (layer, be[i], 0, 0)),
                pl.BlockSpec((None, 1, f2), lambda i, be, nu: (be[i], 0, 0)),
                pl.BlockSpec((None, None, f, d), lambda i, be, nu: (layer, be[i], 0, 0)),
                pl.BlockSpec((None, 1, d), lambda i, be, nu: (be[i], 0, 0)),
            ],
            out_specs=pl.BlockSpec((MOE_TILE, d), lambda i, be, nu: (i, 0)),
            scratch_shapes=[pltpu.VMEM((d, f2), BF16), pltpu.VMEM((f, d), BF16)],
        ),
        out_shape=jax.ShapeDtypeStruct((n_slots, d), F32),
        compiler_params=pltpu.CompilerParams(
            dimension_semantics=("arbitrary",), vmem_limit_bytes=VMEM_LIMIT),
        name="expert_ffn",
    )(block_exp, n_used, xg, w1, b1.reshape(n_exp, 1, f2), w2, b2.reshape(n_exp, 1, d))


def _combine_kernel(dest_ref, y_hbm, x_ref, gate_ref, g_ref, o_ref, ybuf, sem):
    def issue(r, carry):
        for k in range(TOP_K):
            pltpu.make_async_copy(y_hbm.at[pl.ds(dest_ref[0, r * TOP_K + k], 1), :],
                                  ybuf.at[k, pl.ds(r, 1), :], sem).start(priority=k % 2)
        return carry

    lax.fori_loop(0, ROW_TILE, issue, 0, unroll=4)
    for k in range(TOP_K):
        pltpu.make_async_copy(y_hbm.at[pl.ds(0, ROW_TILE), :], ybuf.at[k], sem).wait()
    acc = jnp.zeros(o_ref.shape, F32)
    for k in range(TOP_K):
        acc += gate_ref[:, k:k + 1] * ybuf[k]
    o_ref[...] = x_ref[...] + g_ref[...] * acc


def _moe_combine(xs, y_slots, dest, gates, gate_tab):
    n_batch, seq, d = xs.shape
    n_tiles = seq // ROW_TILE
    tok = lambda w: pl.BlockSpec((None, ROW_TILE, w), lambda b, j: (b, j, 0))
    return pl.pallas_call(
        _combine_kernel,
        grid=(n_batch, n_tiles),
        in_specs=[pl.BlockSpec((None, 1, ROW_TILE * TOP_K), lambda b, j: (b * n_tiles + j, 0, 0),
                               memory_space=pltpu.SMEM),
                  pl.BlockSpec(memory_space=pl.ANY),
                  tok(d), tok(GATE_PAD),
                  pl.BlockSpec((None, 1, d), _mod_index_map(n_batch, n_tiles))],
        out_specs=tok(d),
        out_shape=jax.ShapeDtypeStruct((n_batch, seq, d), F32),
        scratch_shapes=[pltpu.VMEM((TOP_K, ROW_TILE, d), F32), pltpu.SemaphoreType.DMA(())],
        compiler_params=pltpu.CompilerParams(
            dimension_semantics=("arbitrary", "arbitrary"), vmem_limit_bytes=VMEM_LIMIT),
        name="moe_combine",
    )(dest, y_slots, xs, gates, gate_tab)


def _moe_layer(xs, gain, sc_tab, sh_tab, gate_tab, router_w, router_b, layer, w1, b1, w2, b2):
    n_batch, seq, d = xs.shape
    t_count = n_batch * seq
    h, idx, gates, rank, counts = _moe_route(xs, gain, sc_tab, sh_tab, router_w, router_b)
    counts = counts[0, :N_EXPERTS].astype(jnp.int32)
    padded = (counts + MOE_TILE - 1) // MOE_TILE * MOE_TILE
    pend = jnp.cumsum(padded)
    pstart = pend - padded
    top_i = idx.reshape(t_count, GATE_PAD)[:, :TOP_K]
    hit = top_i[:, :, None] == jnp.arange(N_EXPERTS, dtype=jnp.int32)
    dest = rank.reshape(t_count, GATE_PAD)[:, :TOP_K] + jnp.sum(jnp.where(hit, pstart, 0), -1)
    dest = dest.astype(jnp.int32).reshape(t_count // ROW_TILE, 1, ROW_TILE * TOP_K)
    n_blocks = -(-(t_count * TOP_K + N_EXPERTS * (MOE_TILE - 1)) // MOE_TILE)
    tile_start = jnp.arange(n_blocks, dtype=jnp.int32)[:, None] * MOE_TILE
    block_exp = jnp.minimum(jnp.sum((pend[None, :] <= tile_start).astype(jnp.int32), -1), N_EXPERTS - 1)
    n_used = (pend[-1:] // MOE_TILE).astype(jnp.int32)
    slots = _moe_dispatch(h.reshape(t_count, d), dest, n_blocks * MOE_TILE)
    y_slots = _expert_ffn(block_exp, n_used, slots, layer, w1, b1, w2, b2)
    return _moe_combine(xs, y_slots, dest, gates, gate_tab)


def _final_norm_kernel(x_ref, g_ref, o_ref):
    x = x_ref[...]
    o_ref[...] = x * lax.rsqrt(jnp.mean(x * x, -1, keepdims=True) + EPS) * g_ref[...]


def _final_norm(xs, gain, n_lat):
    n_batch, _, d = xs.shape
    return pl.pallas_call(
        _final_norm_kernel,
        grid=(n_batch, n_lat // ROW_TILE),
        in_specs=[pl.BlockSpec((None, ROW_TILE, d), lambda b, j: (b, j, 0)),
                  pl.BlockSpec((1, d), lambda b, j: (0, 0))],
        out_specs=pl.BlockSpec((None, ROW_TILE, d), lambda b, j: (b, j, 0)),
        out_shape=jax.ShapeDtypeStruct((n_batch, n_lat, d), F32),
        compiler_params=pltpu.CompilerParams(dimension_semantics=("parallel", "parallel")),
        name="final_norm",
    )(xs, gain.reshape(1, d))


HG_SUB = 16


def _hg_consts():
    r = lax.broadcasted_iota(jnp.int32, (ROW_TILE, ROW_TILE), 0)
    c = lax.broadcasted_iota(jnp.int32, (ROW_TILE, ROW_TILE), 1)
    same_chunk = (r // HG_SUB) == (c // HG_SUB)
    tri_f = jnp.where(same_chunk & (c <= r), 1.0, 0.0).astype(F32)
    tri_b = jnp.where(same_chunk & (c >= r), 1.0, 0.0).astype(F32)
    blk = jnp.where(same_chunk, 1.0, 0.0).astype(F32)
    heads = jnp.where((r // HG_DK) == (c // HG_DK), 1.0, 0.0).astype(F32)
    return tri_f, tri_b, blk, heads


def _hg_prepare(z_ref, zf_col, lb, tri, blk, qs_s, key_s, cm_s, qh_s, kh_s, dec_s):
    q = jax.nn.silu(z_ref[:, 0:W_GROUP])
    zf = z_ref[:, zf_col:zf_col + W_GROUP]
    f = lb + (1.0 - lb) * jax.nn.sigmoid(zf)
    key = (1.0 - lb) * jax.nn.sigmoid(-zf)
    logf = jnp.log(f)
    cum = _dot_sel(tri, logf, 0, 3)
    tot = _dot_sel(blk, logf, 0, 3)
    qs_s[...] = q
    key_s[...] = key
    cm_s[...] = cum
    qh_s[...] = (q * jnp.exp(cum)).astype(BF16)
    kh_s[...] = (key * jnp.exp(tot - cum)).astype(BF16)
    dec_s[...] = jnp.exp(tot)


def _hg_chunk(c, reverse, z_ref, o_ref, st_ref, heads_bf, heads_f, qs_s, key_s, cm_s, qh_s, kh_s, dec_s):
    sl = pl.ds(pl.multiple_of(c * HG_SUB, HG_SUB), HG_SUB)
    q, k, cm = qs_s[sl, :], key_s[sl, :], cm_s[sl, :]
    v = z_ref[sl, W_GROUP:2 * W_GROUP]
    st = st_ref[...]
    o = lax.dot_general(qh_s[sl, :], st.astype(BF16), (((1,), (1,)), ((), ())),
                        preferred_element_type=F32)
    t_idx = lax.broadcasted_iota(jnp.int32, (HG_SUB, W_GROUP), 0)
    parts = []
    for s in range(HG_SUB):
        live = (t_idx <= s) if reverse else (t_idx >= s)
        e = jnp.exp(jnp.where(live, cm - cm[s:s + 1, :], NEG))
        parts.append((q * k[s:s + 1, :] * e).astype(BF16))
    r = jnp.dot(jnp.concatenate(parts, 0), heads_bf, preferred_element_type=F32)
    for s in range(HG_SUB):
        o += r[s * HG_SUB:(s + 1) * HG_SUB, :] * v[s:s + 1, :]
    o_ref[sl, :] = o
    ut = lax.dot_general(v.astype(BF16), kh_s[sl, :], (((0,), (0,)), ((), ())),
                         preferred_element_type=F32)
    st_ref[...] = st * dec_s[sl, :][0:1, :] + ut * heads_f


def _hgrn2_kernel(zf_ref, zb_ref, lb_ref, trif_ref, trib_ref, blk_ref, heads_ref, of_ref, ob_ref,
                  st_s, qs_s, key_s, cm_s, qh_s, kh_s, dec_s):
    @pl.when(pl.program_id(1) == 0)
    def _():
        st_s[...] = jnp.zeros_like(st_s)

    lb = lb_ref[...]
    blk = blk_ref[...]
    heads_f = heads_ref[...]
    heads_bf = heads_f.astype(BF16)
    _hg_prepare(zf_ref, 2 * W_GROUP, lb, trif_ref[...], blk, qs_s.at[0], key_s.at[0], cm_s.at[0],
                qh_s.at[0], kh_s.at[0], dec_s.at[0])
    _hg_prepare(zb_ref, 3 * W_GROUP, lb, trib_ref[...], blk, qs_s.at[1], key_s.at[1], cm_s.at[1],
                qh_s.at[1], kh_s.at[1], dec_s.at[1])
    n_chunks = ROW_TILE // HG_SUB

    def body(c, carry):
        _hg_chunk(c, False, zf_ref, of_ref, st_s.at[0], heads_bf, heads_f, qs_s.at[0], key_s.at[0],
                  cm_s.at[0], qh_s.at[0], kh_s.at[0], dec_s.at[0])
        _hg_chunk(n_chunks - 1 - c, True, zb_ref, ob_ref, st_s.at[1], heads_bf, heads_f, qs_s.at[1],
                  key_s.at[1], cm_s.at[1], qh_s.at[1], kh_s.at[1], dec_s.at[1])
        return carry

    lax.fori_loop(0, n_chunks, body, 0)


def _scan_tiles(n_lat_tiles):
    fwd = lambda j: jnp.where(j == 0, n_lat_tiles, j - 1)
    bwd = lambda j: jnp.where(j == 0, n_lat_tiles, n_lat_tiles - j)
    return fwd, bwd


def _hgrn2_scan(z_hg, lb):
    n_batch, seq, _ = z_hg.shape
    n_tiles = seq // ROW_TILE
    fwd, bwd = _scan_tiles(n_tiles - 1)
    const = pl.BlockSpec((ROW_TILE, ROW_TILE), lambda b, j: (0, 0))
    scr = lambda dt: pltpu.VMEM((2, ROW_TILE, W_GROUP), dt)
    return pl.pallas_call(
        _hgrn2_kernel,
        grid=(n_batch, n_tiles),
        in_specs=[pl.BlockSpec((None, ROW_TILE, HG_COLS), lambda b, j: (b, fwd(j), 0)),
                  pl.BlockSpec((None, ROW_TILE, HG_COLS), lambda b, j: (b, bwd(j), 0)),
                  pl.BlockSpec((1, W_GROUP), lambda b, j: (0, 0)),
                  const, const, const, const],
        out_specs=[pl.BlockSpec((None, ROW_TILE, W_GROUP), lambda b, j: (b, fwd(j), 0)),
                   pl.BlockSpec((None, ROW_TILE, W_GROUP), lambda b, j: (b, bwd(j), 0))],
        out_shape=[jax.ShapeDtypeStruct((n_batch, seq, W_GROUP), F32)] * 2,
        scratch_shapes=[pltpu.VMEM((2, W_GROUP, W_GROUP), F32), scr(F32), scr(F32), scr(F32),
                        scr(BF16), scr(BF16), scr(F32)],
        compiler_params=pltpu.CompilerParams(
            dimension_semantics=("parallel", "arbitrary"), vmem_limit_bytes=VMEM_LIMIT),
        name="hgrn2_scan",
    )(z_hg, z_hg, lb.reshape(1, W_GROUP), *_hg_consts())


def _dot_hi(a, b):
    return jnp.dot(a, b, preferred_element_type=F32, precision=lax.Precision.HIGHEST)


def _bf16_terms(x, terms):
    parts = []
    for _ in range(terms):
        p = x.astype(BF16)
        parts.append(p)
        x = x - p.astype(F32)
    return parts


def _dot_sel(a, b, select, terms, nt=False):
    ops = [a, b]
    fixed = ops[select].astype(BF16)
    dims = (((1,), (1,)), ((), ())) if nt else (((1,), (0,)), ((), ()))
    out = None
    for p in _bf16_terms(ops[1 - select], terms):
        pair = (fixed, p) if select == 0 else (p, fixed)
        d = lax.dot_general(pair[0], pair[1], dims, preferred_element_type=F32)
        out = d if out is None else out + d
    return out


def _dot_nt(a, b, precision=None):
    return lax.dot_general(a, b, (((1,), (1,)), ((), ())), preferred_element_type=F32,
                           precision=precision)


def _dot_tn(a, b):
    return lax.dot_general(a, b, (((0,), (0,)), ((), ())), preferred_element_type=F32)


def _chunk_consts(chunk, n_heads, gate_cols):
    r = lax.broadcasted_iota(jnp.int32, (chunk, chunk), 0)
    c = lax.broadcasted_iota(jnp.int32, (chunk, chunk), 1)
    low = (c <= r).astype(F32)
    upp = (c >= r).astype(F32)
    dh = W_GROUP // n_heads
    hr = lax.broadcasted_iota(jnp.int32, (W_GROUP, W_GROUP), 0) // dh
    hc = lax.broadcasted_iota(jnp.int32, (W_GROUP, W_GROUP), 1) // dh
    heads = (hr == hc).astype(F32)
    gc = lax.broadcasted_iota(jnp.int32, (GATE_PAD, W_GROUP), 0)
    gh = lax.broadcasted_iota(jnp.int32, (GATE_PAD, W_GROUP), 1) // dh
    expand = jnp.stack([(gc == g0 + gh).astype(F32) for g0 in gate_cols])
    sr = lax.broadcasted_iota(jnp.int32, (8, GATE_PAD), 0)
    sc = lax.broadcasted_iota(jnp.int32, (8, GATE_PAD), 1)
    select = jnp.stack([((sc == g0 + sr) & (sr < n_heads)).astype(F32) for g0 in gate_cols])
    return low, upp, heads, expand, select


class _MlDir:
    def __init__(self, reverse, z_ref, g_ref, gb, c_ref, n_ref, m_ref, low, upp, heads,
                 ex_i, ex_f, sel_i, sel_f):
        self.q = z_ref[:, 0:W_GROUP]
        self.k = z_ref[:, W_GROUP:2 * W_GROUP] * (ML_DH ** -0.5)
        self.v = z_ref[:, 2 * W_GROUP:3 * W_GROUP]
        gates = g_ref[...] + gb
        col_sum, row_sum = (upp, low) if reverse else (low, upp)
        self.li = _dot_sel(gates, ex_i, 1, 3)
        lf = jax.nn.log_sigmoid(_dot_sel(gates, ex_f, 1, 3))
        self.li_rows = _dot_sel(sel_i, gates, 0, 3, nt=True)
        lf_rows = jax.nn.log_sigmoid(_dot_sel(sel_f, gates, 0, 3, nt=True))
        self.fcum = _dot_sel(col_sum, lf, 0, 3)
        self.fcum_rows = _dot_sel(lf_rows, row_sum, 1, 3)
        last = 0 if reverse else ML_CHUNK - 1
        self.ftot = self.fcum[last:last + 1, :]
        g = self.ftot - self.fcum + self.li
        self.g_max = jnp.max(g, axis=0, keepdims=True)
        self.kw = self.k * jnp.exp(g - self.g_max)
        self.c0, self.n0, self.m0 = c_ref[...], n_ref[...], m_ref[...]
        self.k_bf = self.k.astype(BF16)
        self.qc = jnp.dot(self.q.astype(BF16), self.c0.astype(BF16), preferred_element_type=F32)
        self.qn = jnp.dot((self.q * self.n0).astype(BF16), heads.astype(BF16), preferred_element_type=F32)
        self.a_init = self.fcum + self.m0
        rr = lax.broadcasted_iota(jnp.int32, (ML_CHUNK, ML_CHUNK), 0)
        cc = lax.broadcasted_iota(jnp.int32, (ML_CHUNK, ML_CHUNK), 1)
        self.live = (cc >= rr) if reverse else (cc <= rr)


def _mlstm_kernel(zf_ref, zb_ref, gf_ref, gb_ref, bias_ref, low_ref, upp_ref, heads_ref, ex_ref, sel_ref,
                  of_ref, ob_ref, c_s, n_s, m_s):
    @pl.when(pl.program_id(1) == 0)
    def _():
        c_s[...] = jnp.zeros_like(c_s)
        n_s[...] = jnp.zeros_like(n_s)
        m_s[...] = jnp.zeros_like(m_s)

    low, upp, heads, bias = low_ref[...], upp_ref[...], heads_ref[...], bias_ref[...]
    dirs = [_MlDir(False, zf_ref, gf_ref, bias, c_s.at[0], n_s.at[0], m_s.at[0], low, upp, heads,
                   ex_ref[0], ex_ref[1], sel_ref[0], sel_ref[1]),
            _MlDir(True, zb_ref, gb_ref, bias, c_s.at[1], n_s.at[1], m_s.at[1], low, upp, heads,
                   ex_ref[2], ex_ref[3], sel_ref[2], sel_ref[3])]
    lane_head = lax.broadcasted_iota(jnp.int32, (ML_CHUNK, W_GROUP), 1) // ML_DH
    hs = lambda h: slice(h * ML_DH, (h + 1) * ML_DH)
    probs = [(d, h) for d in dirs for h in range(ML_HEADS)]
    qks = [_dot_nt(jnp.where(lane_head == h, d.q, 0.0).astype(BF16), d.k_bf) for d, h in probs]
    a = [jnp.where(d.live, d.fcum[:, hs(h)] - d.fcum_rows[h:h + 1, :] + d.li_rows[h:h + 1, :], NEG)
         for d, h in probs]
    m_t = [jnp.maximum(jnp.max(a_h, axis=1, keepdims=True), d.a_init[:, hs(h)])
           for (d, h), a_h in zip(probs, a)]
    p = [jnp.exp(a_h - m) * qk for a_h, m, qk in zip(a, m_t, qks)]
    pv = [jnp.dot(p_h.astype(BF16), d.v[:, hs(h)].astype(BF16), preferred_element_type=F32)
          for (d, h), p_h in zip(probs, p)]
    outs = []
    for (d, h), p_h, pv_h, m in zip(probs, p, pv, m_t):
        e0 = jnp.exp(d.a_init[:, hs(h)] - m)
        num = pv_h + e0 * d.qc[:, hs(h)]
        den = jnp.sum(p_h, axis=1, keepdims=True) + e0 * d.qn[:, hs(h)]
        outs.append(num / jnp.maximum(jnp.abs(den), jnp.exp(-m)))
    u_c = [_dot_tn(d.kw.astype(BF16), d.v.astype(BF16)) * heads for d in dirs]
    for i, (d, o_ref) in enumerate(zip(dirs, (of_ref, ob_ref))):
        o_ref[...] = jnp.concatenate(outs[i * ML_HEADS:(i + 1) * ML_HEADS], axis=1)
        m_new = jnp.maximum(d.ftot + d.m0, d.g_max)
        a_old = jnp.exp(d.ftot + d.m0 - m_new)
        a_new = jnp.exp(d.g_max - m_new)
        c_s[i] = a_old * d.c0 + a_new * u_c[i]
        n_s[i] = a_old * d.n0 + a_new * jnp.sum(d.kw, axis=0, keepdims=True)
        m_s[i] = m_new


def _scan_chunks(n_lat, n_ctx):
    fwd = lambda j: jnp.where(j < n_ctx, n_lat + j, j - n_ctx)
    bwd = lambda j: jnp.where(j < n_ctx, n_lat + n_ctx - 1 - j, n_lat + n_ctx - 1 - j)
    return fwd, bwd


def _mlstm_scan(z_ml, z_gt, gate_b, ctx_len):
    n_batch, seq, _ = z_ml.shape
    n_chunks = seq // ML_CHUNK
    n_ctx = ctx_len // ML_CHUNK
    fwd, bwd = _scan_chunks(n_chunks - n_ctx, n_ctx)
    low, upp, heads, expand, select = _chunk_consts(ML_CHUNK, ML_HEADS, (0, 4, 8, 12))
    bias = jnp.pad(gate_b.astype(F32).reshape(1, ML_GATES), ((0, 0), (0, GATE_PAD - ML_GATES)))
    full = lambda a: pl.BlockSpec(a.shape, lambda b, j: (0,) * a.ndim)
    return pl.pallas_call(
        _mlstm_kernel,
        grid=(n_batch, n_chunks),
        in_specs=[pl.BlockSpec((None, ML_CHUNK, ML_MAIN), lambda b, j: (b, fwd(j), 0)),
                  pl.BlockSpec((None, ML_CHUNK, ML_MAIN), lambda b, j: (b, bwd(j), 0)),
                  pl.BlockSpec((None, ML_CHUNK, GATE_PAD), lambda b, j: (b, fwd(j), 0)),
                  pl.BlockSpec((None, ML_CHUNK, GATE_PAD), lambda b, j: (b, bwd(j), 0)),
                  full(bias), full(low), full(upp), full(heads), full(expand), full(select)],
        out_specs=[pl.BlockSpec((None, ML_CHUNK, W_GROUP), lambda b, j: (b, fwd(j), 0)),
                   pl.BlockSpec((None, ML_CHUNK, W_GROUP), lambda b, j: (b, bwd(j), 0))],
        out_shape=[jax.ShapeDtypeStruct((n_batch, seq, W_GROUP), F32)] * 2,
        scratch_shapes=[pltpu.VMEM((2, W_GROUP, W_GROUP), F32), pltpu.VMEM((2, 1, W_GROUP), F32),
                        pltpu.VMEM((2, 1, W_GROUP), F32)],
        compiler_params=pltpu.CompilerParams(
            dimension_semantics=("parallel", "arbitrary"), vmem_limit_bytes=VMEM_LIMIT),
        name="mlstm_scan",
    )(z_ml, z_ml, z_gt, z_gt, bias, low, upp, heads, expand, select)


HALO = 8


def _short_conv_kernel(x_ref, prev_ref, next_ref, w_ref, b_ref, heads_ref, o_ref, *, n_lat_tiles, gdn):
    j = pl.program_id(1)
    n_tiles = pl.num_programs(1)
    x = x_ref[...]
    has_prev = jnp.logical_and(j != 0, j != n_lat_tiles)
    has_next = jnp.logical_and(j != n_lat_tiles - 1, j != n_tiles - 1)
    prev_row = jnp.where(has_prev, prev_ref[HALO - 1:HALO, :], 0.0)
    next_row = jnp.where(has_next, next_ref[0:1, :], 0.0)
    row = lax.broadcasted_iota(jnp.int32, x.shape, 0)
    x_prev = jnp.where(row == 0, prev_row, pltpu.roll(x, 1, 0))
    x_next = jnp.where(row == ROW_TILE - 1, next_row, pltpu.roll(x, ROW_TILE - 1, 0))
    y = x_prev * w_ref[0:1, :] + x * w_ref[1:2, :] + x_next * w_ref[2:3, :] + b_ref[...]
    if not gdn:
        o_ref[...] = y
        return
    y = jax.nn.silu(y)
    heads = heads_ref[...]
    for i in range(2):
        t = y[:, i * W_GROUP:(i + 1) * W_GROUP]
        o_ref[:, i * W_GROUP:(i + 1) * W_GROUP] = t * lax.rsqrt(_dot_sel(t * t, heads, 1, 3) + EPS)
    o_ref[:, 2 * W_GROUP:] = y[:, 2 * W_GROUP:]


def _head_mask():
    r = lax.broadcasted_iota(jnp.int32, (W_GROUP, W_GROUP), 0) // HG_DK
    c = lax.broadcasted_iota(jnp.int32, (W_GROUP, W_GROUP), 1) // HG_DK
    return (r == c).astype(F32)


def _short_conv_tiles(z, conv_w, conv_b, n_lat, gdn):
    n_batch, seq, _ = z.shape
    c = conv_w.shape[1]
    n_tiles = seq // ROW_TILE
    per = ROW_TILE // HALO
    n_halo = seq // HALO
    return pl.pallas_call(
        functools.partial(_short_conv_kernel, n_lat_tiles=n_lat // ROW_TILE, gdn=gdn),
        grid=(n_batch, n_tiles),
        in_specs=[pl.BlockSpec((None, ROW_TILE, c), lambda b, j: (b, j, 0)),
                  pl.BlockSpec((None, HALO, c), lambda b, j: (b, jnp.maximum(j * per - 1, 0), 0)),
                  pl.BlockSpec((None, HALO, c), lambda b, j: (b, jnp.minimum((j + 1) * per, n_halo - 1), 0)),
                  pl.BlockSpec((3, c), lambda b, j: (0, 0)),
                  pl.BlockSpec((1, c), lambda b, j: (0, 0)),
                  pl.BlockSpec((W_GROUP, W_GROUP), lambda b, j: (0, 0))],
        out_specs=pl.BlockSpec((None, ROW_TILE, c), lambda b, j: (b, j, 0)),
        out_shape=jax.ShapeDtypeStruct((n_batch, seq, c), F32),
        compiler_params=pltpu.CompilerParams(
            dimension_semantics=("parallel", "parallel"), vmem_limit_bytes=VMEM_LIMIT),
        name="gdn_conv" if gdn else "hyena_conv",
    )(z, z, z, conv_w.astype(F32), conv_b.astype(F32).reshape(1, c), _head_mask())


def _gd_consts():
    r = lax.broadcasted_iota(jnp.int32, (GD_CHUNK, GD_CHUNK), 0)
    c = lax.broadcasted_iota(jnp.int32, (GD_CHUNK, GD_CHUNK), 1)
    same = lambda n: (r // n) == (c // n)
    eye = (r == c).astype(F32)
    masks = [same(8).astype(F32)] + [(same(2 * b) & ~same(b)).astype(F32) for b in (8, 16, 32)]
    return eye, jnp.stack(masks)


def _unit_tri_inverses(ns, eye, masks):
    mm = lambda a, b: jnp.dot(a.astype(BF16), b.astype(BF16), preferred_element_type=F32)
    n8 = [n * masks[0] for n in ns]
    n2 = [mm(a, a) for a in n8]
    n4 = [mm(a, a) for a in n2]
    ts = [mm(eye - a, eye + b) for a, b in zip(n8, n2)]
    ts = [mm(t, eye + b) for t, b in zip(ts, n4)]
    for i in range(1, 4):
        lts = [mm(n * masks[i], t) for n, t in zip(ns, ts)]
        ts = [t - mm(t, lt) for t, lt in zip(ts, lts)]
    return ts


class _GdDir:
    def __init__(self, reverse, x_ref, g_ref, s_ref, prm_lane, prm_row, low, upp, ex_a, ex_b, sel_a):
        self.q = x_ref[:, 0:W_GROUP] * (GD_DH ** -0.5)
        self.k = x_ref[:, W_GROUP:2 * W_GROUP]
        self.v = x_ref[:, 2 * W_GROUP:3 * W_GROUP]
        gates = g_ref[...]
        col_sum, row_sum = (upp, low) if reverse else (low, upp)
        log_a = prm_lane[0:1, :] * jax.nn.softplus(_dot_sel(gates, ex_a, 1, 3) + prm_lane[1:2, :])
        self.beta = jax.nn.sigmoid(_dot_sel(gates, ex_b, 1, 3))
        log_a_rows = prm_row[0] * jax.nn.softplus(_dot_sel(sel_a, gates, 0, 3, nt=True) + prm_row[1])
        self.g = _dot_sel(col_sum, log_a, 0, 3)
        self.g_rows = _dot_sel(log_a_rows, row_sum, 1, 3)
        last = 0 if reverse else GD_CHUNK - 1
        self.g_last = self.g[last:last + 1, :]
        self.eg = jnp.exp(self.g)
        self.k_bf = self.k.astype(BF16)
        self.kb = self.k * self.beta
        self.s0 = s_ref[...]
        rr = lax.broadcasted_iota(jnp.int32, (GD_CHUNK, GD_CHUNK), 0)
        cc = lax.broadcasted_iota(jnp.int32, (GD_CHUNK, GD_CHUNK), 1)
        self.live = (cc >= rr) if reverse else (cc <= rr)
        self.strict = (cc > rr) if reverse else (cc < rr)


def _gdn_kernel(xf_ref, xb_ref, gf_ref, gb_ref, prm_lane_ref, prm_row_ref, low_ref, upp_ref, heads_ref,
                ex_ref, sel_ref, eye_ref, masks_ref, of_ref, ob_ref, s_s):
    @pl.when(pl.program_id(1) == 0)
    def _():
        s_s[...] = jnp.zeros_like(s_s)

    low, upp, heads, eye, masks = low_ref[...], upp_ref[...], heads_ref[...], eye_ref[...], masks_ref[...]
    dirs = [_GdDir(False, xf_ref, gf_ref, s_s.at[0], prm_lane_ref[0], prm_row_ref[0], low, upp,
                   ex_ref[0], ex_ref[1], sel_ref[0]),
            _GdDir(True, xb_ref, gb_ref, s_s.at[1], prm_lane_ref[1], prm_row_ref[1], low, upp,
                   ex_ref[2], ex_ref[3], sel_ref[2])]
    lane_head = lax.broadcasted_iota(jnp.int32, (GD_CHUNK, W_GROUP), 1) // GD_DH
    probs = [(d, h) for d in dirs for h in range(GD_HEADS)]
    only = lambda h, t: jnp.where(lane_head == h, t, 0.0).astype(BF16)
    hs = lambda h: slice(h * GD_DH, (h + 1) * GD_DH)
    decay = [jnp.exp(jnp.where(d.live, d.g[:, hs(h)] - d.g_rows[h:h + 1, :], NEG)) for d, h in probs]
    kks = [_dot_nt(only(h, d.kb), d.k_bf) for d, h in probs]
    ns = [jnp.where(d.strict, kk * dc, 0.0) for (d, h), kk, dc in zip(probs, kks, decay)]
    t_inv = [t.astype(BF16) for t in _unit_tri_inverses(ns, eye, masks)]
    us = [jnp.dot(t, only(h, d.v * d.beta), preferred_element_type=F32) for (d, h), t in zip(probs, t_inv)]
    ws = [jnp.dot(t, only(h, d.kb * d.eg), preferred_element_type=F32) for (d, h), t in zip(probs, t_inv)]
    qks = [(_dot_nt(only(h, d.q), d.k_bf) * dc).astype(BF16) for (d, h), dc in zip(probs, decay)]
    v_new, o = [], []
    for i, d in enumerate(dirs):
        sl = slice(i * GD_HEADS, (i + 1) * GD_HEADS)
        s_bf = d.s0.astype(BF16)
        v_new.append(sum(us[sl]) - jnp.dot(sum(ws[sl]).astype(BF16), s_bf, preferred_element_type=F32))
        o.append(jnp.dot((d.q * d.eg).astype(BF16), s_bf, preferred_element_type=F32))
    intra = [jnp.dot(qk, only(h, v_new[i // GD_HEADS]), preferred_element_type=F32)
             for i, ((d, h), qk) in enumerate(zip(probs, qks))]
    for i, (d, o_ref) in enumerate(zip(dirs, (of_ref, ob_ref))):
        o_ref[...] = o[i] + sum(intra[i * GD_HEADS:(i + 1) * GD_HEADS])
        k_dec = d.k * jnp.exp(d.g_last - d.g)
        s_s[i] = jnp.exp(d.g_last) * d.s0 + _dot_tn(k_dec.astype(BF16), v_new[i].astype(BF16)) * heads


def _gdn_scan(qkv, z_gt, a_log, dt_bias, ctx_len):
    n_batch, seq, _ = qkv.shape
    n_chunks = seq // GD_CHUNK
    n_ctx = ctx_len // GD_CHUNK
    fwd, bwd = _scan_chunks(n_chunks - n_ctx, n_ctx)
    g0 = ML_GATES
    low, upp, heads, expand, select = _chunk_consts(GD_CHUNK, GD_HEADS, (g0, g0 + 4, g0 + 8, g0 + 12))
    eye, masks = _gd_consts()
    prm = jnp.stack([-jnp.exp(a_log.astype(F32)), dt_bias.astype(F32)], 1)
    prm_lane = jnp.repeat(prm, GD_DH, axis=2)
    prm_row = jnp.broadcast_to(jnp.pad(prm, ((0, 0), (0, 0), (0, 8 - GD_HEADS)))[..., None],
                               (2, 2, 8, GD_CHUNK))
    full = lambda a: pl.BlockSpec(a.shape, lambda b, j: (0,) * a.ndim)
    return pl.pallas_call(
        _gdn_kernel,
        grid=(n_batch, n_chunks),
        in_specs=[pl.BlockSpec((None, GD_CHUNK, 3 * W_GROUP), lambda b, j: (b, fwd(j), 0)),
                  pl.BlockSpec((None, GD_CHUNK, 3 * W_GROUP), lambda b, j: (b, bwd(j), 0)),
                  pl.BlockSpec((None, GD_CHUNK, GATE_PAD), lambda b, j: (b, fwd(j), 0)),
                  pl.BlockSpec((None, GD_CHUNK, GATE_PAD), lambda b, j: (b, bwd(j), 0)),
                  full(prm_lane), full(prm_row), full(low), full(upp), full(heads), full(expand),
                  full(select), full(eye), full(masks)],
        out_specs=[pl.BlockSpec((None, GD_CHUNK, W_GROUP), lambda b, j: (b, fwd(j), 0)),
                   pl.BlockSpec((None, GD_CHUNK, W_GROUP), lambda b, j: (b, bwd(j), 0))],
        out_shape=[jax.ShapeDtypeStruct((n_batch, seq, W_GROUP), F32)] * 2,
        scratch_shapes=[pltpu.VMEM((2, W_GROUP, W_GROUP), F32)],
        compiler_params=pltpu.CompilerParams(
            dimension_semantics=("parallel", "arbitrary"), vmem_limit_bytes=VMEM_LIMIT),
        name="gdn_scan",
    )(qkv, qkv, z_gt, z_gt, prm_lane, prm_row, low, upp, heads, expand, select, eye, masks)


HY_FILT_TILE = 512
DFT_TILE = 512


DFT_ROWS = 64


def _dft_table_kernel(ac_ref, as_ref, bc_ref, bs_ref, re_ref, im_ref, imt_ref):
    a = pl.program_id(0)
    ac, sa = ac_ref[pl.ds(a, 1), :], as_ref[pl.ds(a, 1), :]
    bc, sb = bc_ref[...], bs_ref[...]
    cos = ac * bc - sa * sb
    nsin = -(sa * bc + ac * sb)
    i = lax.broadcasted_iota(jnp.int32, cos.shape, 0) + a * DFT_ROWS
    j = lax.broadcasted_iota(jnp.int32, cos.shape, 1)
    sign = lambda n: jnp.where(n % 2 == 0, 1.0, -1.0)
    re_ref[...] = cos.astype(BF16)
    im_ref[...] = jnp.where(i == 0, sign(j), nsin).astype(BF16)
    imt_ref[...] = jnp.where(j == 0, sign(i), nsin).astype(BF16)


def _dft_matrices(length):
    n = 2 * length
    j = jnp.arange(length, dtype=jnp.int32)[None, :]
    coarse = (jnp.arange(length // DFT_ROWS, dtype=jnp.int32)[:, None] * DFT_ROWS * j) % n
    fine = (jnp.arange(DFT_ROWS, dtype=jnp.int32)[:, None] * j) % n
    w = 2.0 * math.pi / n
    tabs = [jnp.cos(coarse.astype(F32) * w), jnp.sin(coarse.astype(F32) * w),
            jnp.cos(fine.astype(F32) * w), jnp.sin(fine.astype(F32) * w)]
    full = lambda a: pl.BlockSpec(a.shape, lambda i: (0, 0))
    out = pl.BlockSpec((DFT_ROWS, length), lambda i: (i, 0))
    return pl.pallas_call(
        _dft_table_kernel,
        grid=(length // DFT_ROWS,),
        in_specs=[full(t) for t in tabs],
        out_specs=[out] * 3,
        out_shape=[jax.ShapeDtypeStruct((length, length), BF16)] * 3,
        compiler_params=pltpu.CompilerParams(dimension_semantics=("parallel",)),
        name="dft_table",
    )(*tabs)


def _hy_filter_kernel(feat_ref, w1_ref, b1_ref, w2_ref, b2_ref, fr_ref, w3_ref, win_ref, filt_ref, ss_ref):
    i = pl.program_id(0)
    hdn = jnp.sin(fr_ref[0:1, :] * (_dot_hi(feat_ref[...], w1_ref[...]) + b1_ref[...]))
    hdn = jnp.sin(fr_ref[1:2, :] * (_dot_hi(hdn, w2_ref[...]) + b2_ref[...]))
    filt = _dot_hi(hdn, w3_ref[...]) * win_ref[...]
    half = filt.shape[1] // 2
    row = lax.broadcasted_iota(jnp.int32, filt.shape, 0) + i * filt.shape[0]
    col = lax.broadcasted_iota(jnp.int32, filt.shape, 1)
    filt = jnp.where(jnp.logical_and(row == 0, col >= half), 0.0, filt)
    filt_ref[...] = filt.astype(BF16)
    sq = jnp.sum(filt * filt, axis=0, keepdims=True)

    @pl.when(i == 0)
    def _():
        ss_ref[...] = jnp.zeros_like(ss_ref)

    ss_ref[...] += sq[:, :half] + sq[:, half:]


def _hyena_filters(length, w1, b1, w2, b2, freq, w3):
    tile = min(HY_FILT_TILE, length)
    t = jnp.linspace(0.0, 1.0, length, dtype=F32)[:, None]
    pos = jnp.arange(length, dtype=F32)[:, None]
    band = jnp.linspace(1e-4, HY_BANDS - 1, HY_BANDS, dtype=F32)[None, :]
    ang = 2.0 * math.pi * pos * band / length
    feats = jnp.concatenate([t, jnp.cos(ang), -jnp.sin(ang)], -1)
    deltas = jnp.abs(jnp.linspace(math.log(HY_TARGET) / HY_SLOW_DECAY,
                                  math.log(HY_TARGET) / HY_FAST_DECAY, W_GROUP, dtype=F32))
    window = jnp.tile(jnp.exp(-t * deltas), (1, 2 * HY_ORDER))
    n_out = w3.shape[1]
    emb, ffn = w1.shape
    full = lambda a: pl.BlockSpec(a.shape, lambda i: (0,) * a.ndim)
    args = [w1.astype(F32), b1.astype(F32).reshape(1, ffn), w2.astype(F32), b2.astype(F32).reshape(1, ffn),
            freq.astype(F32), w3.astype(F32)]
    filt, ss = pl.pallas_call(
        _hy_filter_kernel,
        grid=(length // tile,),
        in_specs=[pl.BlockSpec((tile, emb), lambda i: (i, 0))] + [full(a) for a in args] +
                 [pl.BlockSpec((tile, n_out), lambda i: (i, 0))],
        out_specs=[pl.BlockSpec((tile, n_out), lambda i: (i, 0)),
                   pl.BlockSpec((1, n_out // 2), lambda i: (0, 0))],
        out_shape=[jax.ShapeDtypeStruct((length, n_out), BF16),
                   jax.ShapeDtypeStruct((1, n_out // 2), F32)],
        compiler_params=pltpu.CompilerParams(
            dimension_semantics=("arbitrary",), vmem_limit_bytes=VMEM_LIMIT),
        name="hyena_filter",
    )(feats, *args, window)
    return filt, ss


def _spec_mul(yr, yi, kr, ki, first):
    row0 = jnp.logical_and(first, lax.broadcasted_iota(jnp.int32, yr.shape, 0) == 0)
    zr = yr * kr - jnp.where(row0, 0.0, yi * ki)
    zi = jnp.where(row0, yi * ki, yr * ki + yi * kr)
    return zr, zi


def _dft_filter_kernel(wre_ref, wim_ref, x_ref, ss_ref, kre_ref, kim_ref):
    x = x_ref[...]
    yr = jnp.dot(wre_ref[...], x, preferred_element_type=F32)
    yi = jnp.dot(wim_ref[...], x, preferred_element_type=F32)
    half = yr.shape[1] // 2
    row0 = jnp.logical_and(pl.program_id(0) == 0,
                           lax.broadcasted_iota(jnp.int32, (yr.shape[0], half), 0) == 0)
    n = 2 * x.shape[0]
    scale = lax.rsqrt(ss_ref[...] + EPS) * jnp.where(row0, 1.0 / n, 2.0 / n)
    kre_ref[...] = (yr[:, :half] + yr[:, half:]) * scale
    kim_ref[...] = (yi[:, :half] + jnp.where(row0, yi[:, half:], -yi[:, half:])) * scale


def _dft_filter(f_re, f_im, filt, ss):
    length, n = filt.shape
    tile = min(DFT_TILE, length)
    wspec = pl.BlockSpec((tile, length), lambda i: (i, 0))
    return pl.pallas_call(
        _dft_filter_kernel,
        grid=(length // tile,),
        in_specs=[wspec, wspec, pl.BlockSpec((length, n), lambda i: (0, 0)),
                  pl.BlockSpec((1, n // 2), lambda i: (0, 0))],
        out_specs=[pl.BlockSpec((tile, n // 2), lambda i: (i, 0))] * 2,
        out_shape=[jax.ShapeDtypeStruct((length, n // 2), F32)] * 2,
        compiler_params=pltpu.CompilerParams(
            dimension_semantics=("parallel",), vmem_limit_bytes=VMEM_LIMIT),
        name="hyena_filter_dft",
    )(f_re, f_im, filt, ss)


def _dft_fwd_kernel(wre_ref, wim_ref, y_ref, kre_ref, kim_ref, zre_ref, zim_ref):
    y = y_ref[...].astype(BF16)
    yr = jnp.dot(wre_ref[...], y, preferred_element_type=F32)
    yi = jnp.dot(wim_ref[...], y, preferred_element_type=F32)
    zr, zi = _spec_mul(yr, yi, kre_ref[...], kim_ref[...], pl.program_id(0) == 0)
    zre_ref[...] = zr.astype(BF16)
    zim_ref[...] = zi.astype(BF16)


def _dft_fwd(f_re, f_im, y, k_re, k_im, order, row0):
    n_batch = y.shape[0]
    length = f_re.shape[0]
    tile = min(DFT_TILE, length)
    wspec = pl.BlockSpec((tile, length), lambda i, b: (i, 0))
    kspec = pl.BlockSpec((tile, W_GROUP), lambda i, b: (i, order))
    zspec = pl.BlockSpec((None, tile, W_GROUP), lambda i, b: (b, i, 0))
    return pl.pallas_call(
        _dft_fwd_kernel,
        grid=(length // tile, n_batch),
        in_specs=[wspec, wspec,
                  pl.BlockSpec((None, length, W_GROUP), lambda i, b: (b, row0 // length, 0)),
                  kspec, kspec],
        out_specs=[zspec, zspec],
        out_shape=[jax.ShapeDtypeStruct((n_batch, length, W_GROUP), BF16)] * 2,
        compiler_params=pltpu.CompilerParams(
            dimension_semantics=("parallel", "arbitrary"), vmem_limit_bytes=VMEM_LIMIT),
        name="hyena_dft_fwd",
    )(f_re, f_im, y, k_re, k_im)


def _dft_inv_kernel(wre_ref, wim_ref, zre_ref, zim_ref, y_ref, gate_ref, bias_ref, o_ref, obf_ref):
    conv = (jnp.dot(wre_ref[...], zre_ref[...], preferred_element_type=F32) +
            jnp.dot(wim_ref[...], zim_ref[...], preferred_element_type=F32))
    out = gate_ref[...] * (conv + bias_ref[...] * y_ref[...])
    o_ref[...] = out
    obf_ref[...] = out.astype(BF16)


def _dft_inv(i_re, i_im, z_re, z_im, y_src, gate_src, bias):
    n_batch, length, _ = z_re.shape
    tile = min(DFT_TILE, length)
    wspec = pl.BlockSpec((tile, length), lambda i, b: (i, 0))
    zspec = pl.BlockSpec((None, length, W_GROUP), lambda i, b: (b, 0, 0))
    ospec = pl.BlockSpec((None, tile, W_GROUP), lambda i, b: (b, i, 0))
    view = lambda src: pl.BlockSpec((None, tile, W_GROUP), lambda i, b: (b, src[1] // tile + i, src[2]))
    return pl.pallas_call(
        _dft_inv_kernel,
        grid=(length // tile, n_batch),
        in_specs=[wspec, wspec, zspec, zspec, view(y_src), view(gate_src),
                  pl.BlockSpec((1, W_GROUP), lambda i, b: (0, 0))],
        out_specs=[ospec, ospec],
        out_shape=[jax.ShapeDtypeStruct((n_batch, length, W_GROUP), F32),
                   jax.ShapeDtypeStruct((n_batch, length, W_GROUP), BF16)],
        compiler_params=pltpu.CompilerParams(
            dimension_semantics=("parallel", "arbitrary"), vmem_limit_bytes=VMEM_LIMIT),
        name="hyena_dft_inv",
    )(i_re, i_im, z_re, z_im, y_src[0], gate_src[0], bias.astype(F32).reshape(1, W_GROUP))


def _hyena_segment(u, row0, length, dft, filt_params, bias):
    f_re, f_im, i_im = dft
    i_re = f_re
    filt, ss = _hyena_filters(length, *filt_params)
    k_re, k_im = _dft_filter(f_re, f_im, filt, ss)
    z_re, z_im = _dft_fwd(f_re, f_im, u, k_re, k_im, 0, row0)
    y, y_bf = _dft_inv(i_re, i_im, z_re, z_im, (u, row0, 0), (u, row0, 1), bias[0])
    z_re, z_im = _dft_fwd(f_re, f_im, y_bf, k_re, k_im, 1, 0)
    y, _ = _dft_inv(i_re, i_im, z_re, z_im, (y, 0, 0), (u, row0, 2), bias[1])
    return y


GT_COLS = 4


def _grid_transpose_kernel(x_ref, c_ref, o_ref, *, n_major, n_lat_steps):
    j = pl.program_id(1)
    for half in range(2):
        @pl.when(jnp.logical_and(j < n_lat_steps, j % 2 == half))
        def _():
            for i in range(GT_COLS):
                o_ref[i * n_major:(i + 1) * n_major, :] = x_ref[:, half * GT_COLS + i, :]

    @pl.when(j >= n_lat_steps)
    def _():
        o_ref[...] = c_ref[...]


def _grid_transpose(t, n_lat, n_major):
    n_batch, seq, w = t.shape
    n_minor = n_lat // n_major
    rows = GT_COLS * n_major
    n_lat_steps = n_minor // GT_COLS
    view = t.reshape(n_batch, seq // n_minor, n_minor, w)
    return pl.pallas_call(
        functools.partial(_grid_transpose_kernel, n_major=n_major, n_lat_steps=n_lat_steps),
        grid=(n_batch, seq // rows),
        in_specs=[pl.BlockSpec((None, n_major, 2 * GT_COLS, w),
                               lambda b, j: (b, 0, jnp.minimum(j, n_lat_steps - 1) // 2, 0)),
                  pl.BlockSpec((None, rows, w), lambda b, j: (b, jnp.maximum(j, n_lat_steps), 0))],
        out_specs=pl.BlockSpec((None, rows, w), lambda b, j: (b, j, 0)),
        out_shape=jax.ShapeDtypeStruct((n_batch, seq, w), t.dtype),
        compiler_params=pltpu.CompilerParams(
            dimension_semantics=("parallel", "arbitrary"), vmem_limit_bytes=VMEM_LIMIT),
        name="grid_transpose",
    )(view, t)


def _permute_w_in(w_in):
    o_ml = HG_COLS
    o_hy = o_ml + ML_MAIN + ML_GATES
    o_gd = o_hy + HY_COLS
    parts = [w_in[:, :HG_COLS], w_in[:, o_ml:o_ml + ML_MAIN], w_in[:, o_hy:o_gd],
             w_in[:, o_gd:o_gd + GD_MAIN], w_in[:, o_ml + ML_MAIN:o_hy],
             w_in[:, o_gd + GD_MAIN:],
             jnp.zeros((w_in.shape[0], GATE_PAD - ML_GATES - GD_GATES), w_in.dtype)]
    return jnp.concatenate(parts, 1).astype(BF16)


IN_SPLITS = (HG_COLS, ML_MAIN, HY_COLS, GD_MAIN, GATE_PAD)


def kernel(x, c, ctx, c_ctx, mod_w, mod_b, norm1_g, norm2_g, w_in, w_out, hg_lb_logits, hg_norm_g,
           ml_gate_b, ml_norm_g, hy_conv_w, hy_conv_b, hy_w1, hy_b1, hy_w2, hy_b2, hy_freq, hy_w3,
           hy_bias, gd_conv_w, gd_a_log, gd_dt_bias, gd_norm_g, router_w, router_b, exp_w1, exp_b1,
           exp_w2, exp_b2, final_g):
    n_batch, seq, d = x.shape
    ctx_len = ctx.shape[1]
    depth = mod_w.shape[0]
    rows = seq // GRID_W
    p = jax.nn.softmax(hg_lb_logits.astype(F32), axis=0)
    lower_bounds = jnp.cumsum(p, 0) - p[0]
    s_all = jnp.concatenate([jax.nn.silu(c), jax.nn.silu(c_ctx)[None]], 0)
    xs = jnp.concatenate([x, ctx], 1)
    colmajor = lambda t: _grid_transpose(t, seq, rows)
    rowmajor = lambda t: _grid_transpose(t, seq, GRID_W)
    dft_lat, dft_ctx = _dft_matrices(seq), _dft_matrices(ctx_len)
    for l in range(depth):
        mod = (s_all @ mod_w[l] + mod_b[l]).reshape(n_batch + 1, 6, 1, d)
        sh1, sc1, g1, sh2, sc2, g2 = (mod[:, i] for i in range(6))
        z_hg, z_ml, z_hy, z_gd, z_gt = _in_proj(xs, norm1_g[l], sc1, sh1,
                                                _permute_w_in(w_in[l]), IN_SPLITS)
        hg_f, hg_b = _hgrn2_scan(z_hg, lower_bounds[l])
        ml_f, ml_b = _mlstm_scan(colmajor(z_ml), colmajor(z_gt), ml_gate_b[l], ctx_len)
        filt = (hy_w1[l], hy_b1[l], hy_w2[l], hy_b2[l], hy_freq[l], hy_w3[l])
        u_hy = _short_conv_tiles(z_hy, hy_conv_w[l], hy_conv_b[l], seq, False)
        c_lat = _hyena_segment(u_hy, 0, seq, dft_lat, filt, hy_bias[l])
        if l == depth - 1:
            c_ctx = jnp.zeros((n_batch, ctx_len, W_GROUP), F32)
        else:
            c_ctx = _hyena_segment(u_hy, seq, ctx_len, dft_ctx, filt, hy_bias[l])
        qkv = _short_conv_tiles(z_gd, gd_conv_w[l], jnp.zeros((3 * W_GROUP,), F32), seq, True)
        gd_f, gd_b = _gdn_scan(qkv, z_gt, gd_a_log[l], gd_dt_bias[l], ctx_len)
        mixers = [("silu", hg_f, hg_b, (z_hg, 4)),
                  ("sigmoid", rowmajor(ml_f), rowmajor(ml_b), (z_ml, 3)),
                  ("final", jnp.concatenate([c_lat, c_ctx], 1)),
                  ("silu", gd_f, gd_b, (z_gd, 3))]
        gains = jnp.stack([hg_norm_g[l], ml_norm_g[l], jnp.ones_like(ml_norm_g[l]), gd_norm_g[l]]).astype(F32)
        xs = _out_proj(xs, mixers, gains, g1, w_out[l].astype(BF16))
        xs = _moe_layer(xs, norm2_g[l], sc2, sh2, g2, router_w[l], router_b[l],
                        l, exp_w1, exp_b1[l], exp_w2, exp_b2[l])
    return _final_norm(xs, final_g, seq)
```

```python
import functools
import math

import jax
import jax.numpy as jnp
from jax import lax
from jax.experimental import pallas as pl
from jax.experimental.pallas import tpu as pltpu

F32 = jnp.float32
BF16 = jnp.bfloat16

D_MODEL = 1024
GRID_W = 64
N_MIXERS = 4
W_GROUP = D_MODEL // N_MIXERS
HG_HEADS = 4
HG_DK = W_GROUP // HG_HEADS
HG_CHUNK = 16
ML_HEADS = 4
ML_DH = W_GROUP // ML_HEADS
ML_CHUNK = 64
HY_ORDER = 2
HY_BANDS = 8
HY_FAST_DECAY = 0.3
HY_SLOW_DECAY = 1.5
HY_TARGET = 1e-2
GD_HEADS = 4
GD_DH = W_GROUP // GD_HEADS
GD_CHUNK = 64
N_EXPERTS = 32
TOP_K = 4
SWIGLU_LIMIT = 7.0
SWIGLU_ALPHA = 1.702
EPS = 1e-6
NEG = -1e30
HG_COLS = 5 * W_GROUP
ML_MAIN = 4 * W_GROUP
ML_GATES = 4 * ML_HEADS
HY_COLS = 3 * W_GROUP
GD_MAIN = 4 * W_GROUP
GD_GATES = 4 * GD_HEADS
GATE_PAD = 128

ROW_TILE = 256
MOE_TILE = 256
VMEM_LIMIT = 56 * 1024 * 1024


def _norm_mod(x, gain, sc, sh):
    y = x * lax.rsqrt(jnp.mean(x * x, -1, keepdims=True) + EPS)
    return y * gain * (1.0 + sc) + sh


def _in_proj_kernel(x_ref, g_ref, sc_ref, sh_ref, w_ref, *out_refs, splits):
    h = _norm_mod(x_ref[...], g_ref[...], sc_ref[...], sh_ref[...]).astype(BF16)
    off = 0
    for o_ref, n in zip(out_refs, splits):
        o_ref[...] = jnp.dot(h, w_ref[:, off:off + n], preferred_element_type=F32)
        off += n


def _mod_index_map(n_batch, n_tiles):
    return lambda b, j: (jnp.where(j == n_tiles - 1, n_batch, b), 0, 0)


def _in_proj(xs, gain, sc_tab, sh_tab, w, splits):
    n_batch, seq, d = xs.shape
    n = w.shape[1]
    grid = (n_batch, seq // ROW_TILE)
    mod_spec = pl.BlockSpec((None, 1, d), _mod_index_map(n_batch, seq // ROW_TILE))
    return pl.pallas_call(
        functools.partial(_in_proj_kernel, splits=splits),
        grid=grid,
        in_specs=[
            pl.BlockSpec((None, ROW_TILE, d), lambda b, j: (b, j, 0)),
            pl.BlockSpec((1, d), lambda b, j: (0, 0)),
            mod_spec, mod_spec,
            pl.BlockSpec((d, n), lambda b, j: (0, 0)),
        ],
        out_specs=[pl.BlockSpec((None, ROW_TILE, s), lambda b, j: (b, j, 0)) for s in splits],
        out_shape=[jax.ShapeDtypeStruct((n_batch, seq, s), F32) for s in splits],
        compiler_params=pltpu.CompilerParams(
            dimension_semantics=("parallel", "parallel"), vmem_limit_bytes=VMEM_LIMIT),
        name="in_proj",
    )(xs, gain.reshape(1, d), sc_tab, sh_tab, w)


def _moe_route_kernel(x_ref, g_ref, sc_ref, sh_ref, rw_ref, rb_ref, tri_ref, h_ref, idx_ref, gate_ref,
                      rank_ref, cnt_ref, carry):
    @pl.when(jnp.logical_and(pl.program_id(0) == 0, pl.program_id(1) == 0))
    def _():
        carry[...] = jnp.zeros_like(carry)

    h = _norm_mod(x_ref[...], g_ref[...], sc_ref[...], sh_ref[...])
    h_ref[...] = h
    work = _dot_hi(h, rw_ref[...]) + rb_ref[...]
    lane = lax.broadcasted_iota(jnp.int32, work.shape, 1)
    vals, hits = [], []
    for _ in range(TOP_K):
        m = jnp.max(work, axis=-1, keepdims=True)
        first = jnp.min(jnp.where(work == m, lane, GATE_PAD), axis=-1, keepdims=True)
        hit = lane == first
        vals.append(m)
        hits.append(hit)
        work = jnp.where(hit, -jnp.inf, work)
    exps = [jnp.exp(v - vals[0]) for v in vals]
    total = sum(exps)
    chosen = sum(hit.astype(F32) for hit in hits)
    before = jnp.dot(tri_ref[...], chosen.astype(BF16), preferred_element_type=F32) + carry[...]
    idx = jnp.zeros(work.shape, jnp.int32)
    gate = jnp.zeros(work.shape, F32)
    rank = jnp.zeros(work.shape, jnp.int32)
    for k in range(TOP_K):
        idx = jnp.where(lane == k, jnp.sum(jnp.where(hits[k], lane, 0), axis=-1, keepdims=True), idx)
        gate = jnp.where(lane == k, exps[k] / total, gate)
        r_k = jnp.sum(jnp.where(hits[k], before, 0.0), axis=-1, keepdims=True)
        rank = jnp.where(lane == k, r_k.astype(jnp.int32), rank)
    idx_ref[...] = idx
    gate_ref[...] = gate
    rank_ref[...] = rank
    carry[...] += jnp.sum(chosen, axis=0, keepdims=True)
    cnt_ref[...] = carry[...]


def _moe_route(xs, gain, sc_tab, sh_tab, router_w, router_b):
    n_batch, seq, d = xs.shape
    pad = GATE_PAD - N_EXPERTS
    rw = jnp.pad(router_w.astype(F32), ((0, 0), (0, pad)))
    rb = jnp.pad(router_b.astype(F32), (0, pad), constant_values=NEG).reshape(1, GATE_PAD)
    r = lax.broadcasted_iota(jnp.int32, (ROW_TILE, ROW_TILE), 0)
    c = lax.broadcasted_iota(jnp.int32, (ROW_TILE, ROW_TILE), 1)
    tri = (c < r).astype(BF16)
    mod_spec = pl.BlockSpec((None, 1, d), _mod_index_map(n_batch, seq // ROW_TILE))
    tok = lambda w: pl.BlockSpec((None, ROW_TILE, w), lambda b, j: (b, j, 0))
    return pl.pallas_call(
        _moe_route_kernel,
        grid=(n_batch, seq // ROW_TILE),
        in_specs=[
            tok(d),
            pl.BlockSpec((1, d), lambda b, j: (0, 0)),
            mod_spec, mod_spec,
            pl.BlockSpec((d, GATE_PAD), lambda b, j: (0, 0)),
            pl.BlockSpec((1, GATE_PAD), lambda b, j: (0, 0)),
            pl.BlockSpec((ROW_TILE, ROW_TILE), lambda b, j: (0, 0)),
        ],
        out_specs=[tok(d), tok(GATE_PAD), tok(GATE_PAD), tok(GATE_PAD),
                   pl.BlockSpec((1, GATE_PAD), lambda b, j: (0, 0))],
        out_shape=[jax.ShapeDtypeStruct((n_batch, seq, d), F32),
                   jax.ShapeDtypeStruct((n_batch, seq, GATE_PAD), jnp.int32),
                   jax.ShapeDtypeStruct((n_batch, seq, GATE_PAD), F32),
                   jax.ShapeDtypeStruct((n_batch, seq, GATE_PAD), jnp.int32),
                   jax.ShapeDtypeStruct((1, GATE_PAD), F32)],
        scratch_shapes=[pltpu.VMEM((1, GATE_PAD), F32)],
        compiler_params=pltpu.CompilerParams(
            dimension_semantics=("arbitrary", "arbitrary"), vmem_limit_bytes=VMEM_LIMIT),
        name="moe_route",
    )(xs, gain.reshape(1, d), sc_tab, sh_tab, rw, rb, tri)


def _out_proj_kernel(*refs, kinds):
    x_ref, refs = refs[0], refs[1:]
    gains_ref, g_ref, heads_ref, w_ref, o_ref = refs[-5:]
    heads = heads_ref[...] * (1.0 / HG_DK)
    acc = jnp.zeros(o_ref.shape, F32)
    pos = 0
    for i, kind in enumerate(kinds):
        if kind == "final":
            m = refs[pos][...]
            pos += 1
        else:
            o = refs[pos][...] + refs[pos + 1][...]
            gate = refs[pos + 2][...]
            pos += 3
            ms = _dot_sel(o * o, heads, 1, 3)
            o = o * lax.rsqrt(ms + EPS) * gains_ref[i:i + 1, :]
            m = o * (jax.nn.silu(gate) if kind == "silu" else jax.nn.sigmoid(gate))
        acc += jnp.dot(m.astype(BF16), w_ref[i * W_GROUP:(i + 1) * W_GROUP, :],
                       preferred_element_type=F32)
    o_ref[...] = x_ref[...] + g_ref[...] * acc


def _out_proj(xs, mixers, gains, gate_tab, w_out):
    n_batch, seq, d = xs.shape
    row = lambda w, cb=0: pl.BlockSpec((None, ROW_TILE, w), lambda b, j: (b, j, cb))
    args, specs, kinds = [], [], []
    for m in mixers:
        kinds.append(m[0])
        if m[0] == "final":
            args.append(m[1])
            specs.append(row(W_GROUP))
        else:
            src, cb = m[3]
            args += [m[1], m[2], src]
            specs += [row(W_GROUP), row(W_GROUP), row(W_GROUP, cb)]
    heads = _head_mask()
    return pl.pallas_call(
        functools.partial(_out_proj_kernel, kinds=tuple(kinds)),
        grid=(n_batch, seq // ROW_TILE),
        in_specs=[row(d)] + specs + [
            pl.BlockSpec((N_MIXERS, W_GROUP), lambda b, j: (0, 0)),
            pl.BlockSpec((None, 1, d), _mod_index_map(n_batch, seq // ROW_TILE)),
            pl.BlockSpec((W_GROUP, W_GROUP), lambda b, j: (0, 0)),
            pl.BlockSpec((d, d), lambda b, j: (0, 0)),
        ],
        out_specs=row(d),
        out_shape=jax.ShapeDtypeStruct((n_batch, seq, d), F32),
        compiler_params=pltpu.CompilerParams(
            dimension_semantics=("parallel", "parallel"), vmem_limit_bytes=VMEM_LIMIT),
        name="out_proj",
    )(xs, *args, gains, gate_tab, heads, w_out)


def _row_copy(src, src_row, dst, dst_row, sem):
    return pltpu.make_async_copy(src.at[pl.ds(src_row, 1), :], dst.at[pl.ds(dst_row, 1), :], sem)


def _dispatch_kernel(tail_ref, dest_ref, h_ref, slots_hbm, zeros_v, sem, zero_sem):
    def fill(start):
        start = pl.multiple_of(start, MOE_TILE)
        return pltpu.make_async_copy(zeros_v, slots_hbm.at[pl.ds(start, MOE_TILE), :], zero_sem)

    def each_unused_tile(fn):
        first = tail_ref[N_EXPERTS] // MOE_TILE
        lax.fori_loop(first, slots_hbm.shape[0] // MOE_TILE, lambda t, c: (fn(fill(t * MOE_TILE)), c)[1], 0)

    @pl.when(pl.program_id(0) == 0)
    def _():
        zeros_v[...] = jnp.zeros_like(zeros_v)
        for e in range(N_EXPERTS):
            @pl.when(tail_ref[e] >= 0)
            def _():
                fill(tail_ref[e]).start()
        each_unused_tile(lambda copy: copy.start())
        for e in range(N_EXPERTS):
            @pl.when(tail_ref[e] >= 0)
            def _():
                fill(tail_ref[e]).wait()
        each_unused_tile(lambda copy: copy.wait())

    def issue(r, carry):
        for k in range(TOP_K):
            _row_copy(h_ref, r, slots_hbm, dest_ref[0, r * TOP_K + k], sem).start(priority=k % 2)
        return carry

    lax.fori_loop(0, ROW_TILE, issue, 0, unroll=4)
    for k in range(TOP_K):
        pltpu.make_async_copy(h_ref, slots_hbm.at[pl.ds(0, ROW_TILE), :], sem).wait()


def _moe_dispatch(h, dest, tail, n_slots):
    t_count, d = h.shape
    return pl.pallas_call(
        _dispatch_kernel,
        grid_spec=pltpu.PrefetchScalarGridSpec(
            num_scalar_prefetch=1,
            grid=(t_count // ROW_TILE,),
            in_specs=[pl.BlockSpec((None, 1, ROW_TILE * TOP_K), lambda i, tl: (i, 0, 0),
                                   memory_space=pltpu.SMEM),
                      pl.BlockSpec((ROW_TILE, d), lambda i, tl: (i, 0))],
            out_specs=pl.BlockSpec(memory_space=pl.ANY),
            scratch_shapes=[pltpu.VMEM((MOE_TILE, d), F32), pltpu.SemaphoreType.DMA(()),
                            pltpu.SemaphoreType.DMA(())],
        ),
        out_shape=jax.ShapeDtypeStruct((n_slots, d), F32),
        compiler_params=pltpu.CompilerParams(
            dimension_semantics=("arbitrary",), vmem_limit_bytes=VMEM_LIMIT),
        name="moe_dispatch",
    )(tail, dest, h)


def _expert_kernel(bexp_ref, nused_ref, x_ref, w1_ref, b1_ref, w2_ref, b2_ref, o_ref, w1b, w2b):
    i = pl.program_id(0)
    changed = jnp.logical_or(i == 0, bexp_ref[i] != bexp_ref[jnp.maximum(i - 1, 0)])
    used = i < nused_ref[0]

    @pl.when(jnp.logical_and(changed, used))
    def _():
        w1b[...] = w1_ref[...].astype(BF16)
        w2b[...] = w2_ref[...].astype(BF16)

    @pl.when(used)
    def _():
        f = w2b.shape[0]
        u = jnp.dot(x_ref[...].astype(BF16), w1b[...], preferred_element_type=F32) + b1_ref[...]
        gate = jnp.minimum(u[:, :f], SWIGLU_LIMIT)
        lin = jnp.clip(u[:, f:], -SWIGLU_LIMIT, SWIGLU_LIMIT)
        y = (lin + 1.0) * gate * jax.nn.sigmoid(SWIGLU_ALPHA * gate)
        o_ref[...] = jnp.dot(y.astype(BF16), w2b[...], preferred_element_type=F32) + b2_ref[...]

    @pl.when(jnp.logical_not(used))
    def _():
        o_ref[...] = jnp.zeros_like(o_ref)


def _expert_ffn(block_exp, n_used, xg, layer, w1, b1, w2, b2):
    n_slots, d = xg.shape
    _, n_exp, _, f2 = w1.shape
    f = f2 // 2
    n_blocks = n_slots // MOE_TILE
    return pl.pallas_call(
        _expert_kernel,
        grid_spec=pltpu.PrefetchScalarGridSpec(
            num_scalar_prefetch=2,
            grid=(n_blocks,),
            in_specs=[
                pl.BlockSpec((MOE_TILE, d), lambda i, be, nu: (jnp.minimum(i, nu[0] - 1), 0)),
                pl.BlockSpec((None, None, d, f2), lambda i, be, nu: (layer, be[i], 0, 0)),
                pl.BlockSpec((None, 1, f2), lambda i, be, nu: (be[i], 0, 0)),
                pl.BlockSpec((None, None, f, d), lambda i, be, nu: (layer, be[i], 0, 0)),
                pl.BlockSpec((None, 1, d), lambda i, be, nu: (be[i], 0, 0)),
            ],
            out_specs=pl.BlockSpec((MOE_TILE, d), lambda i, be, nu: (i, 0)),
            scratch_shapes=[pltpu.VMEM((d, f2), BF16), pltpu.VMEM((f, d), BF16)],
        ),
        out_shape=jax.ShapeDtypeStruct((n_slots, d), F32),
        compiler_params=pltpu.CompilerParams(
            dimension_semantics=("arbitrary",), vmem_limit_bytes=VMEM_LIMIT),
        name="expert_ffn",
    )(block_exp, n_used, xg, w1, b1.reshape(n_exp, 1, f2), w2, b2.reshape(n_exp, 1, d))


def _combine_kernel(dest_ref, y_hbm, x_ref, gate_ref, g_ref, o_ref, ybuf, sem):
    def issue(r, carry):
        for k in range(TOP_K):
            pltpu.make_async_copy(y_hbm.at[pl.ds(dest_ref[0, r * TOP_K + k], 1), :],
                                  ybuf.at[k, pl.ds(r, 1), :], sem).start(priority=k % 2)
        return carry

    lax.fori_loop(0, ROW_TILE, issue, 0, unroll=4)
    for k in range(TOP_K):
        pltpu.make_async_copy(y_hbm.at[pl.ds(0, ROW_TILE), :], ybuf.at[k], sem).wait()
    acc = jnp.zeros(o_ref.shape, F32)
    for k in range(TOP_K):
        acc += gate_ref[:, k:k + 1] * ybuf[k]
    o_ref[...] = x_ref[...] + g_ref[...] * acc


def _moe_combine(xs, y_slots, dest, gates, gate_tab):
    n_batch, seq, d = xs.shape
    n_tiles = seq // ROW_TILE
    tok = lambda w: pl.BlockSpec((None, ROW_TILE, w), lambda b, j: (b, j, 0))
    return pl.pallas_call(
        _combine_kernel,
        grid=(n_batch, n_tiles),
        in_specs=[pl.BlockSpec((None, 1, ROW_TILE * TOP_K), lambda b, j: (b * n_tiles + j, 0, 0),
                               memory_space=pltpu.SMEM),
                  pl.BlockSpec(memory_space=pl.ANY),
                  tok(d), tok(GATE_PAD),
                  pl.BlockSpec((None, 1, d), _mod_index_map(n_batch, n_tiles))],
        out_specs=tok(d),
        out_shape=jax.ShapeDtypeStruct((n_batch, seq, d), F32),
        scratch_shapes=[pltpu.VMEM((TOP_K, ROW_TILE, d), F32), pltpu.SemaphoreType.DMA(())],
        compiler_params=pltpu.CompilerParams(
            dimension_semantics=("arbitrary", "arbitrary"), vmem_limit_bytes=VMEM_LIMIT),
        name="moe_combine",
    )(dest, y_slots, xs, gates, gate_tab)


def _moe_layer(xs, gain, sc_tab, sh_tab, gate_tab, router_w, router_b, layer, w1, b1, w2, b2):
    n_batch, seq, d = xs.shape
    t_count = n_batch * seq
    h, idx, gates, rank, counts = _moe_route(xs, gain, sc_tab, sh_tab, router_w, router_b)
    counts = counts[0, :N_EXPERTS].astype(jnp.int32)
    padded = (counts + MOE_TILE - 1) // MOE_TILE * MOE_TILE
    pend = jnp.cumsum(padded)
    pstart = pend - padded
    top_i = idx.reshape(t_count, GATE_PAD)[:, :TOP_K]
    hit = top_i[:, :, None] == jnp.arange(N_EXPERTS, dtype=jnp.int32)
    dest = rank.reshape(t_count, GATE_PAD)[:, :TOP_K] + jnp.sum(jnp.where(hit, pstart, 0), -1)
    dest = dest.astype(jnp.int32).reshape(t_count // ROW_TILE, 1, ROW_TILE * TOP_K)
    n_blocks = -(-(t_count * TOP_K + N_EXPERTS * (MOE_TILE - 1)) // MOE_TILE)
    tile_start = jnp.arange(n_blocks, dtype=jnp.int32)[:, None] * MOE_TILE
    block_exp = jnp.minimum(jnp.sum((pend[None, :] <= tile_start).astype(jnp.int32), -1), N_EXPERTS - 1)
    n_used = (pend[-1:] // MOE_TILE).astype(jnp.int32)
    tail = jnp.concatenate([jnp.where(padded > 0, pend - MOE_TILE, -1), pend[-1:]]).astype(jnp.int32)
    slots = _moe_dispatch(h.reshape(t_count, d), dest, tail, n_blocks * MOE_TILE)
    y_slots = _expert_ffn(block_exp, n_used, slots, layer, w1, b1, w2, b2)
    return _moe_combine(xs, y_slots, dest, gates, gate_tab)


def _final_norm_kernel(x_ref, g_ref, o_ref):
    x = x_ref[...]
    o_ref[...] = x * lax.rsqrt(jnp.mean(x * x, -1, keepdims=True) + EPS) * g_ref[...]


def _final_norm(xs, gain, n_lat):
    n_batch, _, d = xs.shape
    return pl.pallas_call(
        _final_norm_kernel,
        grid=(n_batch, n_lat // ROW_TILE),
        in_specs=[pl.BlockSpec((None, ROW_TILE, d), lambda b, j: (b, j, 0)),
                  pl.BlockSpec((1, d), lambda b, j: (0, 0))],
        out_specs=pl.BlockSpec((None, ROW_TILE, d), lambda b, j: (b, j, 0)),
        out_shape=jax.ShapeDtypeStruct((n_batch, n_lat, d), F32),
        compiler_params=pltpu.CompilerParams(dimension_semantics=("parallel", "parallel")),
        name="final_norm",
    )(xs, gain.reshape(1, d))


HG_SUB = 16


def _hg_consts():
    r = lax.broadcasted_iota(jnp.int32, (ROW_TILE, ROW_TILE), 0)
    c = lax.broadcasted_iota(jnp.int32, (ROW_TILE, ROW_TILE), 1)
    same_chunk = (r // HG_SUB) == (c // HG_SUB)
    tri_f = jnp.where(same_chunk & (c <= r), 1.0, 0.0).astype(F32)
    tri_b = jnp.where(same_chunk & (c >= r), 1.0, 0.0).astype(F32)
    blk = jnp.where(same_chunk, 1.0, 0.0).astype(F32)
    return tri_f, tri_b, blk


def _hg_prepare(z_ref, zf_col, lb, tri, blk, qs_s, key_s, cm_s, qh_s, kh_s, dec_s):
    q = jax.nn.silu(z_ref[:, 0:W_GROUP])
    zf = z_ref[:, zf_col:zf_col + W_GROUP]
    f = lb + (1.0 - lb) * jax.nn.sigmoid(zf)
    key = (1.0 - lb) * jax.nn.sigmoid(-zf)
    logf = jnp.log(f)
    cum = _dot_sel(tri, logf, 0, 3)
    tot = _dot_sel(blk, logf, 0, 3)
    qs_s[...] = q
    key_s[...] = key
    cm_s[...] = cum
    qh_s[...] = (q * jnp.exp(cum)).astype(BF16)
    kh_s[...] = (key * jnp.exp(tot - cum)).astype(BF16)
    dec_s[...] = jnp.exp(tot)


def _hg_chunk(c, reverse, z_ref, o_ref, st_ref, heads_bf, heads_f, qs_s, key_s, cm_s, qh_s, kh_s, dec_s):
    sl = pl.ds(pl.multiple_of(c * HG_SUB, HG_SUB), HG_SUB)
    q, k, cm = qs_s[sl, :], key_s[sl, :], cm_s[sl, :]
    v = z_ref[sl, W_GROUP:2 * W_GROUP]
    st = st_ref[...]
    o = lax.dot_general(qh_s[sl, :], st.astype(BF16), (((1,), (1,)), ((), ())),
                        preferred_element_type=F32)
    t_idx = lax.broadcasted_iota(jnp.int32, (HG_SUB, W_GROUP), 0)
    parts = []
    for s in range(HG_SUB):
        live = (t_idx <= s) if reverse else (t_idx >= s)
        e = jnp.exp(jnp.where(live, cm - cm[s:s + 1, :], NEG))
        parts.append((q * k[s:s + 1, :] * e).astype(BF16))
    r = jnp.dot(jnp.concatenate(parts, 0), heads_bf, preferred_element_type=F32)
    for s in range(HG_SUB):
        o += r[s * HG_SUB:(s + 1) * HG_SUB, :] * v[s:s + 1, :]
    o_ref[sl, :] = o
    ut = lax.dot_general(v.astype(BF16), kh_s[sl, :], (((0,), (0,)), ((), ())),
                         preferred_element_type=F32)
    st_ref[...] = st * dec_s[sl, :][0:1, :] + ut * heads_f


def _hgrn2_kernel(zf_ref, zb_ref, lb_ref, trif_ref, trib_ref, blk_ref, heads_ref, of_ref, ob_ref,
                  st_s, qs_s, key_s, cm_s, qh_s, kh_s, dec_s):
    @pl.when(pl.program_id(1) == 0)
    def _():
        st_s[...] = jnp.zeros_like(st_s)

    lb = lb_ref[...]
    blk = blk_ref[...]
    heads_f = heads_ref[...]
    heads_bf = heads_f.astype(BF16)
    _hg_prepare(zf_ref, 2 * W_GROUP, lb, trif_ref[...], blk, qs_s.at[0], key_s.at[0], cm_s.at[0],
                qh_s.at[0], kh_s.at[0], dec_s.at[0])
    _hg_prepare(zb_ref, 3 * W_GROUP, lb, trib_ref[...], blk, qs_s.at[1], key_s.at[1], cm_s.at[1],
                qh_s.at[1], kh_s.at[1], dec_s.at[1])
    n_chunks = ROW_TILE // HG_SUB

    def body(c, carry):
        _hg_chunk(c, False, zf_ref, of_ref, st_s.at[0], heads_bf, heads_f, qs_s.at[0], key_s.at[0],
                  cm_s.at[0], qh_s.at[0], kh_s.at[0], dec_s.at[0])
        _hg_chunk(n_chunks - 1 - c, True, zb_ref, ob_ref, st_s.at[1], heads_bf, heads_f, qs_s.at[1],
                  key_s.at[1], cm_s.at[1], qh_s.at[1], kh_s.at[1], dec_s.at[1])
        return carry

    lax.fori_loop(0, n_chunks, body, 0)


def _scan_tiles(n_lat_tiles):
    fwd = lambda j: jnp.where(j == 0, n_lat_tiles, j - 1)
    bwd = lambda j: jnp.where(j == 0, n_lat_tiles, n_lat_tiles - j)
    return fwd, bwd


def _hgrn2_scan(z_hg, lb):
    n_batch, seq, _ = z_hg.shape
    n_tiles = seq // ROW_TILE
    fwd, bwd = _scan_tiles(n_tiles - 1)
    const = pl.BlockSpec((ROW_TILE, ROW_TILE), lambda b, j: (0, 0))
    scr = lambda dt: pltpu.VMEM((2, ROW_TILE, W_GROUP), dt)
    return pl.pallas_call(
        _hgrn2_kernel,
        grid=(n_batch, n_tiles),
        in_specs=[pl.BlockSpec((None, ROW_TILE, HG_COLS), lambda b, j: (b, fwd(j), 0)),
                  pl.BlockSpec((None, ROW_TILE, HG_COLS), lambda b, j: (b, bwd(j), 0)),
                  pl.BlockSpec((1, W_GROUP), lambda b, j: (0, 0)),
                  const, const, const,
                  pl.BlockSpec((W_GROUP, W_GROUP), lambda b, j: (0, 0))],
        out_specs=[pl.BlockSpec((None, ROW_TILE, W_GROUP), lambda b, j: (b, fwd(j), 0)),
                   pl.BlockSpec((None, ROW_TILE, W_GROUP), lambda b, j: (b, bwd(j), 0))],
        out_shape=[jax.ShapeDtypeStruct((n_batch, seq, W_GROUP), F32)] * 2,
        scratch_shapes=[pltpu.VMEM((2, W_GROUP, W_GROUP), F32), scr(F32), scr(F32), scr(F32),
                        scr(BF16), scr(BF16), scr(F32)],
        compiler_params=pltpu.CompilerParams(
            dimension_semantics=("parallel", "arbitrary"), vmem_limit_bytes=VMEM_LIMIT),
        name="hgrn2_scan",
    )(z_hg, z_hg, lb.reshape(1, W_GROUP), *_hg_consts(), _head_mask())


def _dot_hi(a, b):
    return jnp.dot(a, b, preferred_element_type=F32, precision=lax.Precision.HIGHEST)


def _bf16_terms(x, terms):
    parts = []
    for _ in range(terms):
        p = x.astype(BF16)
        parts.append(p)
        x = x - p.astype(F32)
    return parts


def _dot_sel(a, b, select, terms, nt=False):
    ops = [a, b]
    fixed = ops[select].astype(BF16)
    dims = (((1,), (1,)), ((), ())) if nt else (((1,), (0,)), ((), ()))
    out = None
    for p in _bf16_terms(ops[1 - select], terms):
        pair = (fixed, p) if select == 0 else (p, fixed)
        d = lax.dot_general(pair[0], pair[1], dims, preferred_element_type=F32)
        out = d if out is None else out + d
    return out


def _dot_nt(a, b, precision=None):
    return lax.dot_general(a, b, (((1,), (1,)), ((), ())), preferred_element_type=F32,
                           precision=precision)


def _dot_tn(a, b):
    return lax.dot_general(a, b, (((0,), (0,)), ((), ())), preferred_element_type=F32)


def _chunk_consts(chunk, n_heads, gate_cols):
    r = lax.broadcasted_iota(jnp.int32, (chunk, chunk), 0)
    c = lax.broadcasted_iota(jnp.int32, (chunk, chunk), 1)
    low = (c <= r).astype(F32)
    upp = (c >= r).astype(F32)
    dh = W_GROUP // n_heads
    hr = lax.broadcasted_iota(jnp.int32, (W_GROUP, W_GROUP), 0) // dh
    hc = lax.broadcasted_iota(jnp.int32, (W_GROUP, W_GROUP), 1) // dh
    heads = (hr == hc).astype(F32)
    gc = lax.broadcasted_iota(jnp.int32, (GATE_PAD, W_GROUP), 0)
    gh = lax.broadcasted_iota(jnp.int32, (GATE_PAD, W_GROUP), 1) // dh
    expand = jnp.stack([(gc == g0 + gh).astype(F32) for g0 in gate_cols])
    sr = lax.broadcasted_iota(jnp.int32, (8, GATE_PAD), 0)
    sc = lax.broadcasted_iota(jnp.int32, (8, GATE_PAD), 1)
    select = jnp.stack([((sc == g0 + sr) & (sr < n_heads)).astype(F32) for g0 in gate_cols])
    return low, upp, heads, expand, select


class _MlDir:
    def __init__(self, reverse, z_ref, g_ref, gb, c_ref, n_ref, m_ref, low, upp, heads,
                 ex_i, ex_f, sel_i, sel_f):
        self.q = z_ref[:, 0:W_GROUP]
        self.k = z_ref[:, W_GROUP:2 * W_GROUP] * (ML_DH ** -0.5)
        self.v = z_ref[:, 2 * W_GROUP:3 * W_GROUP]
        gates = g_ref[...] + gb
        col_sum, row_sum = (upp, low) if reverse else (low, upp)
        self.li = _dot_sel(gates, ex_i, 1, 3)
        lf = jax.nn.log_sigmoid(_dot_sel(gates, ex_f, 1, 3))
        self.li_rows = _dot_sel(sel_i, gates, 0, 3, nt=True)
        lf_rows = jax.nn.log_sigmoid(_dot_sel(sel_f, gates, 0, 3, nt=True))
        self.fcum = _dot_sel(col_sum, lf, 0, 3)
        self.fcum_rows = _dot_sel(lf_rows, row_sum, 1, 3)
        last = 0 if reverse else ML_CHUNK - 1
        self.ftot = self.fcum[last:last + 1, :]
        g = self.ftot - self.fcum + self.li
        self.g_max = jnp.max(g, axis=0, keepdims=True)
        self.kw = self.k * jnp.exp(g - self.g_max)
        self.c0, self.n0, self.m0 = c_ref[...], n_ref[...], m_ref[...]
        self.k_bf = self.k.astype(BF16)
        self.qc = jnp.dot(self.q.astype(BF16), self.c0.astype(BF16), preferred_element_type=F32)
        self.qn = jnp.dot((self.q * self.n0).astype(BF16), heads.astype(BF16), preferred_element_type=F32)
        self.a_init = self.fcum + self.m0
        rr = lax.broadcasted_iota(jnp.int32, (ML_CHUNK, ML_CHUNK), 0)
        cc = lax.broadcasted_iota(jnp.int32, (ML_CHUNK, ML_CHUNK), 1)
        self.live = (cc >= rr) if reverse else (cc <= rr)


def _mlstm_kernel(zf_ref, zb_ref, gf_ref, gb_ref, bias_ref, low_ref, upp_ref, heads_ref, ex_ref, sel_ref,
                  of_ref, ob_ref, c_s, n_s, m_s):
    @pl.when(pl.program_id(1) == 0)
    def _():
        c_s[...] = jnp.zeros_like(c_s)
        n_s[...] = jnp.zeros_like(n_s)
        m_s[...] = jnp.zeros_like(m_s)

    low, upp, heads, bias = low_ref[...], upp_ref[...], heads_ref[...], bias_ref[...]
    dirs = [_MlDir(False, zf_ref, gf_ref, bias, c_s.at[0], n_s.at[0], m_s.at[0], low, upp, heads,
                   ex_ref[0], ex_ref[1], sel_ref[0], sel_ref[1]),
            _MlDir(True, zb_ref, gb_ref, bias, c_s.at[1], n_s.at[1], m_s.at[1], low, upp, heads,
                   ex_ref[2], ex_ref[3], sel_ref[2], sel_ref[3])]
    lane_head = lax.broadcasted_iota(jnp.int32, (ML_CHUNK, W_GROUP), 1) // ML_DH
    hs = lambda h: slice(h * ML_DH, (h + 1) * ML_DH)
    probs = [(d, h) for d in dirs for h in range(ML_HEADS)]
    qks = [_dot_nt(jnp.where(lane_head == h, d.q, 0.0).astype(BF16), d.k_bf) for d, h in probs]
    a = [jnp.where(d.live, d.fcum[:, hs(h)] - d.fcum_rows[h:h + 1, :] + d.li_rows[h:h + 1, :], NEG)
         for d, h in probs]
    m_t = [jnp.maximum(jnp.max(a_h, axis=1, keepdims=True), d.a_init[:, hs(h)])
           for (d, h), a_h in zip(probs, a)]
    p = [jnp.exp(a_h - m) * qk for a_h, m, qk in zip(a, m_t, qks)]
    pv = [jnp.dot(p_h.astype(BF16), d.v[:, hs(h)].astype(BF16), preferred_element_type=F32)
          for (d, h), p_h in zip(probs, p)]
    outs = []
    for (d, h), p_h, pv_h, m in zip(probs, p, pv, m_t):
        e0 = jnp.exp(d.a_init[:, hs(h)] - m)
        num = pv_h + e0 * d.qc[:, hs(h)]
        den = jnp.sum(p_h, axis=1, keepdims=True) + e0 * d.qn[:, hs(h)]
        outs.append(num / jnp.maximum(jnp.abs(den), jnp.exp(-m)))
    u_c = [_dot_tn(d.kw.astype(BF16), d.v.astype(BF16)) * heads for d in dirs]
    for i, (d, o_ref) in enumerate(zip(dirs, (of_ref, ob_ref))):
        o_ref[...] = jnp.concatenate(outs[i * ML_HEADS:(i + 1) * ML_HEADS], axis=1)
        m_new = jnp.maximum(d.ftot + d.m0, d.g_max)
        a_old = jnp.exp(d.ftot + d.m0 - m_new)
        a_new = jnp.exp(d.g_max - m_new)
        c_s[i] = a_old * d.c0 + a_new * u_c[i]
        n_s[i] = a_old * d.n0 + a_new * jnp.sum(d.kw, axis=0, keepdims=True)
        m_s[i] = m_new


def _scan_chunks(n_lat, n_ctx):
    fwd = lambda j: jnp.where(j < n_ctx, n_lat + j, j - n_ctx)
    bwd = lambda j: jnp.where(j < n_ctx, n_lat + n_ctx - 1 - j, n_lat + n_ctx - 1 - j)
    return fwd, bwd


def _mlstm_scan(z_ml, z_gt, gate_b, ctx_len):
    n_batch, seq, _ = z_ml.shape
    n_chunks = seq // ML_CHUNK
    n_ctx = ctx_len // ML_CHUNK
    fwd, bwd = _scan_chunks(n_chunks - n_ctx, n_ctx)
    low, upp, heads, expand, select = _chunk_consts(ML_CHUNK, ML_HEADS, (0, 4, 8, 12))
    bias = jnp.pad(gate_b.astype(F32).reshape(1, ML_GATES), ((0, 0), (0, GATE_PAD - ML_GATES)))
    full = lambda a: pl.BlockSpec(a.shape, lambda b, j: (0,) * a.ndim)
    return pl.pallas_call(
        _mlstm_kernel,
        grid=(n_batch, n_chunks),
        in_specs=[pl.BlockSpec((None, ML_CHUNK, ML_MAIN), lambda b, j: (b, fwd(j), 0)),
                  pl.BlockSpec((None, ML_CHUNK, ML_MAIN), lambda b, j: (b, bwd(j), 0)),
                  pl.BlockSpec((None, ML_CHUNK, GATE_PAD), lambda b, j: (b, fwd(j), 0)),
                  pl.BlockSpec((None, ML_CHUNK, GATE_PAD), lambda b, j: (b, bwd(j), 0)),
                  full(bias), full(low), full(upp), full(heads), full(expand), full(select)],
        out_specs=[pl.BlockSpec((None, ML_CHUNK, W_GROUP), lambda b, j: (b, fwd(j), 0)),
                   pl.BlockSpec((None, ML_CHUNK, W_GROUP), lambda b, j: (b, bwd(j), 0))],
        out_shape=[jax.ShapeDtypeStruct((n_batch, seq, W_GROUP), F32)] * 2,
        scratch_shapes=[pltpu.VMEM((2, W_GROUP, W_GROUP), F32), pltpu.VMEM((2, 1, W_GROUP), F32),
                        pltpu.VMEM((2, 1, W_GROUP), F32)],
        compiler_params=pltpu.CompilerParams(
            dimension_semantics=("parallel", "arbitrary"), vmem_limit_bytes=VMEM_LIMIT),
        name="mlstm_scan",
    )(z_ml, z_ml, z_gt, z_gt, bias, low, upp, heads, expand, select)


HALO = 8


def _short_conv_kernel(x_ref, prev_ref, next_ref, w_ref, b_ref, heads_ref, o_ref, *, n_lat_tiles, gdn):
    j = pl.program_id(1)
    n_tiles = pl.num_programs(1)
    x = x_ref[...]
    has_prev = jnp.logical_and(j != 0, j != n_lat_tiles)
    has_next = jnp.logical_and(j != n_lat_tiles - 1, j != n_tiles - 1)
    prev_row = jnp.where(has_prev, prev_ref[HALO - 1:HALO, :], 0.0)
    next_row = jnp.where(has_next, next_ref[0:1, :], 0.0)
    row = lax.broadcasted_iota(jnp.int32, x.shape, 0)
    x_prev = jnp.where(row == 0, prev_row, pltpu.roll(x, 1, 0))
    x_next = jnp.where(row == ROW_TILE - 1, next_row, pltpu.roll(x, ROW_TILE - 1, 0))
    y = x_prev * w_ref[0:1, :] + x * w_ref[1:2, :] + x_next * w_ref[2:3, :] + b_ref[...]
    if not gdn:
        o_ref[...] = y
        return
    y = jax.nn.silu(y)
    heads = heads_ref[...]
    for i in range(2):
        t = y[:, i * W_GROUP:(i + 1) * W_GROUP]
        o_ref[:, i * W_GROUP:(i + 1) * W_GROUP] = t * lax.rsqrt(_dot_sel(t * t, heads, 1, 3) + EPS)
    o_ref[:, 2 * W_GROUP:] = y[:, 2 * W_GROUP:]


def _head_mask():
    r = lax.broadcasted_iota(jnp.int32, (W_GROUP, W_GROUP), 0) // HG_DK
    c = lax.broadcasted_iota(jnp.int32, (W_GROUP, W_GROUP), 1) // HG_DK
    return (r == c).astype(F32)


def _short_conv_tiles(z, conv_w, conv_b, n_lat, gdn):
    n_batch, seq, _ = z.shape
    c = conv_w.shape[1]
    n_tiles = seq // ROW_TILE
    per = ROW_TILE // HALO
    n_halo = seq // HALO
    return pl.pallas_call(
        functools.partial(_short_conv_kernel, n_lat_tiles=n_lat // ROW_TILE, gdn=gdn),
        grid=(n_batch, n_tiles),
        in_specs=[pl.BlockSpec((None, ROW_TILE, c), lambda b, j: (b, j, 0)),
                  pl.BlockSpec((None, HALO, c), lambda b, j: (b, jnp.maximum(j * per - 1, 0), 0)),
                  pl.BlockSpec((None, HALO, c), lambda b, j: (b, jnp.minimum((j + 1) * per, n_halo - 1), 0)),
                  pl.BlockSpec((3, c), lambda b, j: (0, 0)),
                  pl.BlockSpec((1, c), lambda b, j: (0, 0)),
                  pl.BlockSpec((W_GROUP, W_GROUP), lambda b, j: (0, 0))],
        out_specs=pl.BlockSpec((None, ROW_TILE, c), lambda b, j: (b, j, 0)),
        out_shape=jax.ShapeDtypeStruct((n_batch, seq, c), F32),
        compiler_params=pltpu.CompilerParams(
            dimension_semantics=("parallel", "parallel"), vmem_limit_bytes=VMEM_LIMIT),
        name="gdn_conv" if gdn else "hyena_conv",
    )(z, z, z, conv_w.astype(F32), conv_b.astype(F32).reshape(1, c), _head_mask())


def _gd_consts():
    r = lax.broadcasted_iota(jnp.int32, (GD_CHUNK, GD_CHUNK), 0)
    c = lax.broadcasted_iota(jnp.int32, (GD_CHUNK, GD_CHUNK), 1)
    same = lambda n: (r // n) == (c // n)
    eye = (r == c).astype(F32)
    masks = [same(8).astype(F32)] + [(same(2 * b) & ~same(b)).astype(F32) for b in (8, 16, 32)]
    return eye, jnp.stack(masks)


def _unit_tri_inverses(ns, eye, masks):
    mm = lambda a, b: jnp.dot(a.astype(BF16), b.astype(BF16), preferred_element_type=F32)
    n8 = [n * masks[0] for n in ns]
    n2 = [mm(a, a) for a in n8]
    n4 = [mm(a, a) for a in n2]
    ts = [mm(eye - a, eye + b) for a, b in zip(n8, n2)]
    ts = [mm(t, eye + b) for t, b in zip(ts, n4)]
    for i in range(1, 4):
        lts = [mm(n * masks[i], t) for n, t in zip(ns, ts)]
        ts = [t - mm(t, lt) for t, lt in zip(ts, lts)]
    return ts


class _GdDir:
    def __init__(self, reverse, x_ref, g_ref, s_ref, prm_lane, prm_row, low, upp, ex_a, ex_b, sel_a):
        self.q = x_ref[:, 0:W_GROUP] * (GD_DH ** -0.5)
        self.k = x_ref[:, W_GROUP:2 * W_GROUP]
        self.v = x_ref[:, 2 * W_GROUP:3 * W_GROUP]
        gates = g_ref[...]
        col_sum, row_sum = (upp, low) if reverse else (low, upp)
        log_a = prm_lane[0:1, :] * jax.nn.softplus(_dot_sel(gates, ex_a, 1, 3) + prm_lane[1:2, :])
        self.beta = jax.nn.sigmoid(_dot_sel(gates, ex_b, 1, 3))
        log_a_rows = prm_row[0] * jax.nn.softplus(_dot_sel(sel_a, gates, 0, 3, nt=True) + prm_row[1])
        self.g = _dot_sel(col_sum, log_a, 0, 3)
        self.g_rows = _dot_sel(log_a_rows, row_sum, 1, 3)
        last = 0 if reverse else GD_CHUNK - 1
        self.g_last = self.g[last:last + 1, :]
        self.eg = jnp.exp(self.g)
        self.k_bf = self.k.astype(BF16)
        self.kb = self.k * self.beta
        self.s0 = s_ref[...]
        rr = lax.broadcasted_iota(jnp.int32, (GD_CHUNK, GD_CHUNK), 0)
        cc = lax.broadcasted_iota(jnp.int32, (GD_CHUNK, GD_CHUNK), 1)
        self.live = (cc >= rr) if reverse else (cc <= rr)
        self.strict = (cc > rr) if reverse else (cc < rr)


def _gdn_kernel(xf_ref, xb_ref, gf_ref, gb_ref, prm_lane_ref, prm_row_ref, low_ref, upp_ref, heads_ref,
                ex_ref, sel_ref, eye_ref, masks_ref, of_ref, ob_ref, s_s):
    @pl.when(pl.program_id(1) == 0)
    def _():
        s_s[...] = jnp.zeros_like(s_s)

    low, upp, heads, eye, masks = low_ref[...], upp_ref[...], heads_ref[...], eye_ref[...], masks_ref[...]
    dirs = [_GdDir(False, xf_ref, gf_ref, s_s.at[0], prm_lane_ref[0], prm_row_ref[0], low, upp,
                   ex_ref[0], ex_ref[1], sel_ref[0]),
            _GdDir(True, xb_ref, gb_ref, s_s.at[1], prm_lane_ref[1], prm_row_ref[1], low, upp,
                   ex_ref[2], ex_ref[3], sel_ref[2])]
    lane_head = lax.broadcasted_iota(jnp.int32, (GD_CHUNK, W_GROUP), 1) // GD_DH
    probs = [(d, h) for d in dirs for h in range(GD_HEADS)]
    only = lambda h, t: jnp.where(lane_head == h, t, 0.0).astype(BF16)
    hs = lambda h: slice(h * GD_DH, (h + 1) * GD_DH)
    decay = [jnp.exp(jnp.where(d.live, d.g[:, hs(h)] - d.g_rows[h:h + 1, :], NEG)) for d, h in probs]
    kks = [_dot_nt(only(h, d.kb), d.k_bf) for d, h in probs]
    ns = [jnp.where(d.strict, kk * dc, 0.0) for (d, h), kk, dc in zip(probs, kks, decay)]
    t_inv = [t.astype(BF16) for t in _unit_tri_inverses(ns, eye, masks)]
    us = [jnp.dot(t, only(h, d.v * d.beta), preferred_element_type=F32) for (d, h), t in zip(probs, t_inv)]
    ws = [jnp.dot(t, only(h, d.kb * d.eg), preferred_element_type=F32) for (d, h), t in zip(probs, t_inv)]
    qks = [(_dot_nt(only(h, d.q), d.k_bf) * dc).astype(BF16) for (d, h), dc in zip(probs, decay)]
    v_new, o = [], []
    for i, d in enumerate(dirs):
        sl = slice(i * GD_HEADS, (i + 1) * GD_HEADS)
        s_bf = d.s0.astype(BF16)
        v_new.append(sum(us[sl]) - jnp.dot(sum(ws[sl]).astype(BF16), s_bf, preferred_element_type=F32))
        o.append(jnp.dot((d.q * d.eg).astype(BF16), s_bf, preferred_element_type=F32))
    intra = [jnp.dot(qk, only(h, v_new[i // GD_HEADS]), preferred_element_type=F32)
             for i, ((d, h), qk) in enumerate(zip(probs, qks))]
    for i, (d, o_ref) in enumerate(zip(dirs, (of_ref, ob_ref))):
        o_ref[...] = o[i] + sum(intra[i * GD_HEADS:(i + 1) * GD_HEADS])
        k_dec = d.k * jnp.exp(d.g_last - d.g)
        s_s[i] = jnp.exp(d.g_last) * d.s0 + _dot_tn(k_dec.astype(BF16), v_new[i].astype(BF16)) * heads


def _gdn_scan(qkv, z_gt, a_log, dt_bias, ctx_len):
    n_batch, seq, _ = qkv.shape
    n_chunks = seq // GD_CHUNK
    n_ctx = ctx_len // GD_CHUNK
    fwd, bwd = _scan_chunks(n_chunks - n_ctx, n_ctx)
    g0 = ML_GATES
    low, upp, heads, expand, select = _chunk_consts(GD_CHUNK, GD_HEADS, (g0, g0 + 4, g0 + 8, g0 + 12))
    eye, masks = _gd_consts()
    prm = jnp.stack([-jnp.exp(a_log.astype(F32)), dt_bias.astype(F32)], 1)
    prm_lane = jnp.repeat(prm, GD_DH, axis=2)
    prm_row = jnp.broadcast_to(jnp.pad(prm, ((0, 0), (0, 0), (0, 8 - GD_HEADS)))[..., None],
                               (2, 2, 8, GD_CHUNK))
    full = lambda a: pl.BlockSpec(a.shape, lambda b, j: (0,) * a.ndim)
    return pl.pallas_call(
        _gdn_kernel,
        grid=(n_batch, n_chunks),
        in_specs=[pl.BlockSpec((None, GD_CHUNK, 3 * W_GROUP), lambda b, j: (b, fwd(j), 0)),
                  pl.BlockSpec((None, GD_CHUNK, 3 * W_GROUP), lambda b, j: (b, bwd(j), 0)),
                  pl.BlockSpec((None, GD_CHUNK, GATE_PAD), lambda b, j: (b, fwd(j), 0)),
                  pl.BlockSpec((None, GD_CHUNK, GATE_PAD), lambda b, j: (b, bwd(j), 0)),
                  full(prm_lane), full(prm_row), full(low), full(upp), full(heads), full(expand),
                  full(select), full(eye), full(masks)],
        out_specs=[pl.BlockSpec((None, GD_CHUNK, W_GROUP), lambda b, j: (b, fwd(j), 0)),
                   pl.BlockSpec((None, GD_CHUNK, W_GROUP), lambda b, j: (b, bwd(j), 0))],
        out_shape=[jax.ShapeDtypeStruct((n_batch, seq, W_GROUP), F32)] * 2,
        scratch_shapes=[pltpu.VMEM((2, W_GROUP, W_GROUP), F32)],
        compiler_params=pltpu.CompilerParams(
            dimension_semantics=("parallel", "arbitrary"), vmem_limit_bytes=VMEM_LIMIT),
        name="gdn_scan",
    )(qkv, qkv, z_gt, z_gt, prm_lane, prm_row, low, upp, heads, expand, select, eye, masks)


HY_FILT_TILE = 512
DFT_TILE = 512


DFT_ROWS = 64


def _dft_table_kernel(ac_ref, as_ref, bc_ref, bs_ref, re_ref, im_ref, imt_ref):
    a = pl.program_id(0)
    ac, sa = ac_ref[pl.ds(a, 1), :], as_ref[pl.ds(a, 1), :]
    bc, sb = bc_ref[...], bs_ref[...]
    cos = ac * bc - sa * sb
    nsin = -(sa * bc + ac * sb)
    i = lax.broadcasted_iota(jnp.int32, cos.shape, 0) + a * DFT_ROWS
    j = lax.broadcasted_iota(jnp.int32, cos.shape, 1)
    sign = lambda n: jnp.where(n % 2 == 0, 1.0, -1.0)
    re_ref[...] = cos.astype(BF16)
    im_ref[...] = jnp.where(i == 0, sign(j), nsin).astype(BF16)
    imt_ref[...] = jnp.where(j == 0, sign(i), nsin).astype(BF16)


def _dft_matrices(length):
    n = 2 * length
    j = jnp.arange(length, dtype=jnp.int32)[None, :]
    coarse = (jnp.arange(length // DFT_ROWS, dtype=jnp.int32)[:, None] * DFT_ROWS * j) % n
    fine = (jnp.arange(DFT_ROWS, dtype=jnp.int32)[:, None] * j) % n
    w = 2.0 * math.pi / n
    tabs = [jnp.cos(coarse.astype(F32) * w), jnp.sin(coarse.astype(F32) * w),
            jnp.cos(fine.astype(F32) * w), jnp.sin(fine.astype(F32) * w)]
    full = lambda a: pl.BlockSpec(a.shape, lambda i: (0, 0))
    out = pl.BlockSpec((DFT_ROWS, length), lambda i: (i, 0))
    return pl.pallas_call(
        _dft_table_kernel,
        grid=(length // DFT_ROWS,),
        in_specs=[full(t) for t in tabs],
        out_specs=[out] * 3,
        out_shape=[jax.ShapeDtypeStruct((length, length), BF16)] * 3,
        compiler_params=pltpu.CompilerParams(dimension_semantics=("parallel",)),
        name="dft_table",
    )(*tabs)


def _hy_filter_kernel(feat_ref, w1_ref, b1_ref, w2_ref, b2_ref, fr_ref, w3_ref, win_ref, filt_ref, ss_ref):
    i = pl.program_id(0)
    hdn = jnp.sin(fr_ref[0:1, :] * (_dot_hi(feat_ref[...], w1_ref[...]) + b1_ref[...]))
    hdn = jnp.sin(fr_ref[1:2, :] * (_dot_hi(hdn, w2_ref[...]) + b2_ref[...]))
    filt = _dot_hi(hdn, w3_ref[...]) * win_ref[...]
    half = filt.shape[1] // 2
    row = lax.broadcasted_iota(jnp.int32, filt.shape, 0) + i * filt.shape[0]
    col = lax.broadcasted_iota(jnp.int32, filt.shape, 1)
    filt = jnp.where(jnp.logical_and(row == 0, col >= half), 0.0, filt)
    filt_ref[...] = filt.astype(BF16)
    sq = jnp.sum(filt * filt, axis=0, keepdims=True)

    @pl.when(i == 0)
    def _():
        ss_ref[...] = jnp.zeros_like(ss_ref)

    ss_ref[...] += sq[:, :half] + sq[:, half:]


def _hyena_filters(length, w1, b1, w2, b2, freq, w3):
    tile = min(HY_FILT_TILE, length)
    t = jnp.linspace(0.0, 1.0, length, dtype=F32)[:, None]
    pos = jnp.arange(length, dtype=F32)[:, None]
    band = jnp.linspace(1e-4, HY_BANDS - 1, HY_BANDS, dtype=F32)[None, :]
    ang = 2.0 * math.pi * pos * band / length
    feats = jnp.concatenate([t, jnp.cos(ang), -jnp.sin(ang)], -1)
    deltas = jnp.abs(jnp.linspace(math.log(HY_TARGET) / HY_SLOW_DECAY,
                                  math.log(HY_TARGET) / HY_FAST_DECAY, W_GROUP, dtype=F32))
    window = jnp.tile(jnp.exp(-t * deltas), (1, 2 * HY_ORDER))
    n_out = w3.shape[1]
    emb, ffn = w1.shape
    full = lambda a: pl.BlockSpec(a.shape, lambda i: (0,) * a.ndim)
    args = [w1.astype(F32), b1.astype(F32).reshape(1, ffn), w2.astype(F32), b2.astype(F32).reshape(1, ffn),
            freq.astype(F32), w3.astype(F32)]
    filt, ss = pl.pallas_call(
        _hy_filter_kernel,
        grid=(length // tile,),
        in_specs=[pl.BlockSpec((tile, emb), lambda i: (i, 0))] + [full(a) for a in args] +
                 [pl.BlockSpec((tile, n_out), lambda i: (i, 0))],
        out_specs=[pl.BlockSpec((tile, n_out), lambda i: (i, 0)),
                   pl.BlockSpec((1, n_out // 2), lambda i: (0, 0))],
        out_shape=[jax.ShapeDtypeStruct((length, n_out), BF16),
                   jax.ShapeDtypeStruct((1, n_out // 2), F32)],
        compiler_params=pltpu.CompilerParams(
            dimension_semantics=("arbitrary",), vmem_limit_bytes=VMEM_LIMIT),
        name="hyena_filter",
    )(feats, *args, window)
    return filt, ss


def _spec_mul(yr, yi, kr, ki, first):
    row0 = jnp.logical_and(first, lax.broadcasted_iota(jnp.int32, yr.shape, 0) == 0)
    zr = yr * kr - jnp.where(row0, 0.0, yi * ki)
    zi = jnp.where(row0, yi * ki, yr * ki + yi * kr)
    return zr, zi


def _dft_filter_kernel(wre_ref, wim_ref, x_ref, ss_ref, kre_ref, kim_ref):
    x = x_ref[...]
    yr = jnp.dot(wre_ref[...], x, preferred_element_type=F32)
    yi = jnp.dot(wim_ref[...], x, preferred_element_type=F32)
    half = yr.shape[1] // 2
    row0 = jnp.logical_and(pl.program_id(0) == 0,
                           lax.broadcasted_iota(jnp.int32, (yr.shape[0], half), 0) == 0)
    n = 2 * x.shape[0]
    scale = lax.rsqrt(ss_ref[...] + EPS) * jnp.where(row0, 1.0 / n, 2.0 / n)
    kre_ref[...] = (yr[:, :half] + yr[:, half:]) * scale
    kim_ref[...] = (yi[:, :half] + jnp.where(row0, yi[:, half:], -yi[:, half:])) * scale


def _dft_filter(f_re, f_im, filt, ss):
    length, n = filt.shape
    tile = min(DFT_TILE, length)
    wspec = pl.BlockSpec((tile, length), lambda i: (i, 0))
    return pl.pallas_call(
        _dft_filter_kernel,
        grid=(length // tile,),
        in_specs=[wspec, wspec, pl.BlockSpec((length, n), lambda i: (0, 0)),
                  pl.BlockSpec((1, n // 2), lambda i: (0, 0))],
        out_specs=[pl.BlockSpec((tile, n // 2), lambda i: (i, 0))] * 2,
        out_shape=[jax.ShapeDtypeStruct((length, n // 2), F32)] * 2,
        compiler_params=pltpu.CompilerParams(
            dimension_semantics=("parallel",), vmem_limit_bytes=VMEM_LIMIT),
        name="hyena_filter_dft",
    )(f_re, f_im, filt, ss)


def _dft_fwd_kernel(wre_ref, wim_ref, y_ref, kre_ref, kim_ref, zre_ref, zim_ref):
    y = y_ref[...].astype(BF16)
    yr = jnp.dot(wre_ref[...], y, preferred_element_type=F32)
    yi = jnp.dot(wim_ref[...], y, preferred_element_type=F32)
    zr, zi = _spec_mul(yr, yi, kre_ref[...], kim_ref[...], pl.program_id(0) == 0)
    zre_ref[...] = zr.astype(BF16)
    zim_ref[...] = zi.astype(BF16)


def _dft_fwd(f_re, f_im, y, k_re, k_im, order, row0):
    n_batch = y.shape[0]
    length = f_re.shape[0]
    tile = min(DFT_TILE, length)
    wspec = pl.BlockSpec((tile, length), lambda i, b: (i, 0))
    kspec = pl.BlockSpec((tile, W_GROUP), lambda i, b: (i, order))
    zspec = pl.BlockSpec((None, tile, W_GROUP), lambda i, b: (b, i, 0))
    return pl.pallas_call(
        _dft_fwd_kernel,
        grid=(length // tile, n_batch),
        in_specs=[wspec, wspec,
                  pl.BlockSpec((None, length, W_GROUP), lambda i, b: (b, row0 // length, 0)),
                  kspec, kspec],
        out_specs=[zspec, zspec],
        out_shape=[jax.ShapeDtypeStruct((n_batch, length, W_GROUP), BF16)] * 2,
        compiler_params=pltpu.CompilerParams(
            dimension_semantics=("parallel", "arbitrary"), vmem_limit_bytes=VMEM_LIMIT),
        name="hyena_dft_fwd",
    )(f_re, f_im, y, k_re, k_im)


def _dft_inv_kernel(wre_ref, wim_ref, zre_ref, zim_ref, y_ref, gate_ref, bias_ref, o_ref, obf_ref):
    conv = (jnp.dot(wre_ref[...], zre_ref[...], preferred_element_type=F32) +
            jnp.dot(wim_ref[...], zim_ref[...], preferred_element_type=F32))
    out = gate_ref[...] * (conv + bias_ref[...] * y_ref[...])
    o_ref[...] = out
    obf_ref[...] = out.astype(BF16)


def _dft_inv(i_re, i_im, z_re, z_im, y_src, gate_src, bias):
    n_batch, length, _ = z_re.shape
    tile = min(DFT_TILE, length)
    wspec = pl.BlockSpec((tile, length), lambda i, b: (i, 0))
    zspec = pl.BlockSpec((None, length, W_GROUP), lambda i, b: (b, 0, 0))
    ospec = pl.BlockSpec((None, tile, W_GROUP), lambda i, b: (b, i, 0))
    view = lambda src: pl.BlockSpec((None, tile, W_GROUP), lambda i, b: (b, src[1] // tile + i, src[2]))
    return pl.pallas_call(
        _dft_inv_kernel,
        grid=(length // tile, n_batch),
        in_specs=[wspec, wspec, zspec, zspec, view(y_src), view(gate_src),
                  pl.BlockSpec((1, W_GROUP), lambda i, b: (0, 0))],
        out_specs=[ospec, ospec],
        out_shape=[jax.ShapeDtypeStruct((n_batch, length, W_GROUP), F32),
                   jax.ShapeDtypeStruct((n_batch, length, W_GROUP), BF16)],
        compiler_params=pltpu.CompilerParams(
            dimension_semantics=("parallel", "arbitrary"), vmem_limit_bytes=VMEM_LIMIT),
        name="hyena_dft_inv",
    )(i_re, i_im, z_re, z_im, y_src[0], gate_src[0], bias.astype(F32).reshape(1, W_GROUP))


def _hyena_segment(u, row0, length, dft, filt_params, bias):
    f_re, f_im, i_im = dft
    i_re = f_re
    filt, ss = _hyena_filters(length, *filt_params)
    k_re, k_im = _dft_filter(f_re, f_im, filt, ss)
    z_re, z_im = _dft_fwd(f_re, f_im, u, k_re, k_im, 0, row0)
    y, y_bf = _dft_inv(i_re, i_im, z_re, z_im, (u, row0, 0), (u, row0, 1), bias[0])
    z_re, z_im = _dft_fwd(f_re, f_im, y_bf, k_re, k_im, 1, 0)
    y, _ = _dft_inv(i_re, i_im, z_re, z_im, (y, 0, 0), (u, row0, 2), bias[1])
    return y


GT_COLS = 4


def _grid_transpose_kernel(x_ref, c_ref, o_ref, *, n_major, n_lat_steps):
    j = pl.program_id(1)
    for half in range(2):
        @pl.when(jnp.logical_and(j < n_lat_steps, j % 2 == half))
        def _():
            for i in range(GT_COLS):
                o_ref[i * n_major:(i + 1) * n_major, :] = x_ref[:, half * GT_COLS + i, :]

    @pl.when(j >= n_lat_steps)
    def _():
        o_ref[...] = c_ref[...]


def _grid_transpose(t, n_lat, n_major):
    n_batch, seq, w = t.shape
    n_minor = n_lat // n_major
    rows = GT_COLS * n_major
    n_lat_steps = n_minor // GT_COLS
    view = t.reshape(n_batch, seq // n_minor, n_minor, w)
    return pl.pallas_call(
        functools.partial(_grid_transpose_kernel, n_major=n_major, n_lat_steps=n_lat_steps),
        grid=(n_batch, seq // rows),
        in_specs=[pl.BlockSpec((None, n_major, 2 * GT_COLS, w),
                               lambda b, j: (b, 0, jnp.minimum(j, n_lat_steps - 1) // 2, 0)),
                  pl.BlockSpec((None, rows, w), lambda b, j: (b, jnp.maximum(j, n_lat_steps), 0))],
        out_specs=pl.BlockSpec((None, rows, w), lambda b, j: (b, j, 0)),
        out_shape=jax.ShapeDtypeStruct((n_batch, seq, w), t.dtype),
        compiler_params=pltpu.CompilerParams(
            dimension_semantics=("parallel", "arbitrary"), vmem_limit_bytes=VMEM_LIMIT),
        name="grid_transpose",
    )(view, t)


def _permute_w_in(w_in):
    o_ml = HG_COLS
    o_hy = o_ml + ML_MAIN + ML_GATES
    o_gd = o_hy + HY_COLS
    parts = [w_in[:, :HG_COLS], w_in[:, o_ml:o_ml + ML_MAIN], w_in[:, o_hy:o_gd],
             w_in[:, o_gd:o_gd + GD_MAIN], w_in[:, o_ml + ML_MAIN:o_hy],
             w_in[:, o_gd + GD_MAIN:],
             jnp.zeros((w_in.shape[0], GATE_PAD - ML_GATES - GD_GATES), w_in.dtype)]
    return jnp.concatenate(parts, 1).astype(BF16)


IN_SPLITS = (HG_COLS, ML_MAIN, HY_COLS, GD_MAIN, GATE_PAD)


def kernel(x, c, ctx, c_ctx, mod_w, mod_b, norm1_g, norm2_g, w_in, w_out, hg_lb_logits, hg_norm_g,
           ml_gate_b, ml_norm_g, hy_conv_w, hy_conv_b, hy_w1, hy_b1, hy_w2, hy_b2, hy_freq, hy_w3,
           hy_bias, gd_conv_w, gd_a_log, gd_dt_bias, gd_norm_g, router_w, router_b, exp_w1, exp_b1,
           exp_w2, exp_b2, final_g):
    n_batch, seq, d = x.shape
    ctx_len = ctx.shape[1]
    depth = mod_w.shape[0]
    rows = seq // GRID_W
    p = jax.nn.softmax(hg_lb_logits.astype(F32), axis=0)
    lower_bounds = jnp.cumsum(p, 0) - p[0]
    s_all = jnp.concatenate([jax.nn.silu(c), jax.nn.silu(c_ctx)[None]], 0)
    xs = jnp.concatenate([x, ctx], 1)
    colmajor = lambda t: _grid_transpose(t, seq, rows)
    rowmajor = lambda t: _grid_transpose(t, seq, GRID_W)
    dft_lat, dft_ctx = _dft_matrices(seq), _dft_matrices(ctx_len)
    for l in range(depth):
        mod = (s_all @ mod_w[l] + mod_b[l]).reshape(n_batch + 1, 6, 1, d)
        sh1, sc1, g1, sh2, sc2, g2 = (mod[:, i] for i in range(6))
        z_hg, z_ml, z_hy, z_gd, z_gt = _in_proj(xs, norm1_g[l], sc1, sh1,
                                                _permute_w_in(w_in[l]), IN_SPLITS)
        hg_f, hg_b = _hgrn2_scan(z_hg, lower_bounds[l])
        ml_f, ml_b = _mlstm_scan(colmajor(z_ml), colmajor(z_gt), ml_gate_b[l], ctx_len)
        filt = (hy_w1[l], hy_b1[l], hy_w2[l], hy_b2[l], hy_freq[l], hy_w3[l])
        u_hy = _short_conv_tiles(z_hy, hy_conv_w[l], hy_conv_b[l], seq, False)
        c_lat = _hyena_segment(u_hy, 0, seq, dft_lat, filt, hy_bias[l])
        if l == depth - 1:
            c_ctx = jnp.zeros((n_batch, ctx_len, W_GROUP), F32)
        else:
            c_ctx = _hyena_segment(u_hy, seq, ctx_len, dft_ctx, filt, hy_bias[l])
        qkv = _short_conv_tiles(z_gd, gd_conv_w[l], jnp.zeros((3 * W_GROUP,), F32), seq, True)
        gd_f, gd_b = _gdn_scan(qkv, z_gt, gd_a_log[l], gd_dt_bias[l], ctx_len)
        mixers = [("silu", hg_f, hg_b, (z_hg, 4)),
                  ("sigmoid", rowmajor(ml_f), rowmajor(ml_b), (z_ml, 3)),
                  ("final", jnp.concatenate([c_lat, c_ctx], 1)),
                  ("silu", gd_f, gd_b, (z_gd, 3))]
        gains = jnp.stack([hg_norm_g[l], ml_norm_g[l], jnp.ones_like(ml_norm_g[l]), gd_norm_g[l]]).astype(F32)
        xs = _out_proj(xs, mixers, gains, g1, w_out[l].astype(BF16))
        xs = _moe_layer(xs, norm2_g[l], sc2, sh2, g2, router_w[l], router_b[l],
                        l, exp_w1, exp_b1[l], exp_w2, exp_b2[l])
    return _final_norm(xs, final_g, seq)
```

```python
import functools
import math

import jax
import jax.numpy as jnp
from jax import lax
from jax.experimental import pallas as pl
from jax.experimental.pallas import tpu as pltpu

F32 = jnp.float32
BF16 = jnp.bfloat16

D_MODEL = 1024
GRID_W = 64
N_MIXERS = 4
W_GROUP = D_MODEL // N_MIXERS
HG_HEADS = 4
HG_DK = W_GROUP // HG_HEADS
HG_CHUNK = 16
ML_HEADS = 4
ML_DH = W_GROUP // ML_HEADS
ML_CHUNK = 64
HY_ORDER = 2
HY_BANDS = 8
HY_FAST_DECAY = 0.3
HY_SLOW_DECAY = 1.5
HY_TARGET = 1e-2
GD_HEADS = 4
GD_DH = W_GROUP // GD_HEADS
GD_CHUNK = 64
N_EXPERTS = 32
TOP_K = 4
SWIGLU_LIMIT = 7.0
SWIGLU_ALPHA = 1.702
EPS = 1e-6
NEG = -1e30
HG_COLS = 5 * W_GROUP
ML_MAIN = 4 * W_GROUP
ML_GATES = 4 * ML_HEADS
HY_COLS = 3 * W_GROUP
GD_MAIN = 4 * W_GROUP
GD_GATES = 4 * GD_HEADS
GATE_PAD = 128

ROW_TILE = 256
MOE_TILE = 256
VMEM_LIMIT = 56 * 1024 * 1024


def _norm_mod(x, gain, sc, sh):
    y = x * lax.rsqrt(jnp.mean(x * x, -1, keepdims=True) + EPS)
    return y * gain * (1.0 + sc) + sh


def _in_proj_kernel(x_ref, g_ref, sc_ref, sh_ref, w_ref, *out_refs, splits):
    h = _norm_mod(x_ref[...], g_ref[...], sc_ref[...], sh_ref[...]).astype(BF16)
    off = 0
    for o_ref, n in zip(out_refs, splits):
        o_ref[...] = jnp.dot(h, w_ref[:, off:off + n], preferred_element_type=F32)
        off += n


def _mod_index_map(n_batch, n_tiles):
    return lambda b, j: (jnp.where(j == n_tiles - 1, n_batch, b), 0, 0)


def _in_proj(xs, gain, sc_tab, sh_tab, w, splits):
    n_batch, seq, d = xs.shape
    n = w.shape[1]
    grid = (n_batch, seq // ROW_TILE)
    mod_spec = pl.BlockSpec((None, 1, d), _mod_index_map(n_batch, seq // ROW_TILE))
    return pl.pallas_call(
        functools.partial(_in_proj_kernel, splits=splits),
        grid=grid,
        in_specs=[
            pl.BlockSpec((None, ROW_TILE, d), lambda b, j: (b, j, 0)),
            pl.BlockSpec((1, d), lambda b, j: (0, 0)),
            mod_spec, mod_spec,
            pl.BlockSpec((d, n), lambda b, j: (0, 0)),
        ],
        out_specs=[pl.BlockSpec((None, ROW_TILE, s), lambda b, j: (b, j, 0)) for s in splits],
        out_shape=[jax.ShapeDtypeStruct((n_batch, seq, s), F32) for s in splits],
        compiler_params=pltpu.CompilerParams(
            dimension_semantics=("parallel", "parallel"), vmem_limit_bytes=VMEM_LIMIT),
        name="in_proj",
    )(xs, gain.reshape(1, d), sc_tab, sh_tab, w)


def _moe_route_kernel(x_ref, g_ref, sc_ref, sh_ref, rw_ref, rb_ref, tri_ref, h_ref, idx_ref, gate_ref,
                      rank_ref, cnt_ref, carry):
    @pl.when(jnp.logical_and(pl.program_id(0) == 0, pl.program_id(1) == 0))
    def _():
        carry[...] = jnp.zeros_like(carry)

    h = _norm_mod(x_ref[...], g_ref[...], sc_ref[...], sh_ref[...])
    h_ref[...] = h
    work = _dot_hi(h, rw_ref[...]) + rb_ref[...]
    lane = lax.broadcasted_iota(jnp.int32, work.shape, 1)
    vals, hits = [], []
    for _ in range(TOP_K):
        m = jnp.max(work, axis=-1, keepdims=True)
        first = jnp.min(jnp.where(work == m, lane, GATE_PAD), axis=-1, keepdims=True)
        hit = lane == first
        vals.append(m)
        hits.append(hit)
        work = jnp.where(hit, -jnp.inf, work)
    exps = [jnp.exp(v - vals[0]) for v in vals]
    total = sum(exps)
    chosen = sum(hit.astype(F32) for hit in hits)
    before = jnp.dot(tri_ref[...], chosen.astype(BF16), preferred_element_type=F32) + carry[...]
    idx = jnp.zeros(work.shape, jnp.int32)
    gate = jnp.zeros(work.shape, F32)
    rank = jnp.zeros(work.shape, jnp.int32)
    for k in range(TOP_K):
        idx = jnp.where(lane == k, jnp.sum(jnp.where(hits[k], lane, 0), axis=-1, keepdims=True), idx)
        gate = jnp.where(lane == k, exps[k] / total, gate)
        r_k = jnp.sum(jnp.where(hits[k], before, 0.0), axis=-1, keepdims=True)
        rank = jnp.where(lane == k, r_k.astype(jnp.int32), rank)
    idx_ref[...] = idx
    gate_ref[...] = gate
    rank_ref[...] = rank
    carry[...] += jnp.sum(chosen, axis=0, keepdims=True)
    cnt_ref[...] = carry[...]


def _moe_route(xs, gain, sc_tab, sh_tab, router_w, router_b):
    n_batch, seq, d = xs.shape
    pad = GATE_PAD - N_EXPERTS
    rw = jnp.pad(router_w.astype(F32), ((0, 0), (0, pad)))
    rb = jnp.pad(router_b.astype(F32), (0, pad), constant_values=NEG).reshape(1, GATE_PAD)
    r = lax.broadcasted_iota(jnp.int32, (ROW_TILE, ROW_TILE), 0)
    c = lax.broadcasted_iota(jnp.int32, (ROW_TILE, ROW_TILE), 1)
    tri = (c < r).astype(BF16)
    mod_spec = pl.BlockSpec((None, 1, d), _mod_index_map(n_batch, seq // ROW_TILE))
    tok = lambda w: pl.BlockSpec((None, ROW_TILE, w), lambda b, j: (b, j, 0))
    return pl.pallas_call(
        _moe_route_kernel,
        grid=(n_batch, seq // ROW_TILE),
        in_specs=[
            tok(d),
            pl.BlockSpec((1, d), lambda b, j: (0, 0)),
            mod_spec, mod_spec,
            pl.BlockSpec((d, GATE_PAD), lambda b, j: (0, 0)),
            pl.BlockSpec((1, GATE_PAD), lambda b, j: (0, 0)),
            pl.BlockSpec((ROW_TILE, ROW_TILE), lambda b, j: (0, 0)),
        ],
        out_specs=[tok(d), tok(GATE_PAD), tok(GATE_PAD), tok(GATE_PAD),
                   pl.BlockSpec((1, GATE_PAD), lambda b, j: (0, 0))],
        out_shape=[jax.ShapeDtypeStruct((n_batch, seq, d), F32),
                   jax.ShapeDtypeStruct((n_batch, seq, GATE_PAD), jnp.int32),
                   jax.ShapeDtypeStruct((n_batch, seq, GATE_PAD), F32),
                   jax.ShapeDtypeStruct((n_batch, seq, GATE_PAD), jnp.int32),
                   jax.ShapeDtypeStruct((1, GATE_PAD), F32)],
        scratch_shapes=[pltpu.VMEM((1, GATE_PAD), F32)],
        compiler_params=pltpu.CompilerParams(
            dimension_semantics=("arbitrary", "arbitrary"), vmem_limit_bytes=VMEM_LIMIT),
        name="moe_route",
    )(xs, gain.reshape(1, d), sc_tab, sh_tab, rw, rb, tri)


def _out_proj_kernel(*refs, kinds):
    x_ref, refs = refs[0], refs[1:]
    gains_ref, g_ref, heads_ref, w_ref, o_ref = refs[-5:]
    heads = heads_ref[...] * (1.0 / HG_DK)
    acc = jnp.zeros(o_ref.shape, F32)
    pos = 0
    for i, kind in enumerate(kinds):
        if kind == "final":
            m = refs[pos][...]
            pos += 1
        else:
            o = refs[pos][...] + refs[pos + 1][...]
            gate = refs[pos + 2][...]
            pos += 3
            ms = _dot_sel(o * o, heads, 1, 3)
            o = o * lax.rsqrt(ms + EPS) * gains_ref[i:i + 1, :]
            m = o * (jax.nn.silu(gate) if kind == "silu" else jax.nn.sigmoid(gate))
        acc += jnp.dot(m.astype(BF16), w_ref[i * W_GROUP:(i + 1) * W_GROUP, :],
                       preferred_element_type=F32)
    o_ref[...] = x_ref[...] + g_ref[...] * acc


def _out_proj(xs, mixers, gains, gate_tab, w_out):
    n_batch, seq, d = xs.shape
    row = lambda w, cb=0: pl.BlockSpec((None, ROW_TILE, w), lambda b, j: (b, j, cb))
    args, specs, kinds = [], [], []
    for m in mixers:
        kinds.append(m[0])
        if m[0] == "final":
            args.append(m[1])
            specs.append(row(W_GROUP))
        else:
            src, cb = m[3]
            args += [m[1], m[2], src]
            specs += [row(W_GROUP), row(W_GROUP), row(W_GROUP, cb)]
    heads = _head_mask()
    return pl.pallas_call(
        functools.partial(_out_proj_kernel, kinds=tuple(kinds)),
        grid=(n_batch, seq // ROW_TILE),
        in_specs=[row(d)] + specs + [
            pl.BlockSpec((N_MIXERS, W_GROUP), lambda b, j: (0, 0)),
            pl.BlockSpec((None, 1, d), _mod_index_map(n_batch, seq // ROW_TILE)),
            pl.BlockSpec((W_GROUP, W_GROUP), lambda b, j: (0, 0)),
            pl.BlockSpec((d, d), lambda b, j: (0, 0)),
        ],
        out_specs=row(d),
        out_shape=jax.ShapeDtypeStruct((n_batch, seq, d), F32),
        compiler_params=pltpu.CompilerParams(
            dimension_semantics=("parallel", "parallel"), vmem_limit_bytes=VMEM_LIMIT),
        name="out_proj",
    )(xs, *args, gains, gate_tab, heads, w_out)


def _row_copy(src, src_row, dst, dst_row, sem):
    return pltpu.make_async_copy(src.at[pl.ds(src_row, 1), :], dst.at[pl.ds(dst_row, 1), :], sem)


def _dispatch_kernel(tail_ref, dest_ref, h_ref, slots_hbm, zeros_v, sem, zero_sem):
    def fill(start):
        start = pl.multiple_of(start, MOE_TILE)
        return pltpu.make_async_copy(zeros_v, slots_hbm.at[pl.ds(start, MOE_TILE), :], zero_sem)

    def each_unused_tile(fn):
        first = tail_ref[N_EXPERTS] // MOE_TILE
        lax.fori_loop(first, slots_hbm.shape[0] // MOE_TILE, lambda t, c: (fn(fill(t * MOE_TILE)), c)[1], 0)

    @pl.when(pl.program_id(0) == 0)
    def _():
        zeros_v[...] = jnp.zeros_like(zeros_v)
        for e in range(N_EXPERTS):
            @pl.when(tail_ref[e] >= 0)
            def _():
                fill(tail_ref[e]).start()
        each_unused_tile(lambda copy: copy.start())
        for e in range(N_EXPERTS):
            @pl.when(tail_ref[e] >= 0)
            def _():
                fill(tail_ref[e]).wait()
        each_unused_tile(lambda copy: copy.wait())

    def issue(r, carry):
        for k in range(TOP_K):
            _row_copy(h_ref, r, slots_hbm, dest_ref[0, r * TOP_K + k], sem).start(priority=k % 2)
        return carry

    lax.fori_loop(0, ROW_TILE, issue, 0, unroll=4)
    for k in range(TOP_K):
        pltpu.make_async_copy(h_ref, slots_hbm.at[pl.ds(0, ROW_TILE), :], sem).wait()


def _moe_dispatch(h, dest, tail, n_slots):
    t_count, d = h.shape
    return pl.pallas_call(
        _dispatch_kernel,
        grid_spec=pltpu.PrefetchScalarGridSpec(
            num_scalar_prefetch=1,
            grid=(t_count // ROW_TILE,),
            in_specs=[pl.BlockSpec((None, 1, ROW_TILE * TOP_K), lambda i, tl: (i, 0, 0),
                                   memory_space=pltpu.SMEM),
                      pl.BlockSpec((ROW_TILE, d), lambda i, tl: (i, 0))],
            out_specs=pl.BlockSpec(memory_space=pl.ANY),
            scratch_shapes=[pltpu.VMEM((MOE_TILE, d), F32), pltpu.SemaphoreType.DMA(()),
                            pltpu.SemaphoreType.DMA(())],
        ),
        out_shape=jax.ShapeDtypeStruct((n_slots, d), F32),
        compiler_params=pltpu.CompilerParams(
            dimension_semantics=("arbitrary",), vmem_limit_bytes=VMEM_LIMIT),
        name="moe_dispatch",
    )(tail, dest, h)


def _expert_kernel(bexp_ref, nused_ref, x_ref, w1_ref, b1_ref, w2_ref, b2_ref, o_ref, w1b, w2b):
    i = pl.program_id(0)
    changed = jnp.logical_or(i == 0, bexp_ref[i] != bexp_ref[jnp.maximum(i - 1, 0)])
    used = i < nused_ref[0]

    @pl.when(jnp.logical_and(changed, used))
    def _():
        w1b[...] = w1_ref[...].astype(BF16)
        w2b[...] = w2_ref[...].astype(BF16)

    @pl.when(used)
    def _():
        f = w2b.shape[0]
        u = jnp.dot(x_ref[...].astype(BF16), w1b[...], preferred_element_type=F32) + b1_ref[...]
        gate = jnp.minimum(u[:, :f], SWIGLU_LIMIT)
        lin = jnp.clip(u[:, f:], -SWIGLU_LIMIT, SWIGLU_LIMIT)
        y = (lin + 1.0) * gate * jax.nn.sigmoid(SWIGLU_ALPHA * gate)
        o_ref[...] = jnp.dot(y.astype(BF16), w2b[...], preferred_element_type=F32) + b2_ref[...]

    @pl.when(jnp.logical_not(used))
    def _():
        o_ref[...] = jnp.zeros_like(o_ref)


def _expert_ffn(block_exp, n_used, xg, layer, w1, b1, w2, b2):
    n_slots, d = xg.shape
    _, n_exp, _, f2 = w1.shape
    f = f2 // 2
    n_blocks = n_slots // MOE_TILE
    return pl.pallas_call(
        _expert_kernel,
        grid_spec=pltpu.PrefetchScalarGridSpec(
            num_scalar_prefetch=2,
            grid=(n_blocks,),
            in_specs=[
                pl.BlockSpec((MOE_TILE, d), lambda i, be, nu: (jnp.minimum(i, nu[0] - 1), 0)),
                pl.BlockSpec((None, None, d, f2), lambda i, be, nu: (layer, be[i], 0, 0)),
                pl.BlockSpec((None, 1, f2), lambda i, be, nu: (be[i], 0, 0)),
                pl.BlockSpec((None, None, f, d), lambda i, be, nu: (layer, be[i], 0, 0)),
                pl.BlockSpec((None, 1, d), lambda i, be, nu: (be[i], 0, 0)),
            ],
            out_specs=pl.BlockSpec((MOE_TILE, d), lambda i, be, nu: (i, 0)),
            scratch_shapes=[pltpu.VMEM((d, f2), BF16), pltpu.VMEM((f, d), BF16)],
        ),
        out_shape=jax.ShapeDtypeStruct((n_slots, d), F32),
        compiler_params=pltpu.CompilerParams(
            dimension_semantics=("arbitrary",), vmem_limit_bytes=VMEM_LIMIT),
        name="expert_ffn",
    )(block_exp, n_used, xg, w1, b1.reshape(n_exp, 1, f2), w2, b2.reshape(n_exp, 1, d))


def _combine_kernel(dest_ref, y_hbm, x_ref, gate_ref, g_ref, fg_ref, o_ref, ybuf, sem, *, final):
    def issue(r, carry):
        for k in range(TOP_K):
            pltpu.make_async_copy(y_hbm.at[pl.ds(dest_ref[0, r * TOP_K + k], 1), :],
                                  ybuf.at[k, pl.ds(r, 1), :], sem).start(priority=k % 2)
        return carry

    lax.fori_loop(0, ROW_TILE, issue, 0, unroll=4)
    for k in range(TOP_K):
        pltpu.make_async_copy(y_hbm.at[pl.ds(0, ROW_TILE), :], ybuf.at[k], sem).wait()
    acc = jnp.zeros(o_ref.shape, F32)
    for k in range(TOP_K):
        acc += gate_ref[:, k:k + 1] * ybuf[k]
    out = x_ref[...] + g_ref[...] * acc
    if final:
        out = out * lax.rsqrt(jnp.mean(out * out, -1, keepdims=True) + EPS) * fg_ref[...]
    o_ref[...] = out


def _moe_combine(xs, y_slots, dest, gates, gate_tab, final_gain=None, n_out=None):
    n_batch, seq, d = xs.shape
    n_tiles = seq // ROW_TILE
    final = final_gain is not None
    n_out = n_out if final else seq
    fg = (final_gain if final else jnp.ones((d,), F32)).astype(F32).reshape(1, d)
    tok = lambda w: pl.BlockSpec((None, ROW_TILE, w), lambda b, j: (b, j, 0))
    return pl.pallas_call(
        functools.partial(_combine_kernel, final=final),
        grid=(n_batch, n_out // ROW_TILE),
        in_specs=[pl.BlockSpec((None, 1, ROW_TILE * TOP_K), lambda b, j: (b * n_tiles + j, 0, 0),
                               memory_space=pltpu.SMEM),
                  pl.BlockSpec(memory_space=pl.ANY),
                  tok(d), tok(GATE_PAD),
                  pl.BlockSpec((None, 1, d), _mod_index_map(n_batch, n_tiles)),
                  pl.BlockSpec((1, d), lambda b, j: (0, 0))],
        out_specs=tok(d),
        out_shape=jax.ShapeDtypeStruct((n_batch, n_out, d), F32),
        scratch_shapes=[pltpu.VMEM((TOP_K, ROW_TILE, d), F32), pltpu.SemaphoreType.DMA(())],
        compiler_params=pltpu.CompilerParams(
            dimension_semantics=("arbitrary", "arbitrary"), vmem_limit_bytes=VMEM_LIMIT),
        name="moe_combine",
    )(dest, y_slots, xs, gates, gate_tab, fg)


def _moe_layer(xs, gain, sc_tab, sh_tab, gate_tab, router_w, router_b, layer, w1, b1, w2, b2,
               final_gain=None, n_out=None):
    n_batch, seq, d = xs.shape
    t_count = n_batch * seq
    h, idx, gates, rank, counts = _moe_route(xs, gain, sc_tab, sh_tab, router_w, router_b)
    counts = counts[0, :N_EXPERTS].astype(jnp.int32)
    padded = (counts + MOE_TILE - 1) // MOE_TILE * MOE_TILE
    pend = jnp.cumsum(padded)
    pstart = pend - padded
    top_i = idx.reshape(t_count, GATE_PAD)[:, :TOP_K]
    hit = top_i[:, :, None] == jnp.arange(N_EXPERTS, dtype=jnp.int32)
    dest = rank.reshape(t_count, GATE_PAD)[:, :TOP_K] + jnp.sum(jnp.where(hit, pstart, 0), -1)
    dest = dest.astype(jnp.int32).reshape(t_count // ROW_TILE, 1, ROW_TILE * TOP_K)
    n_blocks = -(-(t_count * TOP_K + N_EXPERTS * (MOE_TILE - 1)) // MOE_TILE)
    tile_start = jnp.arange(n_blocks, dtype=jnp.int32)[:, None] * MOE_TILE
    block_exp = jnp.minimum(jnp.sum((pend[None, :] <= tile_start).astype(jnp.int32), -1), N_EXPERTS - 1)
    n_used = (pend[-1:] // MOE_TILE).astype(jnp.int32)
    tail = jnp.concatenate([jnp.where(padded > 0, pend - MOE_TILE, -1), pend[-1:]]).astype(jnp.int32)
    slots = _moe_dispatch(h.reshape(t_count, d), dest, tail, n_blocks * MOE_TILE)
    y_slots = _expert_ffn(block_exp, n_used, slots, layer, w1, b1, w2, b2)
    return _moe_combine(xs, y_slots, dest, gates, gate_tab, final_gain, n_out)


HG_SUB = 16


def _hg_consts():
    r = lax.broadcasted_iota(jnp.int32, (ROW_TILE, ROW_TILE), 0)
    c = lax.broadcasted_iota(jnp.int32, (ROW_TILE, ROW_TILE), 1)
    same_chunk = (r // HG_SUB) == (c // HG_SUB)
    tri_f = jnp.where(same_chunk & (c <= r), 1.0, 0.0).astype(F32)
    tri_b = jnp.where(same_chunk & (c >= r), 1.0, 0.0).astype(F32)
    blk = jnp.where(same_chunk, 1.0, 0.0).astype(F32)
    return tri_f, tri_b, blk


def _hg_prepare(z_ref, zf_col, lb, tri, blk, qs_s, key_s, cm_s, qh_s, kh_s, dec_s):
    q = jax.nn.silu(z_ref[:, 0:W_GROUP])
    zf = z_ref[:, zf_col:zf_col + W_GROUP]
    f = lb + (1.0 - lb) * jax.nn.sigmoid(zf)
    key = (1.0 - lb) * jax.nn.sigmoid(-zf)
    logf = jnp.log(f)
    cum = _dot_sel(tri, logf, 0, 3)
    tot = _dot_sel(blk, logf, 0, 3)
    qs_s[...] = q
    key_s[...] = key
    cm_s[...] = cum
    qh_s[...] = (q * jnp.exp(cum)).astype(BF16)
    kh_s[...] = (key * jnp.exp(tot - cum)).astype(BF16)
    dec_s[...] = jnp.exp(tot)


def _hg_chunk(c, reverse, z_ref, o_ref, st_ref, heads_bf, heads_f, qs_s, key_s, cm_s, qh_s, kh_s, dec_s):
    sl = pl.ds(pl.multiple_of(c * HG_SUB, HG_SUB), HG_SUB)
    q, k, cm = qs_s[sl, :], key_s[sl, :], cm_s[sl, :]
    v = z_ref[sl, W_GROUP:2 * W_GROUP]
    st = st_ref[...]
    o = lax.dot_general(qh_s[sl, :], st.astype(BF16), (((1,), (1,)), ((), ())),
                        preferred_element_type=F32)
    t_idx = lax.broadcasted_iota(jnp.int32, (HG_SUB, W_GROUP), 0)
    parts = []
    for s in range(HG_SUB):
        live = (t_idx <= s) if reverse else (t_idx >= s)
        e = jnp.exp(jnp.where(live, cm - cm[s:s + 1, :], NEG))
        parts.append((q * k[s:s + 1, :] * e).astype(BF16))
    r = jnp.dot(jnp.concatenate(parts, 0), heads_bf, preferred_element_type=F32)
    for s in range(HG_SUB):
        o += r[s * HG_SUB:(s + 1) * HG_SUB, :] * v[s:s + 1, :]
    o_ref[sl, :] = o
    ut = lax.dot_general(v.astype(BF16), kh_s[sl, :], (((0,), (0,)), ((), ())),
                         preferred_element_type=F32)
    st_ref[...] = st * dec_s[sl, :][0:1, :] + ut * heads_f


def _hgrn2_kernel(zf_ref, zb_ref, lb_ref, trif_ref, trib_ref, blk_ref, heads_ref, of_ref, ob_ref,
                  st_s, qs_s, key_s, cm_s, qh_s, kh_s, dec_s):
    @pl.when(pl.program_id(1) == 0)
    def _():
        st_s[...] = jnp.zeros_like(st_s)

    lb = lb_ref[...]
    blk = blk_ref[...]
    heads_f = heads_ref[...]
    heads_bf = heads_f.astype(BF16)
    _hg_prepare(zf_ref, 2 * W_GROUP, lb, trif_ref[...], blk, qs_s.at[0], key_s.at[0], cm_s.at[0],
                qh_s.at[0], kh_s.at[0], dec_s.at[0])
    _hg_prepare(zb_ref, 3 * W_GROUP, lb, trib_ref[...], blk, qs_s.at[1], key_s.at[1], cm_s.at[1],
                qh_s.at[1], kh_s.at[1], dec_s.at[1])
    n_chunks = ROW_TILE // HG_SUB

    def body(c, carry):
        _hg_chunk(c, False, zf_ref, of_ref, st_s.at[0], heads_bf, heads_f, qs_s.at[0], key_s.at[0],
                  cm_s.at[0], qh_s.at[0], kh_s.at[0], dec_s.at[0])
        _hg_chunk(n_chunks - 1 - c, True, zb_ref, ob_ref, st_s.at[1], heads_bf, heads_f, qs_s.at[1],
                  key_s.at[1], cm_s.at[1], qh_s.at[1], kh_s.at[1], dec_s.at[1])
        return carry

    lax.fori_loop(0, n_chunks, body, 0)


def _scan_tiles(n_lat_tiles):
    fwd = lambda j: jnp.where(j == 0, n_lat_tiles, j - 1)
    bwd = lambda j: jnp.where(j == 0, n_lat_tiles, n_lat_tiles - j)
    return fwd, bwd


def _hgrn2_scan(z_hg, lb):
    n_batch, seq, _ = z_hg.shape
    n_tiles = seq // ROW_TILE
    fwd, bwd = _scan_tiles(n_tiles - 1)
    const = pl.BlockSpec((ROW_TILE, ROW_TILE), lambda b, j: (0, 0))
    scr = lambda dt: pltpu.VMEM((2, ROW_TILE, W_GROUP), dt)
    return pl.pallas_call(
        _hgrn2_kernel,
        grid=(n_batch, n_tiles),
        in_specs=[pl.BlockSpec((None, ROW_TILE, HG_COLS), lambda b, j: (b, fwd(j), 0)),
                  pl.BlockSpec((None, ROW_TILE, HG_COLS), lambda b, j: (b, bwd(j), 0)),
                  pl.BlockSpec((1, W_GROUP), lambda b, j: (0, 0)),
                  const, const, const,
                  pl.BlockSpec((W_GROUP, W_GROUP), lambda b, j: (0, 0))],
        out_specs=[pl.BlockSpec((None, ROW_TILE, W_GROUP), lambda b, j: (b, fwd(j), 0)),
                   pl.BlockSpec((None, ROW_TILE, W_GROUP), lambda b, j: (b, bwd(j), 0))],
        out_shape=[jax.ShapeDtypeStruct((n_batch, seq, W_GROUP), F32)] * 2,
        scratch_shapes=[pltpu.VMEM((2, W_GROUP, W_GROUP), F32), scr(F32), scr(F32), scr(F32),
                        scr(BF16), scr(BF16), scr(F32)],
        compiler_params=pltpu.CompilerParams(
            dimension_semantics=("parallel", "arbitrary"), vmem_limit_bytes=VMEM_LIMIT),
        name="hgrn2_scan",
    )(z_hg, z_hg, lb.reshape(1, W_GROUP), *_hg_consts(), _head_mask())


def _dot_hi(a, b):
    return jnp.dot(a, b, preferred_element_type=F32, precision=lax.Precision.HIGHEST)


def _bf16_terms(x, terms):
    parts = []
    for _ in range(terms):
        p = x.astype(BF16)
        parts.append(p)
        x = x - p.astype(F32)
    return parts


def _dot_sel(a, b, select, terms, nt=False):
    ops = [a, b]
    fixed = ops[select].astype(BF16)
    dims = (((1,), (1,)), ((), ())) if nt else (((1,), (0,)), ((), ()))
    out = None
    for p in _bf16_terms(ops[1 - select], terms):
        pair = (fixed, p) if select == 0 else (p, fixed)
        d = lax.dot_general(pair[0], pair[1], dims, preferred_element_type=F32)
        out = d if out is None else out + d
    return out


def _dot_nt(a, b, precision=None):
    return lax.dot_general(a, b, (((1,), (1,)), ((), ())), preferred_element_type=F32,
                           precision=precision)


def _dot_tn(a, b):
    return lax.dot_general(a, b, (((0,), (0,)), ((), ())), preferred_element_type=F32)


def _chunk_consts(chunk, n_heads, gate_cols):
    r = lax.broadcasted_iota(jnp.int32, (chunk, chunk), 0)
    c = lax.broadcasted_iota(jnp.int32, (chunk, chunk), 1)
    low = (c <= r).astype(F32)
    upp = (c >= r).astype(F32)
    dh = W_GROUP // n_heads
    hr = lax.broadcasted_iota(jnp.int32, (W_GROUP, W_GROUP), 0) // dh
    hc = lax.broadcasted_iota(jnp.int32, (W_GROUP, W_GROUP), 1) // dh
    heads = (hr == hc).astype(F32)
    gc = lax.broadcasted_iota(jnp.int32, (GATE_PAD, W_GROUP), 0)
    gh = lax.broadcasted_iota(jnp.int32, (GATE_PAD, W_GROUP), 1) // dh
    expand = jnp.stack([(gc == g0 + gh).astype(F32) for g0 in gate_cols])
    sr = lax.broadcasted_iota(jnp.int32, (8, GATE_PAD), 0)
    sc = lax.broadcasted_iota(jnp.int32, (8, GATE_PAD), 1)
    select = jnp.stack([((sc == g0 + sr) & (sr < n_heads)).astype(F32) for g0 in gate_cols])
    return low, upp, heads, expand, select


class _MlDir:
    def __init__(self, reverse, z_ref, g_ref, gb, c_ref, n_ref, m_ref, low, upp, heads,
                 ex_i, ex_f, sel_i, sel_f):
        self.q = z_ref[:, 0:W_GROUP]
        self.k = z_ref[:, W_GROUP:2 * W_GROUP] * (ML_DH ** -0.5)
        self.v = z_ref[:, 2 * W_GROUP:3 * W_GROUP]
        gates = g_ref[...] + gb
        col_sum, row_sum = (upp, low) if reverse else (low, upp)
        self.li = _dot_sel(gates, ex_i, 1, 3)
        lf = jax.nn.log_sigmoid(_dot_sel(gates, ex_f, 1, 3))
        self.li_rows = _dot_sel(sel_i, gates, 0, 3, nt=True)
        lf_rows = jax.nn.log_sigmoid(_dot_sel(sel_f, gates, 0, 3, nt=True))
        self.fcum = _dot_sel(col_sum, lf, 0, 3)
        self.fcum_rows = _dot_sel(lf_rows, row_sum, 1, 3)
        last = 0 if reverse else ML_CHUNK - 1
        self.ftot = self.fcum[last:last + 1, :]
        g = self.ftot - self.fcum + self.li
        self.g_max = jnp.max(g, axis=0, keepdims=True)
        self.kw = self.k * jnp.exp(g - self.g_max)
        self.c0, self.n0, self.m0 = c_ref[...], n_ref[...], m_ref[...]
        self.k_bf = self.k.astype(BF16)
        self.qc = jnp.dot(self.q.astype(BF16), self.c0.astype(BF16), preferred_element_type=F32)
        self.qn = jnp.dot((self.q * self.n0).astype(BF16), heads.astype(BF16), preferred_element_type=F32)
        self.a_init = self.fcum + self.m0
        rr = lax.broadcasted_iota(jnp.int32, (ML_CHUNK, ML_CHUNK), 0)
        cc = lax.broadcasted_iota(jnp.int32, (ML_CHUNK, ML_CHUNK), 1)
        self.live = (cc >= rr) if reverse else (cc <= rr)


def _mlstm_kernel(zf_ref, zb_ref, gf_ref, gb_ref, bias_ref, low_ref, upp_ref, heads_ref, ex_ref, sel_ref,
                  of_ref, ob_ref, c_s, n_s, m_s):
    @pl.when(pl.program_id(1) == 0)
    def _():
        c_s[...] = jnp.zeros_like(c_s)
        n_s[...] = jnp.zeros_like(n_s)
        m_s[...] = jnp.zeros_like(m_s)

    low, upp, heads, bias = low_ref[...], upp_ref[...], heads_ref[...], bias_ref[...]
    dirs = [_MlDir(False, zf_ref, gf_ref, bias, c_s.at[0], n_s.at[0], m_s.at[0], low, upp, heads,
                   ex_ref[0], ex_ref[1], sel_ref[0], sel_ref[1]),
            _MlDir(True, zb_ref, gb_ref, bias, c_s.at[1], n_s.at[1], m_s.at[1], low, upp, heads,
                   ex_ref[2], ex_ref[3], sel_ref[2], sel_ref[3])]
    lane_head = lax.broadcasted_iota(jnp.int32, (ML_CHUNK, W_GROUP), 1) // ML_DH
    hs = lambda h: slice(h * ML_DH, (h + 1) * ML_DH)
    probs = [(d, h) for d in dirs for h in range(ML_HEADS)]
    qks = [_dot_nt(jnp.where(lane_head == h, d.q, 0.0).astype(BF16), d.k_bf) for d, h in probs]
    a = [jnp.where(d.live, d.fcum[:, hs(h)] - d.fcum_rows[h:h + 1, :] + d.li_rows[h:h + 1, :], NEG)
         for d, h in probs]
    m_t = [jnp.maximum(jnp.max(a_h, axis=1, keepdims=True), d.a_init[:, hs(h)])
           for (d, h), a_h in zip(probs, a)]
    p = [jnp.exp(a_h - m) * qk for a_h, m, qk in zip(a, m_t, qks)]
    pv = [jnp.dot(p_h.astype(BF16), d.v[:, hs(h)].astype(BF16), preferred_element_type=F32)
          for (d, h), p_h in zip(probs, p)]
    outs = []
    for (d, h), p_h, pv_h, m in zip(probs, p, pv, m_t):
        e0 = jnp.exp(d.a_init[:, hs(h)] - m)
        num = pv_h + e0 * d.qc[:, hs(h)]
        den = jnp.sum(p_h, axis=1, keepdims=True) + e0 * d.qn[:, hs(h)]
        outs.append(num / jnp.maximum(jnp.abs(den), jnp.exp(-m)))
    u_c = [_dot_tn(d.kw.astype(BF16), d.v.astype(BF16)) * heads for d in dirs]
    for i, (d, o_ref) in enumerate(zip(dirs, (of_ref, ob_ref))):
        o_ref[...] = jnp.concatenate(outs[i * ML_HEADS:(i + 1) * ML_HEADS], axis=1)
        m_new = jnp.maximum(d.ftot + d.m0, d.g_max)
        a_old = jnp.exp(d.ftot + d.m0 - m_new)
        a_new = jnp.exp(d.g_max - m_new)
        c_s[i] = a_old * d.c0 + a_new * u_c[i]
        n_s[i] = a_old * d.n0 + a_new * jnp.sum(d.kw, axis=0, keepdims=True)
        m_s[i] = m_new


def _scan_chunks(n_lat, n_ctx):
    fwd = lambda j: jnp.where(j < n_ctx, n_lat + j, j - n_ctx)
    bwd = lambda j: jnp.where(j < n_ctx, n_lat + n_ctx - 1 - j, n_lat + n_ctx - 1 - j)
    return fwd, bwd


def _mlstm_scan(z_ml, z_gt, gate_b, ctx_len):
    n_batch, seq, _ = z_ml.shape
    n_chunks = seq // ML_CHUNK
    n_ctx = ctx_len // ML_CHUNK
    fwd, bwd = _scan_chunks(n_chunks - n_ctx, n_ctx)
    low, upp, heads, expand, select = _chunk_consts(ML_CHUNK, ML_HEADS, (0, 4, 8, 12))
    bias = jnp.pad(gate_b.astype(F32).reshape(1, ML_GATES), ((0, 0), (0, GATE_PAD - ML_GATES)))
    full = lambda a: pl.BlockSpec(a.shape, lambda b, j: (0,) * a.ndim)
    return pl.pallas_call(
        _mlstm_kernel,
        grid=(n_batch, n_chunks),
        in_specs=[pl.BlockSpec((None, ML_CHUNK, ML_MAIN), lambda b, j: (b, fwd(j), 0)),
                  pl.BlockSpec((None, ML_CHUNK, ML_MAIN), lambda b, j: (b, bwd(j), 0)),
                  pl.BlockSpec((None, ML_CHUNK, GATE_PAD), lambda b, j: (b, fwd(j), 0)),
                  pl.BlockSpec((None, ML_CHUNK, GATE_PAD), lambda b, j: (b, bwd(j), 0)),
                  full(bias), full(low), full(upp), full(heads), full(expand), full(select)],
        out_specs=[pl.BlockSpec((None, ML_CHUNK, W_GROUP), lambda b, j: (b, fwd(j), 0)),
                   pl.BlockSpec((None, ML_CHUNK, W_GROUP), lambda b, j: (b, bwd(j), 0))],
        out_shape=[jax.ShapeDtypeStruct((n_batch, seq, W_GROUP), F32)] * 2,
        scratch_shapes=[pltpu.VMEM((2, W_GROUP, W_GROUP), F32), pltpu.VMEM((2, 1, W_GROUP), F32),
                        pltpu.VMEM((2, 1, W_GROUP), F32)],
        compiler_params=pltpu.CompilerParams(
            dimension_semantics=("parallel", "arbitrary"), vmem_limit_bytes=VMEM_LIMIT),
        name="mlstm_scan",
    )(z_ml, z_ml, z_gt, z_gt, bias, low, upp, heads, expand, select)


HALO = 8


def _short_conv_kernel(x_ref, prev_ref, next_ref, w_ref, b_ref, heads_ref, o_ref, *, n_lat_tiles, gdn):
    j = pl.program_id(1)
    n_tiles = pl.num_programs(1)
    x = x_ref[...]
    has_prev = jnp.logical_and(j != 0, j != n_lat_tiles)
    has_next = jnp.logical_and(j != n_lat_tiles - 1, j != n_tiles - 1)
    prev_row = jnp.where(has_prev, prev_ref[HALO - 1:HALO, :], 0.0)
    next_row = jnp.where(has_next, next_ref[0:1, :], 0.0)
    row = lax.broadcasted_iota(jnp.int32, x.shape, 0)
    x_prev = jnp.where(row == 0, prev_row, pltpu.roll(x, 1, 0))
    x_next = jnp.where(row == ROW_TILE - 1, next_row, pltpu.roll(x, ROW_TILE - 1, 0))
    y = x_prev * w_ref[0:1, :] + x * w_ref[1:2, :] + x_next * w_ref[2:3, :] + b_ref[...]
    if not gdn:
        o_ref[...] = y
        return
    y = jax.nn.silu(y)
    heads = heads_ref[...]
    for i in range(2):
        t = y[:, i * W_GROUP:(i + 1) * W_GROUP]
        o_ref[:, i * W_GROUP:(i + 1) * W_GROUP] = t * lax.rsqrt(_dot_sel(t * t, heads, 1, 3) + EPS)
    o_ref[:, 2 * W_GROUP:] = y[:, 2 * W_GROUP:]


def _head_mask():
    r = lax.broadcasted_iota(jnp.int32, (W_GROUP, W_GROUP), 0) // HG_DK
    c = lax.broadcasted_iota(jnp.int32, (W_GROUP, W_GROUP), 1) // HG_DK
    return (r == c).astype(F32)


def _short_conv_tiles(z, conv_w, conv_b, n_lat, gdn):
    n_batch, seq, _ = z.shape
    c = conv_w.shape[1]
    n_tiles = seq // ROW_TILE
    per = ROW_TILE // HALO
    n_halo = seq // HALO
    return pl.pallas_call(
        functools.partial(_short_conv_kernel, n_lat_tiles=n_lat // ROW_TILE, gdn=gdn),
        grid=(n_batch, n_tiles),
        in_specs=[pl.BlockSpec((None, ROW_TILE, c), lambda b, j: (b, j, 0)),
                  pl.BlockSpec((None, HALO, c), lambda b, j: (b, jnp.maximum(j * per - 1, 0), 0)),
                  pl.BlockSpec((None, HALO, c), lambda b, j: (b, jnp.minimum((j + 1) * per, n_halo - 1), 0)),
                  pl.BlockSpec((3, c), lambda b, j: (0, 0)),
                  pl.BlockSpec((1, c), lambda b, j: (0, 0)),
                  pl.BlockSpec((W_GROUP, W_GROUP), lambda b, j: (0, 0))],
        out_specs=pl.BlockSpec((None, ROW_TILE, c), lambda b, j: (b, j, 0)),
        out_shape=jax.ShapeDtypeStruct((n_batch, seq, c), F32),
        compiler_params=pltpu.CompilerParams(
            dimension_semantics=("parallel", "parallel"), vmem_limit_bytes=VMEM_LIMIT),
        name="gdn_conv" if gdn else "hyena_conv",
    )(z, z, z, conv_w.astype(F32), conv_b.astype(F32).reshape(1, c), _head_mask())


def _gd_consts():
    r = lax.broadcasted_iota(jnp.int32, (GD_CHUNK, GD_CHUNK), 0)
    c = lax.broadcasted_iota(jnp.int32, (GD_CHUNK, GD_CHUNK), 1)
    same = lambda n: (r // n) == (c // n)
    eye = (r == c).astype(F32)
    masks = [same(8).astype(F32)] + [(same(2 * b) & ~same(b)).astype(F32) for b in (8, 16, 32)]
    return eye, jnp.stack(masks)


def _unit_tri_inverses(ns, eye, masks):
    mm = lambda a, b: jnp.dot(a.astype(BF16), b.astype(BF16), preferred_element_type=F32)
    n8 = [n * masks[0] for n in ns]
    n2 = [mm(a, a) for a in n8]
    n4 = [mm(a, a) for a in n2]
    ts = [mm(eye - a, eye + b) for a, b in zip(n8, n2)]
    ts = [mm(t, eye + b) for t, b in zip(ts, n4)]
    for i in range(1, 4):
        lts = [mm(n * masks[i], t) for n, t in zip(ns, ts)]
        ts = [t - mm(t, lt) for t, lt in zip(ts, lts)]
    return ts


class _GdDir:
    def __init__(self, reverse, x_ref, g_ref, s_ref, prm_lane, prm_row, low, upp, ex_a, ex_b, sel_a):
        self.q = x_ref[:, 0:W_GROUP] * (GD_DH ** -0.5)
        self.k = x_ref[:, W_GROUP:2 * W_GROUP]
        self.v = x_ref[:, 2 * W_GROUP:3 * W_GROUP]
        gates = g_ref[...]
        col_sum, row_sum = (upp, low) if reverse else (low, upp)
        log_a = prm_lane[0:1, :] * jax.nn.softplus(_dot_sel(gates, ex_a, 1, 3) + prm_lane[1:2, :])
        self.beta = jax.nn.sigmoid(_dot_sel(gates, ex_b, 1, 3))
        log_a_rows = prm_row[0] * jax.nn.softplus(_dot_sel(sel_a, gates, 0, 3, nt=True) + prm_row[1])
        self.g = _dot_sel(col_sum, log_a, 0, 3)
        self.g_rows = _dot_sel(log_a_rows, row_sum, 1, 3)
        last = 0 if reverse else GD_CHUNK - 1
        self.g_last = self.g[last:last + 1, :]
        self.eg = jnp.exp(self.g)
        self.k_bf = self.k.astype(BF16)
        self.kb = self.k * self.beta
        self.s0 = s_ref[...]
        rr = lax.broadcasted_iota(jnp.int32, (GD_CHUNK, GD_CHUNK), 0)
        cc = lax.broadcasted_iota(jnp.int32, (GD_CHUNK, GD_CHUNK), 1)
        self.live = (cc >= rr) if reverse else (cc <= rr)
        self.strict = (cc > rr) if reverse else (cc < rr)


def _gdn_kernel(xf_ref, xb_ref, gf_ref, gb_ref, prm_lane_ref, prm_row_ref, low_ref, upp_ref, heads_ref,
                ex_ref, sel_ref, eye_ref, masks_ref, of_ref, ob_ref, s_s):
    @pl.when(pl.program_id(1) == 0)
    def _():
        s_s[...] = jnp.zeros_like(s_s)

    low, upp, heads, eye, masks = low_ref[...], upp_ref[...], heads_ref[...], eye_ref[...], masks_ref[...]
    dirs = [_GdDir(False, xf_ref, gf_ref, s_s.at[0], prm_lane_ref[0], prm_row_ref[0], low, upp,
                   ex_ref[0], ex_ref[1], sel_ref[0]),
            _GdDir(True, xb_ref, gb_ref, s_s.at[1], prm_lane_ref[1], prm_row_ref[1], low, upp,
                   ex_ref[2], ex_ref[3], sel_ref[2])]
    lane_head = lax.broadcasted_iota(jnp.int32, (GD_CHUNK, W_GROUP), 1) // GD_DH
    probs = [(d, h) for d in dirs for h in range(GD_HEADS)]
    only = lambda h, t: jnp.where(lane_head == h, t, 0.0).astype(BF16)
    hs = lambda h: slice(h * GD_DH, (h + 1) * GD_DH)
    decay = [jnp.exp(jnp.where(d.live, d.g[:, hs(h)] - d.g_rows[h:h + 1, :], NEG)) for d, h in probs]
    kks = [_dot_nt(only(h, d.kb), d.k_bf) for d, h in probs]
    ns = [jnp.where(d.strict, kk * dc, 0.0) for (d, h), kk, dc in zip(probs, kks, decay)]
    t_inv = [t.astype(BF16) for t in _unit_tri_inverses(ns, eye, masks)]
    us = [jnp.dot(t, only(h, d.v * d.beta), preferred_element_type=F32) for (d, h), t in zip(probs, t_inv)]
    ws = [jnp.dot(t, only(h, d.kb * d.eg), preferred_element_type=F32) for (d, h), t in zip(probs, t_inv)]
    qks = [(_dot_nt(only(h, d.q), d.k_bf) * dc).astype(BF16) for (d, h), dc in zip(probs, decay)]
    v_new, o = [], []
    for i, d in enumerate(dirs):
        sl = slice(i * GD_HEADS, (i + 1) * GD_HEADS)
        s_bf = d.s0.astype(BF16)
        v_new.append(sum(us[sl]) - jnp.dot(sum(ws[sl]).astype(BF16), s_bf, preferred_element_type=F32))
        o.append(jnp.dot((d.q * d.eg).astype(BF16), s_bf, preferred_element_type=F32))
    intra = [jnp.dot(qk, only(h, v_new[i // GD_HEADS]), preferred_element_type=F32)
             for i, ((d, h), qk) in enumerate(zip(probs, qks))]
    for i, (d, o_ref) in enumerate(zip(dirs, (of_ref, ob_ref))):
        o_ref[...] = o[i] + sum(intra[i * GD_HEADS:(i + 1) * GD_HEADS])
        k_dec = d.k * jnp.exp(d.g_last - d.g)
        s_s[i] = jnp.exp(d.g_last) * d.s0 + _dot_tn(k_dec.astype(BF16), v_new[i].astype(BF16)) * heads


def _gdn_scan(qkv, z_gt, a_log, dt_bias, ctx_len):
    n_batch, seq, _ = qkv.shape
    n_chunks = seq // GD_CHUNK
    n_ctx = ctx_len // GD_CHUNK
    fwd, bwd = _scan_chunks(n_chunks - n_ctx, n_ctx)
    g0 = ML_GATES
    low, upp, heads, expand, select = _chunk_consts(GD_CHUNK, GD_HEADS, (g0, g0 + 4, g0 + 8, g0 + 12))
    eye, masks = _gd_consts()
    prm = jnp.stack([-jnp.exp(a_log.astype(F32)), dt_bias.astype(F32)], 1)
    prm_lane = jnp.repeat(prm, GD_DH, axis=2)
    prm_row = jnp.broadcast_to(jnp.pad(prm, ((0, 0), (0, 0), (0, 8 - GD_HEADS)))[..., None],
                               (2, 2, 8, GD_CHUNK))
    full = lambda a: pl.BlockSpec(a.shape, lambda b, j: (0,) * a.ndim)
    return pl.pallas_call(
        _gdn_kernel,
        grid=(n_batch, n_chunks),
        in_specs=[pl.BlockSpec((None, GD_CHUNK, 3 * W_GROUP), lambda b, j: (b, fwd(j), 0)),
                  pl.BlockSpec((None, GD_CHUNK, 3 * W_GROUP), lambda b, j: (b, bwd(j), 0)),
                  pl.BlockSpec((None, GD_CHUNK, GATE_PAD), lambda b, j: (b, fwd(j), 0)),
                  pl.BlockSpec((None, GD_CHUNK, GATE_PAD), lambda b, j: (b, bwd(j), 0)),
                  full(prm_lane), full(prm_row), full(low), full(upp), full(heads), full(expand),
                  full(select), full(eye), full(masks)],
        out_specs=[pl.BlockSpec((None, GD_CHUNK, W_GROUP), lambda b, j: (b, fwd(j), 0)),
                   pl.BlockSpec((None, GD_CHUNK, W_GROUP), lambda b, j: (b, bwd(j), 0))],
        out_shape=[jax.ShapeDtypeStruct((n_batch, seq, W_GROUP), F32)] * 2,
        scratch_shapes=[pltpu.VMEM((2, W_GROUP, W_GROUP), F32)],
        compiler_params=pltpu.CompilerParams(
            dimension_semantics=("parallel", "arbitrary"), vmem_limit_bytes=VMEM_LIMIT),
        name="gdn_scan",
    )(qkv, qkv, z_gt, z_gt, prm_lane, prm_row, low, upp, heads, expand, select, eye, masks)


HY_FILT_TILE = 512
DFT_TILE = 512


DFT_ROWS = 64


def _dft_table_kernel(ac_ref, as_ref, bc_ref, bs_ref, re_ref, im_ref, imt_ref):
    a = pl.program_id(0)
    ac, sa = ac_ref[pl.ds(a, 1), :], as_ref[pl.ds(a, 1), :]
    bc, sb = bc_ref[...], bs_ref[...]
    cos = ac * bc - sa * sb
    nsin = -(sa * bc + ac * sb)
    i = lax.broadcasted_iota(jnp.int32, cos.shape, 0) + a * DFT_ROWS
    j = lax.broadcasted_iota(jnp.int32, cos.shape, 1)
    sign = lambda n: jnp.where(n % 2 == 0, 1.0, -1.0)
    re_ref[...] = cos.astype(BF16)
    im_ref[...] = jnp.where(i == 0, sign(j), nsin).astype(BF16)
    imt_ref[...] = jnp.where(j == 0, sign(i), nsin).astype(BF16)


def _dft_matrices(length):
    n = 2 * length
    j = jnp.arange(length, dtype=jnp.int32)[None, :]
    coarse = (jnp.arange(length // DFT_ROWS, dtype=jnp.int32)[:, None] * DFT_ROWS * j) % n
    fine = (jnp.arange(DFT_ROWS, dtype=jnp.int32)[:, None] * j) % n
    w = 2.0 * math.pi / n
    tabs = [jnp.cos(coarse.astype(F32) * w), jnp.sin(coarse.astype(F32) * w),
            jnp.cos(fine.astype(F32) * w), jnp.sin(fine.astype(F32) * w)]
    full = lambda a: pl.BlockSpec(a.shape, lambda i: (0, 0))
    out = pl.BlockSpec((DFT_ROWS, length), lambda i: (i, 0))
    return pl.pallas_call(
        _dft_table_kernel,
        grid=(length // DFT_ROWS,),
        in_specs=[full(t) for t in tabs],
        out_specs=[out] * 3,
        out_shape=[jax.ShapeDtypeStruct((length, length), BF16)] * 3,
        compiler_params=pltpu.CompilerParams(dimension_semantics=("parallel",)),
        name="dft_table",
    )(*tabs)


def _hy_filter_kernel(feat_ref, w1_ref, b1_ref, w2_ref, b2_ref, fr_ref, w3_ref, win_ref, filt_ref, ss_ref):
    i = pl.program_id(0)
    hdn = jnp.sin(fr_ref[0:1, :] * (_dot_hi(feat_ref[...], w1_ref[...]) + b1_ref[...]))
    hdn = jnp.sin(fr_ref[1:2, :] * (_dot_hi(hdn, w2_ref[...]) + b2_ref[...]))
    filt = _dot_hi(hdn, w3_ref[...]) * win_ref[...]
    half = filt.shape[1] // 2
    row = lax.broadcasted_iota(jnp.int32, filt.shape, 0) + i * filt.shape[0]
    col = lax.broadcasted_iota(jnp.int32, filt.shape, 1)
    filt = jnp.where(jnp.logical_and(row == 0, col >= half), 0.0, filt)
    filt_ref[...] = filt.astype(BF16)
    sq = jnp.sum(filt * filt, axis=0, keepdims=True)

    @pl.when(i == 0)
    def _():
        ss_ref[...] = jnp.zeros_like(ss_ref)

    ss_ref[...] += sq[:, :half] + sq[:, half:]


def _hyena_filters(length, w1, b1, w2, b2, freq, w3):
    tile = min(HY_FILT_TILE, length)
    t = jnp.linspace(0.0, 1.0, length, dtype=F32)[:, None]
    pos = jnp.arange(length, dtype=F32)[:, None]
    band = jnp.linspace(1e-4, HY_BANDS - 1, HY_BANDS, dtype=F32)[None, :]
    ang = 2.0 * math.pi * pos * band / length
    feats = jnp.concatenate([t, jnp.cos(ang), -jnp.sin(ang)], -1)
    deltas = jnp.abs(jnp.linspace(math.log(HY_TARGET) / HY_SLOW_DECAY,
                                  math.log(HY_TARGET) / HY_FAST_DECAY, W_GROUP, dtype=F32))
    window = jnp.tile(jnp.exp(-t * deltas), (1, 2 * HY_ORDER))
    n_out = w3.shape[1]
    emb, ffn = w1.shape
    full = lambda a: pl.BlockSpec(a.shape, lambda i: (0,) * a.ndim)
    args = [w1.astype(F32), b1.astype(F32).reshape(1, ffn), w2.astype(F32), b2.astype(F32).reshape(1, ffn),
            freq.astype(F32), w3.astype(F32)]
    filt, ss = pl.pallas_call(
        _hy_filter_kernel,
        grid=(length // tile,),
        in_specs=[pl.BlockSpec((tile, emb), lambda i: (i, 0))] + [full(a) for a in args] +
                 [pl.BlockSpec((tile, n_out), lambda i: (i, 0))],
        out_specs=[pl.BlockSpec((tile, n_out), lambda i: (i, 0)),
                   pl.BlockSpec((1, n_out // 2), lambda i: (0, 0))],
        out_shape=[jax.ShapeDtypeStruct((length, n_out), BF16),
                   jax.ShapeDtypeStruct((1, n_out // 2), F32)],
        compiler_params=pltpu.CompilerParams(
            dimension_semantics=("arbitrary",), vmem_limit_bytes=VMEM_LIMIT),
        name="hyena_filter",
    )(feats, *args, window)
    return filt, ss


def _spec_mul(yr, yi, kr, ki, first):
    row0 = jnp.logical_and(first, lax.broadcasted_iota(jnp.int32, yr.shape, 0) == 0)
    zr = yr * kr - jnp.where(row0, 0.0, yi * ki)
    zi = jnp.where(row0, yi * ki, yr * ki + yi * kr)
    return zr, zi


def _dft_filter_kernel(wre_ref, wim_ref, x_ref, ss_ref, kre_ref, kim_ref):
    x = x_ref[...]
    yr = jnp.dot(wre_ref[...], x, preferred_element_type=F32)
    yi = jnp.dot(wim_ref[...], x, preferred_element_type=F32)
    half = yr.shape[1] // 2
    row0 = jnp.logical_and(pl.program_id(0) == 0,
                           lax.broadcasted_iota(jnp.int32, (yr.shape[0], half), 0) == 0)
    n = 2 * x.shape[0]
    scale = lax.rsqrt(ss_ref[...] + EPS) * jnp.where(row0, 1.0 / n, 2.0 / n)
    kre_ref[...] = (yr[:, :half] + yr[:, half:]) * scale
    kim_ref[...] = (yi[:, :half] + jnp.where(row0, yi[:, half:], -yi[:, half:])) * scale


def _dft_filter(f_re, f_im, filt, ss):
    length, n = filt.shape
    tile = min(DFT_TILE, length)
    wspec = pl.BlockSpec((tile, length), lambda i: (i, 0))
    return pl.pallas_call(
        _dft_filter_kernel,
        grid=(length // tile,),
        in_specs=[wspec, wspec, pl.BlockSpec((length, n), lambda i: (0, 0)),
                  pl.BlockSpec((1, n // 2), lambda i: (0, 0))],
        out_specs=[pl.BlockSpec((tile, n // 2), lambda i: (i, 0))] * 2,
        out_shape=[jax.ShapeDtypeStruct((length, n // 2), F32)] * 2,
        compiler_params=pltpu.CompilerParams(
            dimension_semantics=("parallel",), vmem_limit_bytes=VMEM_LIMIT),
        name="hyena_filter_dft",
    )(f_re, f_im, filt, ss)


def _dft_fwd_kernel(wre_ref, wim_ref, y_ref, kre_ref, kim_ref, zre_ref, zim_ref):
    y = y_ref[...].astype(BF16)
    yr = jnp.dot(wre_ref[...], y, preferred_element_type=F32)
    yi = jnp.dot(wim_ref[...], y, preferred_element_type=F32)
    zr, zi = _spec_mul(yr, yi, kre_ref[...], kim_ref[...], pl.program_id(0) == 0)
    zre_ref[...] = zr.astype(BF16)
    zim_ref[...] = zi.astype(BF16)


def _dft_fwd(f_re, f_im, y, k_re, k_im, order, row0):
    n_batch = y.shape[0]
    length = f_re.shape[0]
    tile = min(DFT_TILE, length)
    wspec = pl.BlockSpec((tile, length), lambda i, b: (i, 0))
    kspec = pl.BlockSpec((tile, W_GROUP), lambda i, b: (i, order))
    zspec = pl.BlockSpec((None, tile, W_GROUP), lambda i, b: (b, i, 0))
    return pl.pallas_call(
        _dft_fwd_kernel,
        grid=(length // tile, n_batch),
        in_specs=[wspec, wspec,
                  pl.BlockSpec((None, length, W_GROUP), lambda i, b: (b, row0 // length, 0)),
                  kspec, kspec],
        out_specs=[zspec, zspec],
        out_shape=[jax.ShapeDtypeStruct((n_batch, length, W_GROUP), BF16)] * 2,
        compiler_params=pltpu.CompilerParams(
            dimension_semantics=("parallel", "arbitrary"), vmem_limit_bytes=VMEM_LIMIT),
        name="hyena_dft_fwd",
    )(f_re, f_im, y, k_re, k_im)


def _dft_inv_kernel(wre_ref, wim_ref, zre_ref, zim_ref, y_ref, gate_ref, bias_ref, o_ref, obf_ref):
    conv = (jnp.dot(wre_ref[...], zre_ref[...], preferred_element_type=F32) +
            jnp.dot(wim_ref[...], zim_ref[...], preferred_element_type=F32))
    out = gate_ref[...] * (conv + bias_ref[...] * y_ref[...])
    o_ref[...] = out
    obf_ref[...] = out.astype(BF16)


def _dft_inv(i_re, i_im, z_re, z_im, y_src, gate_src, bias):
    n_batch, length, _ = z_re.shape
    tile = min(DFT_TILE, length)
    wspec = pl.BlockSpec((tile, length), lambda i, b: (i, 0))
    zspec = pl.BlockSpec((None, length, W_GROUP), lambda i, b: (b, 0, 0))
    ospec = pl.BlockSpec((None, tile, W_GROUP), lambda i, b: (b, i, 0))
    view = lambda src: pl.BlockSpec((None, tile, W_GROUP), lambda i, b: (b, src[1] // tile + i, src[2]))
    return pl.pallas_call(
        _dft_inv_kernel,
        grid=(length // tile, n_batch),
        in_specs=[wspec, wspec, zspec, zspec, view(y_src), view(gate_src),
                  pl.BlockSpec((1, W_GROUP), lambda i, b: (0, 0))],
        out_specs=[ospec, ospec],
        out_shape=[jax.ShapeDtypeStruct((n_batch, length, W_GROUP), F32),
                   jax.ShapeDtypeStruct((n_batch, length, W_GROUP), BF16)],
        compiler_params=pltpu.CompilerParams(
            dimension_semantics=("parallel", "arbitrary"), vmem_limit_bytes=VMEM_LIMIT),
        name="hyena_dft_inv",
    )(i_re, i_im, z_re, z_im, y_src[0], gate_src[0], bias.astype(F32).reshape(1, W_GROUP))


def _hyena_segment(u, row0, length, dft, filt_params, bias):
    f_re, f_im, i_im = dft
    i_re = f_re
    filt, ss = _hyena_filters(length, *filt_params)
    k_re, k_im = _dft_filter(f_re, f_im, filt, ss)
    z_re, z_im = _dft_fwd(f_re, f_im, u, k_re, k_im, 0, row0)
    y, y_bf = _dft_inv(i_re, i_im, z_re, z_im, (u, row0, 0), (u, row0, 1), bias[0])
    z_re, z_im = _dft_fwd(f_re, f_im, y_bf, k_re, k_im, 1, 0)
    y, _ = _dft_inv(i_re, i_im, z_re, z_im, (y, 0, 0), (u, row0, 2), bias[1])
    return y


GT_COLS = 4


def _grid_transpose_kernel(x_ref, c_ref, o_ref, *, n_major, n_lat_steps):
    j = pl.program_id(1)
    for half in range(2):
        @pl.when(jnp.logical_and(j < n_lat_steps, j % 2 == half))
        def _():
            for i in range(GT_COLS):
                o_ref[i * n_major:(i + 1) * n_major, :] = x_ref[:, half * GT_COLS + i, :]

    @pl.when(j >= n_lat_steps)
    def _():
        o_ref[...] = c_ref[...]


def _grid_transpose(t, n_lat, n_major):
    n_batch, seq, w = t.shape
    n_minor = n_lat // n_major
    rows = GT_COLS * n_major
    n_lat_steps = n_minor // GT_COLS
    view = t.reshape(n_batch, seq // n_minor, n_minor, w)
    return pl.pallas_call(
        functools.partial(_grid_transpose_kernel, n_major=n_major, n_lat_steps=n_lat_steps),
        grid=(n_batch, seq // rows),
        in_specs=[pl.BlockSpec((None, n_major, 2 * GT_COLS, w),
                               lambda b, j: (b, 0, jnp.minimum(j, n_lat_steps - 1) // 2, 0)),
                  pl.BlockSpec((None, rows, w), lambda b, j: (b, jnp.maximum(j, n_lat_steps), 0))],
        out_specs=pl.BlockSpec((None, rows, w), lambda b, j: (b, j, 0)),
        out_shape=jax.ShapeDtypeStruct((n_batch, seq, w), t.dtype),
        compiler_params=pltpu.CompilerParams(
            dimension_semantics=("parallel", "arbitrary"), vmem_limit_bytes=VMEM_LIMIT),
        name="grid_transpose",
    )(view, t)


def _permute_w_in(w_in):
    o_ml = HG_COLS
    o_hy = o_ml + ML_MAIN + ML_GATES
    o_gd = o_hy + HY_COLS
    parts = [w_in[:, :HG_COLS], w_in[:, o_ml:o_ml + ML_MAIN], w_in[:, o_hy:o_gd],
             w_in[:, o_gd:o_gd + GD_MAIN], w_in[:, o_ml + ML_MAIN:o_hy],
             w_in[:, o_gd + GD_MAIN:],
             jnp.zeros((w_in.shape[0], GATE_PAD - ML_GATES - GD_GATES), w_in.dtype)]
    return jnp.concatenate(parts, 1).astype(BF16)


IN_SPLITS = (HG_COLS, ML_MAIN, HY_COLS, GD_MAIN, GATE_PAD)


def kernel(x, c, ctx, c_ctx, mod_w, mod_b, norm1_g, norm2_g, w_in, w_out, hg_lb_logits, hg_norm_g,
           ml_gate_b, ml_norm_g, hy_conv_w, hy_conv_b, hy_w1, hy_b1, hy_w2, hy_b2, hy_freq, hy_w3,
           hy_bias, gd_conv_w, gd_a_log, gd_dt_bias, gd_norm_g, router_w, router_b, exp_w1, exp_b1,
           exp_w2, exp_b2, final_g):
    n_batch, seq, d = x.shape
    ctx_len = ctx.shape[1]
    depth = mod_w.shape[0]
    rows = seq // GRID_W
    p = jax.nn.softmax(hg_lb_logits.astype(F32), axis=0)
    lower_bounds = jnp.cumsum(p, 0) - p[0]
    s_all = jnp.concatenate([jax.nn.silu(c), jax.nn.silu(c_ctx)[None]], 0)
    xs = jnp.concatenate([x, ctx], 1)
    colmajor = lambda t: _grid_transpose(t, seq, rows)
    rowmajor = lambda t: _grid_transpose(t, seq, GRID_W)
    dft_lat, dft_ctx = _dft_matrices(seq), _dft_matrices(ctx_len)
    for l in range(depth):
        mod = (s_all @ mod_w[l] + mod_b[l]).reshape(n_batch + 1, 6, 1, d)
        sh1, sc1, g1, sh2, sc2, g2 = (mod[:, i] for i in range(6))
        z_hg, z_ml, z_hy, z_gd, z_gt = _in_proj(xs, norm1_g[l], sc1, sh1,
                                                _permute_w_in(w_in[l]), IN_SPLITS)
        hg_f, hg_b = _hgrn2_scan(z_hg, lower_bounds[l])
        ml_f, ml_b = _mlstm_scan(colmajor(z_ml), colmajor(z_gt), ml_gate_b[l], ctx_len)
        filt = (hy_w1[l], hy_b1[l], hy_w2[l], hy_b2[l], hy_freq[l], hy_w3[l])
        u_hy = _short_conv_tiles(z_hy, hy_conv_w[l], hy_conv_b[l], seq, False)
        c_lat = _hyena_segment(u_hy, 0, seq, dft_lat, filt, hy_bias[l])
        if l == depth - 1:
            c_ctx = jnp.zeros((n_batch, ctx_len, W_GROUP), F32)
        else:
            c_ctx = _hyena_segment(u_hy, seq, ctx_len, dft_ctx, filt, hy_bias[l])
        qkv = _short_conv_tiles(z_gd, gd_conv_w[l], jnp.zeros((3 * W_GROUP,), F32), seq, True)
        gd_f, gd_b = _gdn_scan(qkv, z_gt, gd_a_log[l], gd_dt_bias[l], ctx_len)
        mixers = [("silu", hg_f, hg_b, (z_hg, 4)),
                  ("sigmoid", rowmajor(ml_f), rowmajor(ml_b), (z_ml, 3)),
                  ("final", jnp.concatenate([c_lat, c_ctx], 1)),
                  ("silu", gd_f, gd_b, (z_gd, 3))]
        gains = jnp.stack([hg_norm_g[l], ml_norm_g[l], jnp.ones_like(ml_norm_g[l]), gd_norm_g[l]]).astype(F32)
        xs = _out_proj(xs, mixers, gains, g1, w_out[l].astype(BF16))
        last = l == depth - 1
        xs = _moe_layer(xs, norm2_g[l], sc2, sh2, g2, router_w[l], router_b[l],
                        l, exp_w1, exp_b1[l], exp_w2, exp_b2[l],
                        final_gain=final_g if last else None, n_out=seq if last else None)
    return xs
```

```python
import functools
import math

import jax
import jax.numpy as jnp
from jax import lax
from jax.experimental import pallas as pl
from jax.experimental.pallas import tpu as pltpu

F32 = jnp.float32
BF16 = jnp.bfloat16

D_MODEL = 1024
GRID_W = 64
N_MIXERS = 4
W_GROUP = D_MODEL // N_MIXERS
HG_HEADS = 4
HG_DK = W_GROUP // HG_HEADS
HG_CHUNK = 16
ML_HEADS = 4
ML_DH = W_GROUP // ML_HEADS
ML_CHUNK = 64
HY_ORDER = 2
HY_BANDS = 8
HY_FAST_DECAY = 0.3
HY_SLOW_DECAY = 1.5
HY_TARGET = 1e-2
GD_HEADS = 4
GD_DH = W_GROUP // GD_HEADS
GD_CHUNK = 64
N_EXPERTS = 32
TOP_K = 4
SWIGLU_LIMIT = 7.0
SWIGLU_ALPHA = 1.702
EPS = 1e-6
NEG = -1e30
HG_COLS = 5 * W_GROUP
ML_MAIN = 4 * W_GROUP
ML_GATES = 4 * ML_HEADS
HY_COLS = 3 * W_GROUP
GD_MAIN = 4 * W_GROUP
GD_GATES = 4 * GD_HEADS
GATE_PAD = 128

ROW_TILE = 256
MOE_TILE = 256
VMEM_LIMIT = 56 * 1024 * 1024


def _norm_mod(x, gain, sc, sh):
    y = x * lax.rsqrt(jnp.mean(x * x, -1, keepdims=True) + EPS)
    return y * gain * (1.0 + sc) + sh


def _in_proj_kernel(x_ref, g_ref, sc_ref, sh_ref, w_ref, *out_refs, splits):
    h = _norm_mod(x_ref[...], g_ref[...], sc_ref[...], sh_ref[...]).astype(BF16)
    off = 0
    for o_ref, n in zip(out_refs, splits):
        o_ref[...] = jnp.dot(h, w_ref[:, off:off + n], preferred_element_type=F32)
        off += n


def _mod_index_map(n_batch, n_tiles):
    return lambda b, j: (jnp.where(j == n_tiles - 1, n_batch, b), 0, 0)


def _in_proj(xs, gain, sc_tab, sh_tab, w, splits):
    n_batch, seq, d = xs.shape
    n = w.shape[1]
    grid = (n_batch, seq // ROW_TILE)
    mod_spec = pl.BlockSpec((None, 1, d), _mod_index_map(n_batch, seq // ROW_TILE))
    return pl.pallas_call(
        functools.partial(_in_proj_kernel, splits=splits),
        grid=grid,
        in_specs=[
            pl.BlockSpec((None, ROW_TILE, d), lambda b, j: (b, j, 0)),
            pl.BlockSpec((1, d), lambda b, j: (0, 0)),
            mod_spec, mod_spec,
            pl.BlockSpec((d, n), lambda b, j: (0, 0)),
        ],
        out_specs=[pl.BlockSpec((None, ROW_TILE, s), lambda b, j: (b, j, 0)) for s in splits],
        out_shape=[jax.ShapeDtypeStruct((n_batch, seq, s), F32) for s in splits],
        compiler_params=pltpu.CompilerParams(
            dimension_semantics=("parallel", "parallel"), vmem_limit_bytes=VMEM_LIMIT),
        name="in_proj",
    )(xs, gain.reshape(1, d), sc_tab, sh_tab, w)


def _moe_route_kernel(x_ref, g_ref, sc_ref, sh_ref, rw_ref, rb_ref, tri_ref, h_ref, idx_ref, gate_ref,
                      rank_ref, cnt_ref, carry):
    @pl.when(jnp.logical_and(pl.program_id(0) == 0, pl.program_id(1) == 0))
    def _():
        carry[...] = jnp.zeros_like(carry)

    h = _norm_mod(x_ref[...], g_ref[...], sc_ref[...], sh_ref[...])
    h_ref[...] = h
    work = _dot_hi(h, rw_ref[...]) + rb_ref[...]
    lane = lax.broadcasted_iota(jnp.int32, work.shape, 1)
    vals, hits = [], []
    for _ in range(TOP_K):
        m = jnp.max(work, axis=-1, keepdims=True)
        first = jnp.min(jnp.where(work == m, lane, GATE_PAD), axis=-1, keepdims=True)
        hit = lane == first
        vals.append(m)
        hits.append(hit)
        work = jnp.where(hit, -jnp.inf, work)
    exps = [jnp.exp(v - vals[0]) for v in vals]
    total = sum(exps)
    chosen = sum(hit.astype(F32) for hit in hits)
    before = jnp.dot(tri_ref[...], chosen.astype(BF16), preferred_element_type=F32) + carry[...]
    idx = jnp.zeros(work.shape, jnp.int32)
    gate = jnp.zeros(work.shape, F32)
    rank = jnp.zeros(work.shape, jnp.int32)
    for k in range(TOP_K):
        idx = jnp.where(lane == k, jnp.sum(jnp.where(hits[k], lane, 0), axis=-1, keepdims=True), idx)
        gate = jnp.where(lane == k, exps[k] / total, gate)
        r_k = jnp.sum(jnp.where(hits[k], before, 0.0), axis=-1, keepdims=True)
        rank = jnp.where(lane == k, r_k.astype(jnp.int32), rank)
    idx_ref[...] = idx
    gate_ref[...] = gate
    rank_ref[...] = rank
    carry[...] += jnp.sum(chosen, axis=0, keepdims=True)
    cnt_ref[...] = carry[...]


def _moe_route(xs, gain, sc_tab, sh_tab, router_w, router_b):
    n_batch, seq, d = xs.shape
    pad = GATE_PAD - N_EXPERTS
    rw = jnp.pad(router_w.astype(F32), ((0, 0), (0, pad)))
    rb = jnp.pad(router_b.astype(F32), (0, pad), constant_values=NEG).reshape(1, GATE_PAD)
    r = lax.broadcasted_iota(jnp.int32, (ROW_TILE, ROW_TILE), 0)
    c = lax.broadcasted_iota(jnp.int32, (ROW_TILE, ROW_TILE), 1)
    tri = (c < r).astype(BF16)
    mod_spec = pl.BlockSpec((None, 1, d), _mod_index_map(n_batch, seq // ROW_TILE))
    tok = lambda w: pl.BlockSpec((None, ROW_TILE, w), lambda b, j: (b, j, 0))
    return pl.pallas_call(
        _moe_route_kernel,
        grid=(n_batch, seq // ROW_TILE),
        in_specs=[
            tok(d),
            pl.BlockSpec((1, d), lambda b, j: (0, 0)),
            mod_spec, mod_spec,
            pl.BlockSpec((d, GATE_PAD), lambda b, j: (0, 0)),
            pl.BlockSpec((1, GATE_PAD), lambda b, j: (0, 0)),
            pl.BlockSpec((ROW_TILE, ROW_TILE), lambda b, j: (0, 0)),
        ],
        out_specs=[tok(d), tok(GATE_PAD), tok(GATE_PAD), tok(GATE_PAD),
                   pl.BlockSpec((1, GATE_PAD), lambda b, j: (0, 0))],
        out_shape=[jax.ShapeDtypeStruct((n_batch, seq, d), F32),
                   jax.ShapeDtypeStruct((n_batch, seq, GATE_PAD), jnp.int32),
                   jax.ShapeDtypeStruct((n_batch, seq, GATE_PAD), F32),
                   jax.ShapeDtypeStruct((n_batch, seq, GATE_PAD), jnp.int32),
                   jax.ShapeDtypeStruct((1, GATE_PAD), F32)],
        scratch_shapes=[pltpu.VMEM((1, GATE_PAD), F32)],
        compiler_params=pltpu.CompilerParams(
            dimension_semantics=("arbitrary", "arbitrary"), vmem_limit_bytes=VMEM_LIMIT),
        name="moe_route",
    )(xs, gain.reshape(1, d), sc_tab, sh_tab, rw, rb, tri)


def _out_proj_kernel(*refs, kinds):
    x_ref, refs = refs[0], refs[1:]
    gains_ref, g_ref, heads_ref, w_ref, o_ref = refs[-5:]
    heads = heads_ref[...] * (1.0 / HG_DK)
    acc = jnp.zeros(o_ref.shape, F32)
    pos = 0
    for i, kind in enumerate(kinds):
        if kind == "final":
            m = refs[pos][...]
            pos += 1
        else:
            o = refs[pos][...] + refs[pos + 1][...]
            gate = refs[pos + 2][...]
            pos += 3
            ms = _dot_sel(o * o, heads, 1, 3)
            o = o * lax.rsqrt(ms + EPS) * gains_ref[i:i + 1, :]
            m = o * (jax.nn.silu(gate) if kind == "silu" else jax.nn.sigmoid(gate))
        acc += jnp.dot(m.astype(BF16), w_ref[i * W_GROUP:(i + 1) * W_GROUP, :],
                       preferred_element_type=F32)
    o_ref[...] = x_ref[...] + g_ref[...] * acc


def _out_proj(xs, mixers, gains, gate_tab, w_out):
    n_batch, seq, d = xs.shape
    row = lambda w, cb=0: pl.BlockSpec((None, ROW_TILE, w), lambda b, j: (b, j, cb))
    args, specs, kinds = [], [], []
    for m in mixers:
        kinds.append(m[0])
        if m[0] == "final":
            args.append(m[1])
            specs.append(row(W_GROUP))
        else:
            src, cb = m[3]
            args += [m[1], m[2], src]
            specs += [row(W_GROUP), row(W_GROUP), row(W_GROUP, cb)]
    heads = _head_mask()
    return pl.pallas_call(
        functools.partial(_out_proj_kernel, kinds=tuple(kinds)),
        grid=(n_batch, seq // ROW_TILE),
        in_specs=[row(d)] + specs + [
            pl.BlockSpec((N_MIXERS, W_GROUP), lambda b, j: (0, 0)),
            pl.BlockSpec((None, 1, d), _mod_index_map(n_batch, seq // ROW_TILE)),
            pl.BlockSpec((W_GROUP, W_GROUP), lambda b, j: (0, 0)),
            pl.BlockSpec((d, d), lambda b, j: (0, 0)),
        ],
        out_specs=row(d),
        out_shape=jax.ShapeDtypeStruct((n_batch, seq, d), F32),
        compiler_params=pltpu.CompilerParams(
            dimension_semantics=("parallel", "parallel"), vmem_limit_bytes=VMEM_LIMIT),
        name="out_proj",
    )(xs, *args, gains, gate_tab, heads, w_out)


def _row_copy(src, src_row, dst, dst_row, sem):
    return pltpu.make_async_copy(src.at[pl.ds(src_row, 1), :], dst.at[pl.ds(dst_row, 1), :], sem)


def _dispatch_kernel(tail_ref, dest_ref, h_ref, slots_hbm, zeros_v, sem, zero_sem):
    def fill(start):
        start = pl.multiple_of(start, MOE_TILE)
        return pltpu.make_async_copy(zeros_v, slots_hbm.at[pl.ds(start, MOE_TILE), :], zero_sem)

    def each_unused_tile(fn):
        first = tail_ref[N_EXPERTS] // MOE_TILE
        lax.fori_loop(first, slots_hbm.shape[0] // MOE_TILE, lambda t, c: (fn(fill(t * MOE_TILE)), c)[1], 0)

    @pl.when(pl.program_id(0) == 0)
    def _():
        zeros_v[...] = jnp.zeros_like(zeros_v)
        for e in range(N_EXPERTS):
            @pl.when(tail_ref[e] >= 0)
            def _():
                fill(tail_ref[e]).start()
        each_unused_tile(lambda copy: copy.start())
        for e in range(N_EXPERTS):
            @pl.when(tail_ref[e] >= 0)
            def _():
                fill(tail_ref[e]).wait()
        each_unused_tile(lambda copy: copy.wait())

    def issue(r, carry):
        for k in range(TOP_K):
            _row_copy(h_ref, r, slots_hbm, dest_ref[0, r * TOP_K + k], sem).start(priority=k % 2)
        return carry

    lax.fori_loop(0, ROW_TILE, issue, 0, unroll=4)
    for k in range(TOP_K):
        pltpu.make_async_copy(h_ref, slots_hbm.at[pl.ds(0, ROW_TILE), :], sem).wait()


def _moe_dispatch(h, dest, tail, n_slots):
    t_count, d = h.shape
    return pl.pallas_call(
        _dispatch_kernel,
        grid_spec=pltpu.PrefetchScalarGridSpec(
            num_scalar_prefetch=1,
            grid=(t_count // ROW_TILE,),
            in_specs=[pl.BlockSpec((None, 1, ROW_TILE * TOP_K), lambda i, tl: (i, 0, 0),
                                   memory_space=pltpu.SMEM),
                      pl.BlockSpec((ROW_TILE, d), lambda i, tl: (i, 0))],
            out_specs=pl.BlockSpec(memory_space=pl.ANY),
            scratch_shapes=[pltpu.VMEM((MOE_TILE, d), F32), pltpu.SemaphoreType.DMA(()),
                            pltpu.SemaphoreType.DMA(())],
        ),
        out_shape=jax.ShapeDtypeStruct((n_slots, d), F32),
        compiler_params=pltpu.CompilerParams(
            dimension_semantics=("arbitrary",), vmem_limit_bytes=VMEM_LIMIT),
        name="moe_dispatch",
    )(tail, dest, h)


def _expert_kernel(bexp_ref, nused_ref, x_ref, w1_ref, b1_ref, w2_ref, b2_ref, o_ref, w1b, w2b):
    i = pl.program_id(0)
    changed = jnp.logical_or(i == 0, bexp_ref[i] != bexp_ref[jnp.maximum(i - 1, 0)])
    used = i < nused_ref[0]

    @pl.when(jnp.logical_and(changed, used))
    def _():
        w1b[...] = w1_ref[...].astype(BF16)
        w2b[...] = w2_ref[...].astype(BF16)

    @pl.when(used)
    def _():
        f = w2b.shape[0]
        u = jnp.dot(x_ref[...].astype(BF16), w1b[...], preferred_element_type=F32) + b1_ref[...]
        gate = jnp.minimum(u[:, :f], SWIGLU_LIMIT)
        lin = jnp.clip(u[:, f:], -SWIGLU_LIMIT, SWIGLU_LIMIT)
        y = (lin + 1.0) * gate * jax.nn.sigmoid(SWIGLU_ALPHA * gate)
        o_ref[...] = jnp.dot(y.astype(BF16), w2b[...], preferred_element_type=F32) + b2_ref[...]

    @pl.when(jnp.logical_not(used))
    def _():
        o_ref[...] = jnp.zeros_like(o_ref)


def _expert_ffn(block_exp, n_used, xg, layer, w1, b1, w2, b2):
    n_slots, d = xg.shape
    _, n_exp, _, f2 = w1.shape
    f = f2 // 2
    n_blocks = n_slots // MOE_TILE
    return pl.pallas_call(
        _expert_kernel,
        grid_spec=pltpu.PrefetchScalarGridSpec(
            num_scalar_prefetch=2,
            grid=(n_blocks,),
            in_specs=[
                pl.BlockSpec((MOE_TILE, d), lambda i, be, nu: (jnp.minimum(i, nu[0] - 1), 0)),
                pl.BlockSpec((None, None, d, f2), lambda i, be, nu: (layer, be[i], 0, 0)),
                pl.BlockSpec((None, 1, f2), lambda i, be, nu: (be[i], 0, 0)),
                pl.BlockSpec((None, None, f, d), lambda i, be, nu: (layer, be[i], 0, 0)),
                pl.BlockSpec((None, 1, d), lambda i, be, nu: (be[i], 0, 0)),
            ],
            out_specs=pl.BlockSpec((MOE_TILE, d), lambda i, be, nu: (i, 0)),
            scratch_shapes=[pltpu.VMEM((d, f2), BF16), pltpu.VMEM((f, d), BF16)],
        ),
        out_shape=jax.ShapeDtypeStruct((n_slots, d), F32),
        compiler_params=pltpu.CompilerParams(
            dimension_semantics=("arbitrary",), vmem_limit_bytes=VMEM_LIMIT),
        name="expert_ffn",
    )(block_exp, n_used, xg, w1, b1.reshape(n_exp, 1, f2), w2, b2.reshape(n_exp, 1, d))


def _combine_kernel(dest_ref, next_ref, y_hbm, x_ref, gate_ref, g_ref, fg_ref, o_ref, ybuf, sem, *, final):
    step = pl.program_id(0) * pl.num_programs(1) + pl.program_id(1)
    n_steps = pl.num_programs(0) * pl.num_programs(1)
    slot = step % 2

    def gather(idx_ref, into):
        def issue(r, carry):
            for k in range(TOP_K):
                pltpu.make_async_copy(y_hbm.at[pl.ds(idx_ref[0, r * TOP_K + k], 1), :],
                                      ybuf.at[into, k, pl.ds(r, 1), :], sem.at[into]).start(priority=k % 2)
            return carry

        lax.fori_loop(0, ROW_TILE, issue, 0, unroll=4)

    @pl.when(step == 0)
    def _():
        gather(dest_ref, 0)

    @pl.when(step + 1 < n_steps)
    def _():
        gather(next_ref, 1 - slot)

    for k in range(TOP_K):
        pltpu.make_async_copy(y_hbm.at[pl.ds(0, ROW_TILE), :], ybuf.at[slot, k], sem.at[slot]).wait()
    acc = jnp.zeros(o_ref.shape, F32)
    for k in range(TOP_K):
        acc += gate_ref[:, k:k + 1] * ybuf[slot, k]
    out = x_ref[...] + g_ref[...] * acc
    if final:
        out = out * lax.rsqrt(jnp.mean(out * out, -1, keepdims=True) + EPS) * fg_ref[...]
    o_ref[...] = out


def _moe_combine(xs, y_slots, dest, gates, gate_tab, final_gain=None, n_out=None):
    n_batch, seq, d = xs.shape
    n_tiles = seq // ROW_TILE
    final = final_gain is not None
    n_out = n_out if final else seq
    fg = (final_gain if final else jnp.ones((d,), F32)).astype(F32).reshape(1, d)
    tok = lambda w: pl.BlockSpec((None, ROW_TILE, w), lambda b, j: (b, j, 0))
    n_out_tiles = n_out // ROW_TILE

    def next_tile(b, j):
        nxt = jnp.minimum(b * n_out_tiles + j + 1, n_batch * n_out_tiles - 1)
        return ((nxt // n_out_tiles) * n_tiles + nxt % n_out_tiles, 0, 0)

    return pl.pallas_call(
        functools.partial(_combine_kernel, final=final),
        grid=(n_batch, n_out_tiles),
        in_specs=[pl.BlockSpec((None, 1, ROW_TILE * TOP_K), lambda b, j: (b * n_tiles + j, 0, 0),
                               memory_space=pltpu.SMEM),
                  pl.BlockSpec((None, 1, ROW_TILE * TOP_K), next_tile, memory_space=pltpu.SMEM),
                  pl.BlockSpec(memory_space=pl.ANY),
                  tok(d), tok(GATE_PAD),
                  pl.BlockSpec((None, 1, d), _mod_index_map(n_batch, n_tiles)),
                  pl.BlockSpec((1, d), lambda b, j: (0, 0))],
        out_specs=tok(d),
        out_shape=jax.ShapeDtypeStruct((n_batch, n_out, d), F32),
        scratch_shapes=[pltpu.VMEM((2, TOP_K, ROW_TILE, d), F32), pltpu.SemaphoreType.DMA((2,))],
        compiler_params=pltpu.CompilerParams(
            dimension_semantics=("arbitrary", "arbitrary"), vmem_limit_bytes=VMEM_LIMIT),
        name="moe_combine",
    )(dest, dest, y_slots, xs, gates, gate_tab, fg)


def _moe_layer(xs, gain, sc_tab, sh_tab, gate_tab, router_w, router_b, layer, w1, b1, w2, b2,
               final_gain=None, n_out=None):
    n_batch, seq, d = xs.shape
    t_count = n_batch * seq
    h, idx, gates, rank, counts = _moe_route(xs, gain, sc_tab, sh_tab, router_w, router_b)
    counts = counts[0, :N_EXPERTS].astype(jnp.int32)
    padded = (counts + MOE_TILE - 1) // MOE_TILE * MOE_TILE
    pend = jnp.cumsum(padded)
    pstart = pend - padded
    top_i = idx.reshape(t_count, GATE_PAD)[:, :TOP_K]
    hit = top_i[:, :, None] == jnp.arange(N_EXPERTS, dtype=jnp.int32)
    dest = rank.reshape(t_count, GATE_PAD)[:, :TOP_K] + jnp.sum(jnp.where(hit, pstart, 0), -1)
    dest = dest.astype(jnp.int32).reshape(t_count // ROW_TILE, 1, ROW_TILE * TOP_K)
    n_blocks = -(-(t_count * TOP_K + N_EXPERTS * (MOE_TILE - 1)) // MOE_TILE)
    tile_start = jnp.arange(n_blocks, dtype=jnp.int32)[:, None] * MOE_TILE
    block_exp = jnp.minimum(jnp.sum((pend[None, :] <= tile_start).astype(jnp.int32), -1), N_EXPERTS - 1)
    n_used = (pend[-1:] // MOE_TILE).astype(jnp.int32)
    tail = jnp.concatenate([jnp.where(padded > 0, pend - MOE_TILE, -1), pend[-1:]]).astype(jnp.int32)
    slots = _moe_dispatch(h.reshape(t_count, d), dest, tail, n_blocks * MOE_TILE)
    y_slots = _expert_ffn(block_exp, n_used, slots, layer, w1, b1, w2, b2)
    return _moe_combine(xs, y_slots, dest, gates, gate_tab, final_gain, n_out)


HG_SUB = 16


def _hg_consts():
    r = lax.broadcasted_iota(jnp.int32, (ROW_TILE, ROW_TILE), 0)
    c = lax.broadcasted_iota(jnp.int32, (ROW_TILE, ROW_TILE), 1)
    same_chunk = (r // HG_SUB) == (c // HG_SUB)
    tri_f = jnp.where(same_chunk & (c <= r), 1.0, 0.0).astype(F32)
    tri_b = jnp.where(same_chunk & (c >= r), 1.0, 0.0).astype(F32)
    blk = jnp.where(same_chunk, 1.0, 0.0).astype(F32)
    return tri_f, tri_b, blk


def _hg_prepare(z_ref, zf_col, lb, tri, blk, qs_s, key_s, cm_s, qh_s, kh_s, dec_s):
    q = jax.nn.silu(z_ref[:, 0:W_GROUP])
    zf = z_ref[:, zf_col:zf_col + W_GROUP]
    f = lb + (1.0 - lb) * jax.nn.sigmoid(zf)
    key = (1.0 - lb) * jax.nn.sigmoid(-zf)
    logf = jnp.log(f)
    cum = _dot_sel(tri, logf, 0, 3)
    tot = _dot_sel(blk, logf, 0, 3)
    qs_s[...] = q
    key_s[...] = key
    cm_s[...] = cum
    qh_s[...] = (q * jnp.exp(cum)).astype(BF16)
    kh_s[...] = (key * jnp.exp(tot - cum)).astype(BF16)
    dec_s[...] = jnp.exp(tot)


def _hg_chunk(c, reverse, z_ref, o_ref, st_ref, heads_bf, heads_f, qs_s, key_s, cm_s, qh_s, kh_s, dec_s):
    sl = pl.ds(pl.multiple_of(c * HG_SUB, HG_SUB), HG_SUB)
    q, k, cm = qs_s[sl, :], key_s[sl, :], cm_s[sl, :]
    v = z_ref[sl, W_GROUP:2 * W_GROUP]
    st = st_ref[...]
    o = lax.dot_general(qh_s[sl, :], st.astype(BF16), (((1,), (1,)), ((), ())),
                        preferred_element_type=F32)
    t_idx = lax.broadcasted_iota(jnp.int32, (HG_SUB, W_GROUP), 0)
    parts = []
    for s in range(HG_SUB):
        live = (t_idx <= s) if reverse else (t_idx >= s)
        e = jnp.exp(jnp.where(live, cm - cm[s:s + 1, :], NEG))
        parts.append((q * k[s:s + 1, :] * e).astype(BF16))
    r = jnp.dot(jnp.concatenate(parts, 0), heads_bf, preferred_element_type=F32)
    for s in range(HG_SUB):
        o += r[s * HG_SUB:(s + 1) * HG_SUB, :] * v[s:s + 1, :]
    o_ref[sl, :] = o
    ut = lax.dot_general(v.astype(BF16), kh_s[sl, :], (((0,), (0,)), ((), ())),
                         preferred_element_type=F32)
    st_ref[...] = st * dec_s[sl, :][0:1, :] + ut * heads_f


def _hgrn2_kernel(zf_ref, zb_ref, lb_ref, trif_ref, trib_ref, blk_ref, heads_ref, of_ref, ob_ref,
                  st_s, qs_s, key_s, cm_s, qh_s, kh_s, dec_s):
    @pl.when(pl.program_id(1) == 0)
    def _():
        st_s[...] = jnp.zeros_like(st_s)

    lb = lb_ref[...]
    blk = blk_ref[...]
    heads_f = heads_ref[...]
    heads_bf = heads_f.astype(BF16)
    _hg_prepare(zf_ref, 2 * W_GROUP, lb, trif_ref[...], blk, qs_s.at[0], key_s.at[0], cm_s.at[0],
                qh_s.at[0], kh_s.at[0], dec_s.at[0])
    _hg_prepare(zb_ref, 3 * W_GROUP, lb, trib_ref[...], blk, qs_s.at[1], key_s.at[1], cm_s.at[1],
                qh_s.at[1], kh_s.at[1], dec_s.at[1])
    n_chunks = ROW_TILE // HG_SUB

    def body(c, carry):
        _hg_chunk(c, False, zf_ref, of_ref, st_s.at[0], heads_bf, heads_f, qs_s.at[0], key_s.at[0],
                  cm_s.at[0], qh_s.at[0], kh_s.at[0], dec_s.at[0])
        _hg_chunk(n_chunks - 1 - c, True, zb_ref, ob_ref, st_s.at[1], heads_bf, heads_f, qs_s.at[1],
                  key_s.at[1], cm_s.at[1], qh_s.at[1], kh_s.at[1], dec_s.at[1])
        return carry

    lax.fori_loop(0, n_chunks, body, 0)


def _scan_tiles(n_lat_tiles):
    fwd = lambda j: jnp.where(j == 0, n_lat_tiles, j - 1)
    bwd = lambda j: jnp.where(j == 0, n_lat_tiles, n_lat_tiles - j)
    return fwd, bwd


def _hgrn2_scan(z_hg, lb):
    n_batch, seq, _ = z_hg.shape
    n_tiles = seq // ROW_TILE
    fwd, bwd = _scan_tiles(n_tiles - 1)
    const = pl.BlockSpec((ROW_TILE, ROW_TILE), lambda b, j: (0, 0))
    scr = lambda dt: pltpu.VMEM((2, ROW_TILE, W_GROUP), dt)
    return pl.pallas_call(
        _hgrn2_kernel,
        grid=(n_batch, n_tiles),
        in_specs=[pl.BlockSpec((None, ROW_TILE, HG_COLS), lambda b, j: (b, fwd(j), 0)),
                  pl.BlockSpec((None, ROW_TILE, HG_COLS), lambda b, j: (b, bwd(j), 0)),
                  pl.BlockSpec((1, W_GROUP), lambda b, j: (0, 0)),
                  const, const, const,
                  pl.BlockSpec((W_GROUP, W_GROUP), lambda b, j: (0, 0))],
        out_specs=[pl.BlockSpec((None, ROW_TILE, W_GROUP), lambda b, j: (b, fwd(j), 0)),
                   pl.BlockSpec((None, ROW_TILE, W_GROUP), lambda b, j: (b, bwd(j), 0))],
        out_shape=[jax.ShapeDtypeStruct((n_batch, seq, W_GROUP), F32)] * 2,
        scratch_shapes=[pltpu.VMEM((2, W_GROUP, W_GROUP), F32), scr(F32), scr(F32), scr(F32),
                        scr(BF16), scr(BF16), scr(F32)],
        compiler_params=pltpu.CompilerParams(
            dimension_semantics=("parallel", "arbitrary"), vmem_limit_bytes=VMEM_LIMIT),
        name="hgrn2_scan",
    )(z_hg, z_hg, lb.reshape(1, W_GROUP), *_hg_consts(), _head_mask())


def _dot_hi(a, b):
    return jnp.dot(a, b, preferred_element_type=F32, precision=lax.Precision.HIGHEST)


def _bf16_terms(x, terms):
    parts = []
    for _ in range(terms):
        p = x.astype(BF16)
        parts.append(p)
        x = x - p.astype(F32)
    return parts


def _dot_sel(a, b, select, terms, nt=False):
    ops = [a, b]
    fixed = ops[select].astype(BF16)
    dims = (((1,), (1,)), ((), ())) if nt else (((1,), (0,)), ((), ()))
    out = None
    for p in _bf16_terms(ops[1 - select], terms):
        pair = (fixed, p) if select == 0 else (p, fixed)
        d = lax.dot_general(pair[0], pair[1], dims, preferred_element_type=F32)
        out = d if out is None else out + d
    return out


def _dot_nt(a, b, precision=None):
    return lax.dot_general(a, b, (((1,), (1,)), ((), ())), preferred_element_type=F32,
                           precision=precision)


def _dot_tn(a, b):
    return lax.dot_general(a, b, (((0,), (0,)), ((), ())), preferred_element_type=F32)


def _chunk_consts(chunk, n_heads, gate_cols):
    r = lax.broadcasted_iota(jnp.int32, (chunk, chunk), 0)
    c = lax.broadcasted_iota(jnp.int32, (chunk, chunk), 1)
    low = (c <= r).astype(F32)
    upp = (c >= r).astype(F32)
    dh = W_GROUP // n_heads
    hr = lax.broadcasted_iota(jnp.int32, (W_GROUP, W_GROUP), 0) // dh
    hc = lax.broadcasted_iota(jnp.int32, (W_GROUP, W_GROUP), 1) // dh
    heads = (hr == hc).astype(F32)
    gc = lax.broadcasted_iota(jnp.int32, (GATE_PAD, W_GROUP), 0)
    gh = lax.broadcasted_iota(jnp.int32, (GATE_PAD, W_GROUP), 1) // dh
    expand = jnp.stack([(gc == g0 + gh).astype(F32) for g0 in gate_cols])
    sr = lax.broadcasted_iota(jnp.int32, (8, GATE_PAD), 0)
    sc = lax.broadcasted_iota(jnp.int32, (8, GATE_PAD), 1)
    select = jnp.stack([((sc == g0 + sr) & (sr < n_heads)).astype(F32) for g0 in gate_cols])
    return low, upp, heads, expand, select


class _MlDir:
    def __init__(self, reverse, z_ref, g_ref, gb, c_ref, n_ref, m_ref, low, upp, heads,
                 ex_i, ex_f, sel_i, sel_f):
        self.q = z_ref[:, 0:W_GROUP]
        self.k = z_ref[:, W_GROUP:2 * W_GROUP] * (ML_DH ** -0.5)
        self.v = z_ref[:, 2 * W_GROUP:3 * W_GROUP]
        gates = g_ref[...] + gb
        col_sum, row_sum = (upp, low) if reverse else (low, upp)
        self.li = _dot_sel(gates, ex_i, 1, 3)
        lf = jax.nn.log_sigmoid(_dot_sel(gates, ex_f, 1, 3))
        self.li_rows = _dot_sel(sel_i, gates, 0, 3, nt=True)
        lf_rows = jax.nn.log_sigmoid(_dot_sel(sel_f, gates, 0, 3, nt=True))
        self.fcum = _dot_sel(col_sum, lf, 0, 3)
        self.fcum_rows = _dot_sel(lf_rows, row_sum, 1, 3)
        last = 0 if reverse else ML_CHUNK - 1
        self.ftot = self.fcum[last:last + 1, :]
        g = self.ftot - self.fcum + self.li
        self.g_max = jnp.max(g, axis=0, keepdims=True)
        self.kw = self.k * jnp.exp(g - self.g_max)
        self.c0, self.n0, self.m0 = c_ref[...], n_ref[...], m_ref[...]
        self.k_bf = self.k.astype(BF16)
        self.qc = jnp.dot(self.q.astype(BF16), self.c0.astype(BF16), preferred_element_type=F32)
        self.qn = jnp.dot((self.q * self.n0).astype(BF16), heads.astype(BF16), preferred_element_type=F32)
        self.a_init = self.fcum + self.m0
        rr = lax.broadcasted_iota(jnp.int32, (ML_CHUNK, ML_CHUNK), 0)
        cc = lax.broadcasted_iota(jnp.int32, (ML_CHUNK, ML_CHUNK), 1)
        self.live = (cc >= rr) if reverse else (cc <= rr)


def _mlstm_kernel(zf_ref, zb_ref, gf_ref, gb_ref, bias_ref, low_ref, upp_ref, heads_ref, ex_ref, sel_ref,
                  of_ref, ob_ref, c_s, n_s, m_s):
    @pl.when(pl.program_id(1) == 0)
    def _():
        c_s[...] = jnp.zeros_like(c_s)
        n_s[...] = jnp.zeros_like(n_s)
        m_s[...] = jnp.zeros_like(m_s)

    low, upp, heads, bias = low_ref[...], upp_ref[...], heads_ref[...], bias_ref[...]
    dirs = [_MlDir(False, zf_ref, gf_ref, bias, c_s.at[0], n_s.at[0], m_s.at[0], low, upp, heads,
                   ex_ref[0], ex_ref[1], sel_ref[0], sel_ref[1]),
            _MlDir(True, zb_ref, gb_ref, bias, c_s.at[1], n_s.at[1], m_s.at[1], low, upp, heads,
                   ex_ref[2], ex_ref[3], sel_ref[2], sel_ref[3])]
    lane_head = lax.broadcasted_iota(jnp.int32, (ML_CHUNK, W_GROUP), 1) // ML_DH
    hs = lambda h: slice(h * ML_DH, (h + 1) * ML_DH)
    probs = [(d, h) for d in dirs for h in range(ML_HEADS)]
    qks = [_dot_nt(jnp.where(lane_head == h, d.q, 0.0).astype(BF16), d.k_bf) for d, h in probs]
    a = [jnp.where(d.live, d.fcum[:, hs(h)] - d.fcum_rows[h:h + 1, :] + d.li_rows[h:h + 1, :], NEG)
         for d, h in probs]
    m_t = [jnp.maximum(jnp.max(a_h, axis=1, keepdims=True), d.a_init[:, hs(h)])
           for (d, h), a_h in zip(probs, a)]
    p = [jnp.exp(a_h - m) * qk for a_h, m, qk in zip(a, m_t, qks)]
    pv = [jnp.dot(p_h.astype(BF16), d.v[:, hs(h)].astype(BF16), preferred_element_type=F32)
          for (d, h), p_h in zip(probs, p)]
    outs = []
    for (d, h), p_h, pv_h, m in zip(probs, p, pv, m_t):
        e0 = jnp.exp(d.a_init[:, hs(h)] - m)
        num = pv_h + e0 * d.qc[:, hs(h)]
        den = jnp.sum(p_h, axis=1, keepdims=True) + e0 * d.qn[:, hs(h)]
        outs.append(num / jnp.maximum(jnp.abs(den), jnp.exp(-m)))
    u_c = [_dot_tn(d.kw.astype(BF16), d.v.astype(BF16)) * heads for d in dirs]
    for i, (d, o_ref) in enumerate(zip(dirs, (of_ref, ob_ref))):
        o_ref[...] = jnp.concatenate(outs[i * ML_HEADS:(i + 1) * ML_HEADS], axis=1)
        m_new = jnp.maximum(d.ftot + d.m0, d.g_max)
        a_old = jnp.exp(d.ftot + d.m0 - m_new)
        a_new = jnp.exp(d.g_max - m_new)
        c_s[i] = a_old * d.c0 + a_new * u_c[i]
        n_s[i] = a_old * d.n0 + a_new * jnp.sum(d.kw, axis=0, keepdims=True)
        m_s[i] = m_new


def _scan_chunks(n_lat, n_ctx):
    fwd = lambda j: jnp.where(j < n_ctx, n_lat + j, j - n_ctx)
    bwd = lambda j: jnp.where(j < n_ctx, n_lat + n_ctx - 1 - j, n_lat + n_ctx - 1 - j)
    return fwd, bwd


def _mlstm_scan(z_ml, z_gt, gate_b, ctx_len):
    n_batch, seq, _ = z_ml.shape
    n_chunks = seq // ML_CHUNK
    n_ctx = ctx_len // ML_CHUNK
    fwd, bwd = _scan_chunks(n_chunks - n_ctx, n_ctx)
    low, upp, heads, expand, select = _chunk_consts(ML_CHUNK, ML_HEADS, (0, 4, 8, 12))
    bias = jnp.pad(gate_b.astype(F32).reshape(1, ML_GATES), ((0, 0), (0, GATE_PAD - ML_GATES)))
    full = lambda a: pl.BlockSpec(a.shape, lambda b, j: (0,) * a.ndim)
    return pl.pallas_call(
        _mlstm_kernel,
        grid=(n_batch, n_chunks),
        in_specs=[pl.BlockSpec((None, ML_CHUNK, ML_MAIN), lambda b, j: (b, fwd(j), 0)),
                  pl.BlockSpec((None, ML_CHUNK, ML_MAIN), lambda b, j: (b, bwd(j), 0)),
                  pl.BlockSpec((None, ML_CHUNK, GATE_PAD), lambda b, j: (b, fwd(j), 0)),
                  pl.BlockSpec((None, ML_CHUNK, GATE_PAD), lambda b, j: (b, bwd(j), 0)),
                  full(bias), full(low), full(upp), full(heads), full(expand), full(select)],
        out_specs=[pl.BlockSpec((None, ML_CHUNK, W_GROUP), lambda b, j: (b, fwd(j), 0)),
                   pl.BlockSpec((None, ML_CHUNK, W_GROUP), lambda b, j: (b, bwd(j), 0))],
        out_shape=[jax.ShapeDtypeStruct((n_batch, seq, W_GROUP), F32)] * 2,
        scratch_shapes=[pltpu.VMEM((2, W_GROUP, W_GROUP), F32), pltpu.VMEM((2, 1, W_GROUP), F32),
                        pltpu.VMEM((2, 1, W_GROUP), F32)],
        compiler_params=pltpu.CompilerParams(
            dimension_semantics=("parallel", "arbitrary"), vmem_limit_bytes=VMEM_LIMIT),
        name="mlstm_scan",
    )(z_ml, z_ml, z_gt, z_gt, bias, low, upp, heads, expand, select)


HALO = 8


def _short_conv_kernel(x_ref, prev_ref, next_ref, w_ref, b_ref, heads_ref, o_ref, *, n_lat_tiles, gdn):
    j = pl.program_id(1)
    n_tiles = pl.num_programs(1)
    x = x_ref[...]
    has_prev = jnp.logical_and(j != 0, j != n_lat_tiles)
    has_next = jnp.logical_and(j != n_lat_tiles - 1, j != n_tiles - 1)
    prev_row = jnp.where(has_prev, prev_ref[HALO - 1:HALO, :], 0.0)
    next_row = jnp.where(has_next, next_ref[0:1, :], 0.0)
    row = lax.broadcasted_iota(jnp.int32, x.shape, 0)
    x_prev = jnp.where(row == 0, prev_row, pltpu.roll(x, 1, 0))
    x_next = jnp.where(row == ROW_TILE - 1, next_row, pltpu.roll(x, ROW_TILE - 1, 0))
    y = x_prev * w_ref[0:1, :] + x * w_ref[1:2, :] + x_next * w_ref[2:3, :] + b_ref[...]
    if not gdn:
        o_ref[...] = y
        return
    y = jax.nn.silu(y)
    heads = heads_ref[...]
    for i in range(2):
        t = y[:, i * W_GROUP:(i + 1) * W_GROUP]
        o_ref[:, i * W_GROUP:(i + 1) * W_GROUP] = t * lax.rsqrt(_dot_sel(t * t, heads, 1, 3) + EPS)
    o_ref[:, 2 * W_GROUP:] = y[:, 2 * W_GROUP:]


def _head_mask():
    r = lax.broadcasted_iota(jnp.int32, (W_GROUP, W_GROUP), 0) // HG_DK
    c = lax.broadcasted_iota(jnp.int32, (W_GROUP, W_GROUP), 1) // HG_DK
    return (r == c).astype(F32)


def _short_conv_tiles(z, conv_w, conv_b, n_lat, gdn):
    n_batch, seq, _ = z.shape
    c = conv_w.shape[1]
    n_tiles = seq // ROW_TILE
    per = ROW_TILE // HALO
    n_halo = seq // HALO
    return pl.pallas_call(
        functools.partial(_short_conv_kernel, n_lat_tiles=n_lat // ROW_TILE, gdn=gdn),
        grid=(n_batch, n_tiles),
        in_specs=[pl.BlockSpec((None, ROW_TILE, c), lambda b, j: (b, j, 0)),
                  pl.BlockSpec((None, HALO, c), lambda b, j: (b, jnp.maximum(j * per - 1, 0), 0)),
                  pl.BlockSpec((None, HALO, c), lambda b, j: (b, jnp.minimum((j + 1) * per, n_halo - 1), 0)),
                  pl.BlockSpec((3, c), lambda b, j: (0, 0)),
                  pl.BlockSpec((1, c), lambda b, j: (0, 0)),
                  pl.BlockSpec((W_GROUP, W_GROUP), lambda b, j: (0, 0))],
        out_specs=pl.BlockSpec((None, ROW_TILE, c), lambda b, j: (b, j, 0)),
        out_shape=jax.ShapeDtypeStruct((n_batch, seq, c), F32),
        compiler_params=pltpu.CompilerParams(
            dimension_semantics=("parallel", "parallel"), vmem_limit_bytes=VMEM_LIMIT),
        name="gdn_conv" if gdn else "hyena_conv",
    )(z, z, z, conv_w.astype(F32), conv_b.astype(F32).reshape(1, c), _head_mask())


def _gd_consts():
    r = lax.broadcasted_iota(jnp.int32, (GD_CHUNK, GD_CHUNK), 0)
    c = lax.broadcasted_iota(jnp.int32, (GD_CHUNK, GD_CHUNK), 1)
    same = lambda n: (r // n) == (c // n)
    eye = (r == c).astype(F32)
    masks = [same(8).astype(F32)] + [(same(2 * b) & ~same(b)).astype(F32) for b in (8, 16, 32)]
    return eye, jnp.stack(masks)


def _unit_tri_inverses(ns, eye, masks):
    mm = lambda a, b: jnp.dot(a.astype(BF16), b.astype(BF16), preferred_element_type=F32)
    n8 = [n * masks[0] for n in ns]
    n2 = [mm(a, a) for a in n8]
    n4 = [mm(a, a) for a in n2]
    ts = [mm(eye - a, eye + b) for a, b in zip(n8, n2)]
    ts = [mm(t, eye + b) for t, b in zip(ts, n4)]
    for i in range(1, 4):
        lts = [mm(n * masks[i], t) for n, t in zip(ns, ts)]
        ts = [t - mm(t, lt) for t, lt in zip(ts, lts)]
    return ts


class _GdDir:
    def __init__(self, reverse, x_ref, g_ref, s_ref, prm_lane, prm_row, low, upp, ex_a, ex_b, sel_a):
        self.q = x_ref[:, 0:W_GROUP] * (GD_DH ** -0.5)
        self.k = x_ref[:, W_GROUP:2 * W_GROUP]
        self.v = x_ref[:, 2 * W_GROUP:3 * W_GROUP]
        gates = g_ref[...]
        col_sum, row_sum = (upp, low) if reverse else (low, upp)
        log_a = prm_lane[0:1, :] * jax.nn.softplus(_dot_sel(gates, ex_a, 1, 3) + prm_lane[1:2, :])
        self.beta = jax.nn.sigmoid(_dot_sel(gates, ex_b, 1, 3))
        log_a_rows = prm_row[0] * jax.nn.softplus(_dot_sel(sel_a, gates, 0, 3, nt=True) + prm_row[1])
        self.g = _dot_sel(col_sum, log_a, 0, 3)
        self.g_rows = _dot_sel(log_a_rows, row_sum, 1, 3)
        last = 0 if reverse else GD_CHUNK - 1
        self.g_last = self.g[last:last + 1, :]
        self.eg = jnp.exp(self.g)
        self.k_bf = self.k.astype(BF16)
        self.kb = self.k * self.beta
        self.s0 = s_ref[...]
        rr = lax.broadcasted_iota(jnp.int32, (GD_CHUNK, GD_CHUNK), 0)
        cc = lax.broadcasted_iota(jnp.int32, (GD_CHUNK, GD_CHUNK), 1)
        self.live = (cc >= rr) if reverse else (cc <= rr)
        self.strict = (cc > rr) if reverse else (cc < rr)


def _gdn_kernel(xf_ref, xb_ref, gf_ref, gb_ref, prm_lane_ref, prm_row_ref, low_ref, upp_ref, heads_ref,
                ex_ref, sel_ref, eye_ref, masks_ref, of_ref, ob_ref, s_s):
    @pl.when(pl.program_id(1) == 0)
    def _():
        s_s[...] = jnp.zeros_like(s_s)

    low, upp, heads, eye, masks = low_ref[...], upp_ref[...], heads_ref[...], eye_ref[...], masks_ref[...]
    dirs = [_GdDir(False, xf_ref, gf_ref, s_s.at[0], prm_lane_ref[0], prm_row_ref[0], low, upp,
                   ex_ref[0], ex_ref[1], sel_ref[0]),
            _GdDir(True, xb_ref, gb_ref, s_s.at[1], prm_lane_ref[1], prm_row_ref[1], low, upp,
                   ex_ref[2], ex_ref[3], sel_ref[2])]
    lane_head = lax.broadcasted_iota(jnp.int32, (GD_CHUNK, W_GROUP), 1) // GD_DH
    probs = [(d, h) for d in dirs for h in range(GD_HEADS)]
    only = lambda h, t: jnp.where(lane_head == h, t, 0.0).astype(BF16)
    hs = lambda h: slice(h * GD_DH, (h + 1) * GD_DH)
    decay = [jnp.exp(jnp.where(d.live, d.g[:, hs(h)] - d.g_rows[h:h + 1, :], NEG)) for d, h in probs]
    kks = [_dot_nt(only(h, d.kb), d.k_bf) for d, h in probs]
    ns = [jnp.where(d.strict, kk * dc, 0.0) for (d, h), kk, dc in zip(probs, kks, decay)]
    t_inv = [t.astype(BF16) for t in _unit_tri_inverses(ns, eye, masks)]
    us = [jnp.dot(t, only(h, d.v * d.beta), preferred_element_type=F32) for (d, h), t in zip(probs, t_inv)]
    ws = [jnp.dot(t, only(h, d.kb * d.eg), preferred_element_type=F32) for (d, h), t in zip(probs, t_inv)]
    qks = [(_dot_nt(only(h, d.q), d.k_bf) * dc).astype(BF16) for (d, h), dc in zip(probs, decay)]
    v_new, o = [], []
    for i, d in enumerate(dirs):
        sl = slice(i * GD_HEADS, (i + 1) * GD_HEADS)
        s_bf = d.s0.astype(BF16)
        v_new.append(sum(us[sl]) - jnp.dot(sum(ws[sl]).astype(BF16), s_bf, preferred_element_type=F32))
        o.append(jnp.dot((d.q * d.eg).astype(BF16), s_bf, preferred_element_type=F32))
    intra = [jnp.dot(qk, only(h, v_new[i // GD_HEADS]), preferred_element_type=F32)
             for i, ((d, h), qk) in enumerate(zip(probs, qks))]
    for i, (d, o_ref) in enumerate(zip(dirs, (of_ref, ob_ref))):
        o_ref[...] = o[i] + sum(intra[i * GD_HEADS:(i + 1) * GD_HEADS])
        k_dec = d.k * jnp.exp(d.g_last - d.g)
        s_s[i] = jnp.exp(d.g_last) * d.s0 + _dot_tn(k_dec.astype(BF16), v_new[i].astype(BF16)) * heads


def _gdn_scan(qkv, z_gt, a_log, dt_bias, ctx_len):
    n_batch, seq, _ = qkv.shape
    n_chunks = seq // GD_CHUNK
    n_ctx = ctx_len // GD_CHUNK
    fwd, bwd = _scan_chunks(n_chunks - n_ctx, n_ctx)
    g0 = ML_GATES
    low, upp, heads, expand, select = _chunk_consts(GD_CHUNK, GD_HEADS, (g0, g0 + 4, g0 + 8, g0 + 12))
    eye, masks = _gd_consts()
    prm = jnp.stack([-jnp.exp(a_log.astype(F32)), dt_bias.astype(F32)], 1)
    prm_lane = jnp.repeat(prm, GD_DH, axis=2)
    prm_row = jnp.broadcast_to(jnp.pad(prm, ((0, 0), (0, 0), (0, 8 - GD_HEADS)))[..., None],
                               (2, 2, 8, GD_CHUNK))
    full = lambda a: pl.BlockSpec(a.shape, lambda b, j: (0,) * a.ndim)
    return pl.pallas_call(
        _gdn_kernel,
        grid=(n_batch, n_chunks),
        in_specs=[pl.BlockSpec((None, GD_CHUNK, 3 * W_GROUP), lambda b, j: (b, fwd(j), 0)),
                  pl.BlockSpec((None, GD_CHUNK, 3 * W_GROUP), lambda b, j: (b, bwd(j), 0)),
                  pl.BlockSpec((None, GD_CHUNK, GATE_PAD), lambda b, j: (b, fwd(j), 0)),
                  pl.BlockSpec((None, GD_CHUNK, GATE_PAD), lambda b, j: (b, bwd(j), 0)),
                  full(prm_lane), full(prm_row), full(low), full(upp), full(heads), full(expand),
                  full(select), full(eye), full(masks)],
        out_specs=[pl.BlockSpec((None, GD_CHUNK, W_GROUP), lambda b, j: (b, fwd(j), 0)),
                   pl.BlockSpec((None, GD_CHUNK, W_GROUP), lambda b, j: (b, bwd(j), 0))],
        out_shape=[jax.ShapeDtypeStruct((n_batch, seq, W_GROUP), F32)] * 2,
        scratch_shapes=[pltpu.VMEM((2, W_GROUP, W_GROUP), F32)],
        compiler_params=pltpu.CompilerParams(
            dimension_semantics=("parallel", "arbitrary"), vmem_limit_bytes=VMEM_LIMIT),
        name="gdn_scan",
    )(qkv, qkv, z_gt, z_gt, prm_lane, prm_row, low, upp, heads, expand, select, eye, masks)


HY_FILT_TILE = 512
DFT_TILE = 512


DFT_ROWS = 64


def _dft_table_kernel(ac_ref, as_ref, bc_ref, bs_ref, re_ref, im_ref, imt_ref):
    a = pl.program_id(0)
    ac, sa = ac_ref[pl.ds(a, 1), :], as_ref[pl.ds(a, 1), :]
    bc, sb = bc_ref[...], bs_ref[...]
    cos = ac * bc - sa * sb
    nsin = -(sa * bc + ac * sb)
    i = lax.broadcasted_iota(jnp.int32, cos.shape, 0) + a * DFT_ROWS
    j = lax.broadcasted_iota(jnp.int32, cos.shape, 1)
    sign = lambda n: jnp.where(n % 2 == 0, 1.0, -1.0)
    re_ref[...] = cos.astype(BF16)
    im_ref[...] = jnp.where(i == 0, sign(j), nsin).astype(BF16)
    imt_ref[...] = jnp.where(j == 0, sign(i), nsin).astype(BF16)


def _dft_matrices(length):
    n = 2 * length
    j = jnp.arange(length, dtype=jnp.int32)[None, :]
    coarse = (jnp.arange(length // DFT_ROWS, dtype=jnp.int32)[:, None] * DFT_ROWS * j) % n
    fine = (jnp.arange(DFT_ROWS, dtype=jnp.int32)[:, None] * j) % n
    w = 2.0 * math.pi / n
    tabs = [jnp.cos(coarse.astype(F32) * w), jnp.sin(coarse.astype(F32) * w),
            jnp.cos(fine.astype(F32) * w), jnp.sin(fine.astype(F32) * w)]
    full = lambda a: pl.BlockSpec(a.shape, lambda i: (0, 0))
    out = pl.BlockSpec((DFT_ROWS, length), lambda i: (i, 0))
    return pl.pallas_call(
        _dft_table_kernel,
        grid=(length // DFT_ROWS,),
        in_specs=[full(t) for t in tabs],
        out_specs=[out] * 3,
        out_shape=[jax.ShapeDtypeStruct((length, length), BF16)] * 3,
        compiler_params=pltpu.CompilerParams(dimension_semantics=("parallel",)),
        name="dft_table",
    )(*tabs)


def _hy_filter_kernel(feat_ref, w1_ref, b1_ref, w2_ref, b2_ref, fr_ref, w3_ref, win_ref, filt_ref, ss_ref):
    i = pl.program_id(0)
    hdn = jnp.sin(fr_ref[0:1, :] * (_dot_hi(feat_ref[...], w1_ref[...]) + b1_ref[...]))
    hdn = jnp.sin(fr_ref[1:2, :] * (_dot_hi(hdn, w2_ref[...]) + b2_ref[...]))
    filt = _dot_hi(hdn, w3_ref[...]) * win_ref[...]
    half = filt.shape[1] // 2
    row = lax.broadcasted_iota(jnp.int32, filt.shape, 0) + i * filt.shape[0]
    col = lax.broadcasted_iota(jnp.int32, filt.shape, 1)
    filt = jnp.where(jnp.logical_and(row == 0, col >= half), 0.0, filt)
    filt_ref[...] = filt.astype(BF16)
    sq = jnp.sum(filt * filt, axis=0, keepdims=True)

    @pl.when(i == 0)
    def _():
        ss_ref[...] = jnp.zeros_like(ss_ref)

    ss_ref[...] += sq[:, :half] + sq[:, half:]


def _hyena_filters(length, w1, b1, w2, b2, freq, w3):
    tile = min(HY_FILT_TILE, length)
    t = jnp.linspace(0.0, 1.0, length, dtype=F32)[:, None]
    pos = jnp.arange(length, dtype=F32)[:, None]
    band = jnp.linspace(1e-4, HY_BANDS - 1, HY_BANDS, dtype=F32)[None, :]
    ang = 2.0 * math.pi * pos * band / length
    feats = jnp.concatenate([t, jnp.cos(ang), -jnp.sin(ang)], -1)
    deltas = jnp.abs(jnp.linspace(math.log(HY_TARGET) / HY_SLOW_DECAY,
                                  math.log(HY_TARGET) / HY_FAST_DECAY, W_GROUP, dtype=F32))
    window = jnp.tile(jnp.exp(-t * deltas), (1, 2 * HY_ORDER))
    n_out = w3.shape[1]
    emb, ffn = w1.shape
    full = lambda a: pl.BlockSpec(a.shape, lambda i: (0,) * a.ndim)
    args = [w1.astype(F32), b1.astype(F32).reshape(1, ffn), w2.astype(F32), b2.astype(F32).reshape(1, ffn),
            freq.astype(F32), w3.astype(F32)]
    filt, ss = pl.pallas_call(
        _hy_filter_kernel,
        grid=(length // tile,),
        in_specs=[pl.BlockSpec((tile, emb), lambda i: (i, 0))] + [full(a) for a in args] +
                 [pl.BlockSpec((tile, n_out), lambda i: (i, 0))],
        out_specs=[pl.BlockSpec((tile, n_out), lambda i: (i, 0)),
                   pl.BlockSpec((1, n_out // 2), lambda i: (0, 0))],
        out_shape=[jax.ShapeDtypeStruct((length, n_out), BF16),
                   jax.ShapeDtypeStruct((1, n_out // 2), F32)],
        compiler_params=pltpu.CompilerParams(
            dimension_semantics=("arbitrary",), vmem_limit_bytes=VMEM_LIMIT),
        name="hyena_filter",
    )(feats, *args, window)
    return filt, ss


def _spec_mul(yr, yi, kr, ki, first):
    row0 = jnp.logical_and(first, lax.broadcasted_iota(jnp.int32, yr.shape, 0) == 0)
    zr = yr * kr - jnp.where(row0, 0.0, yi * ki)
    zi = jnp.where(row0, yi * ki, yr * ki + yi * kr)
    return zr, zi


def _dft_filter_kernel(wre_ref, wim_ref, x_ref, ss_ref, kre_ref, kim_ref):
    x = x_ref[...]
    yr = jnp.dot(wre_ref[...], x, preferred_element_type=F32)
    yi = jnp.dot(wim_ref[...], x, preferred_element_type=F32)
    half = yr.shape[1] // 2
    row0 = jnp.logical_and(pl.program_id(0) == 0,
                           lax.broadcasted_iota(jnp.int32, (yr.shape[0], half), 0) == 0)
    n = 2 * x.shape[0]
    scale = lax.rsqrt(ss_ref[...] + EPS) * jnp.where(row0, 1.0 / n, 2.0 / n)
    kre_ref[...] = (yr[:, :half] + yr[:, half:]) * scale
    kim_ref[...] = (yi[:, :half] + jnp.where(row0, yi[:, half:], -yi[:, half:])) * scale


def _dft_filter(f_re, f_im, filt, ss):
    length, n = filt.shape
    tile = min(DFT_TILE, length)
    wspec = pl.BlockSpec((tile, length), lambda i: (i, 0))
    return pl.pallas_call(
        _dft_filter_kernel,
        grid=(length // tile,),
        in_specs=[wspec, wspec, pl.BlockSpec((length, n), lambda i: (0, 0)),
                  pl.BlockSpec((1, n // 2), lambda i: (0, 0))],
        out_specs=[pl.BlockSpec((tile, n // 2), lambda i: (i, 0))] * 2,
        out_shape=[jax.ShapeDtypeStruct((length, n // 2), F32)] * 2,
        compiler_params=pltpu.CompilerParams(
            dimension_semantics=("parallel",), vmem_limit_bytes=VMEM_LIMIT),
        name="hyena_filter_dft",
    )(f_re, f_im, filt, ss)


def _dft_fwd_kernel(wre_ref, wim_ref, y_ref, kre_ref, kim_ref, zre_ref, zim_ref):
    y = y_ref[...].astype(BF16)
    yr = jnp.dot(wre_ref[...], y, preferred_element_type=F32)
    yi = jnp.dot(wim_ref[...], y, preferred_element_type=F32)
    zr, zi = _spec_mul(yr, yi, kre_ref[...], kim_ref[...], pl.program_id(0) == 0)
    zre_ref[...] = zr.astype(BF16)
    zim_ref[...] = zi.astype(BF16)


def _dft_fwd(f_re, f_im, y, k_re, k_im, order, row0):
    n_batch = y.shape[0]
    length = f_re.shape[0]
    tile = min(DFT_TILE, length)
    wspec = pl.BlockSpec((tile, length), lambda i, b: (i, 0))
    kspec = pl.BlockSpec((tile, W_GROUP), lambda i, b: (i, order))
    zspec = pl.BlockSpec((None, tile, W_GROUP), lambda i, b: (b, i, 0))
    return pl.pallas_call(
        _dft_fwd_kernel,
        grid=(length // tile, n_batch),
        in_specs=[wspec, wspec,
                  pl.BlockSpec((None, length, W_GROUP), lambda i, b: (b, row0 // length, 0)),
                  kspec, kspec],
        out_specs=[zspec, zspec],
        out_shape=[jax.ShapeDtypeStruct((n_batch, length, W_GROUP), BF16)] * 2,
        compiler_params=pltpu.CompilerParams(
            dimension_semantics=("parallel", "arbitrary"), vmem_limit_bytes=VMEM_LIMIT),
        name="hyena_dft_fwd",
    )(f_re, f_im, y, k_re, k_im)


def _dft_inv_kernel(wre_ref, wim_ref, zre_ref, zim_ref, y_ref, gate_ref, bias_ref, o_ref, obf_ref):
    conv = (jnp.dot(wre_ref[...], zre_ref[...], preferred_element_type=F32) +
            jnp.dot(wim_ref[...], zim_ref[...], preferred_element_type=F32))
    out = gate_ref[...] * (conv + bias_ref[...] * y_ref[...])
    o_ref[...] = out
    obf_ref[...] = out.astype(BF16)


def _dft_inv(i_re, i_im, z_re, z_im, y_src, gate_src, bias):
    n_batch, length, _ = z_re.shape
    tile = min(DFT_TILE, length)
    wspec = pl.BlockSpec((tile, length), lambda i, b: (i, 0))
    zspec = pl.BlockSpec((None, length, W_GROUP), lambda i, b: (b, 0, 0))
    ospec = pl.BlockSpec((None, tile, W_GROUP), lambda i, b: (b, i, 0))
    view = lambda src: pl.BlockSpec((None, tile, W_GROUP), lambda i, b: (b, src[1] // tile + i, src[2]))
    return pl.pallas_call(
        _dft_inv_kernel,
        grid=(length // tile, n_batch),
        in_specs=[wspec, wspec, zspec, zspec, view(y_src), view(gate_src),
                  pl.BlockSpec((1, W_GROUP), lambda i, b: (0, 0))],
        out_specs=[ospec, ospec],
        out_shape=[jax.ShapeDtypeStruct((n_batch, length, W_GROUP), F32),
                   jax.ShapeDtypeStruct((n_batch, length, W_GROUP), BF16)],
        compiler_params=pltpu.CompilerParams(
            dimension_semantics=("parallel", "arbitrary"), vmem_limit_bytes=VMEM_LIMIT),
        name="hyena_dft_inv",
    )(i_re, i_im, z_re, z_im, y_src[0], gate_src[0], bias.astype(F32).reshape(1, W_GROUP))


def _hyena_segment(u, row0, length, dft, filt_params, bias):
    f_re, f_im, i_im = dft
    i_re = f_re
    filt, ss = _hyena_filters(length, *filt_params)
    k_re, k_im = _dft_filter(f_re, f_im, filt, ss)
    z_re, z_im = _dft_fwd(f_re, f_im, u, k_re, k_im, 0, row0)
    y, y_bf = _dft_inv(i_re, i_im, z_re, z_im, (u, row0, 0), (u, row0, 1), bias[0])
    z_re, z_im = _dft_fwd(f_re, f_im, y_bf, k_re, k_im, 1, 0)
    y, _ = _dft_inv(i_re, i_im, z_re, z_im, (y, 0, 0), (u, row0, 2), bias[1])
    return y


GT_COLS = 4


def _grid_transpose_kernel(x_ref, c_ref, o_ref, *, n_major, n_lat_steps):
    j = pl.program_id(1)
    for half in range(2):
        @pl.when(jnp.logical_and(j < n_lat_steps, j % 2 == half))
        def _():
            for i in range(GT_COLS):
                o_ref[i * n_major:(i + 1) * n_major, :] = x_ref[:, half * GT_COLS + i, :]

    @pl.when(j >= n_lat_steps)
    def _():
        o_ref[...] = c_ref[...]


def _grid_transpose(t, n_lat, n_major):
    n_batch, seq, w = t.shape
    n_minor = n_lat // n_major
    rows = GT_COLS * n_major
    n_lat_steps = n_minor // GT_COLS
    view = t.reshape(n_batch, seq // n_minor, n_minor, w)
    return pl.pallas_call(
        functools.partial(_grid_transpose_kernel, n_major=n_major, n_lat_steps=n_lat_steps),
        grid=(n_batch, seq // rows),
        in_specs=[pl.BlockSpec((None, n_major, 2 * GT_COLS, w),
                               lambda b, j: (b, 0, jnp.minimum(j, n_lat_steps - 1) // 2, 0)),
                  pl.BlockSpec((None, rows, w), lambda b, j: (b, jnp.maximum(j, n_lat_steps), 0))],
        out_specs=pl.BlockSpec((None, rows, w), lambda b, j: (b, j, 0)),
        out_shape=jax.ShapeDtypeStruct((n_batch, seq, w), t.dtype),
        compiler_params=pltpu.CompilerParams(
            dimension_semantics=("parallel", "arbitrary"), vmem_limit_bytes=VMEM_LIMIT),
        name="grid_transpose",
    )(view, t)


def _permute_w_in(w_in):
    o_ml = HG_COLS
    o_hy = o_ml + ML_MAIN + ML_GATES
    o_gd = o_hy + HY_COLS
    parts = [w_in[:, :HG_COLS], w_in[:, o_ml:o_ml + ML_MAIN], w_in[:, o_hy:o_gd],
             w_in[:, o_gd:o_gd + GD_MAIN], w_in[:, o_ml + ML_MAIN:o_hy],
             w_in[:, o_gd + GD_MAIN:],
             jnp.zeros((w_in.shape[0], GATE_PAD - ML_GATES - GD_GATES), w_in.dtype)]
    return jnp.concatenate(parts, 1).astype(BF16)


IN_SPLITS = (HG_COLS, ML_MAIN, HY_COLS, GD_MAIN, GATE_PAD)


def kernel(x, c, ctx, c_ctx, mod_w, mod_b, norm1_g, norm2_g, w_in, w_out, hg_lb_logits, hg_norm_g,
           ml_gate_b, ml_norm_g, hy_conv_w, hy_conv_b, hy_w1, hy_b1, hy_w2, hy_b2, hy_freq, hy_w3,
           hy_bias, gd_conv_w, gd_a_log, gd_dt_bias, gd_norm_g, router_w, router_b, exp_w1, exp_b1,
           exp_w2, exp_b2, final_g):
    n_batch, seq, d = x.shape
    ctx_len = ctx.shape[1]
    depth = mod_w.shape[0]
    rows = seq // GRID_W
    p = jax.nn.softmax(hg_lb_logits.astype(F32), axis=0)
    lower_bounds = jnp.cumsum(p, 0) - p[0]
    s_all = jnp.concatenate([jax.nn.silu(c), jax.nn.silu(c_ctx)[None]], 0)
    xs = jnp.concatenate([x, ctx], 1)
    colmajor = lambda t: _grid_transpose(t, seq, rows)
    rowmajor = lambda t: _grid_transpose(t, seq, GRID_W)
    dft_lat, dft_ctx = _dft_matrices(seq), _dft_matrices(ctx_len)
    for l in range(depth):
        mod = (s_all @ mod_w[l] + mod_b[l]).reshape(n_batch + 1, 6, 1, d)
        sh1, sc1, g1, sh2, sc2, g2 = (mod[:, i] for i in range(6))
        z_hg, z_ml, z_hy, z_gd, z_gt = _in_proj(xs, norm1_g[l], sc1, sh1,
                                                _permute_w_in(w_in[l]), IN_SPLITS)
        hg_f, hg_b = _hgrn2_scan(z_hg, lower_bounds[l])
        ml_f, ml_b = _mlstm_scan(colmajor(z_ml), colmajor(z_gt), ml_gate_b[l], ctx_len)
        filt = (hy_w1[l], hy_b1[l], hy_w2[l], hy_b2[l], hy_freq[l], hy_w3[l])
        u_hy = _short_conv_tiles(z_hy, hy_conv_w[l], hy_conv_b[l], seq, False)
        c_lat = _hyena_segment(u_hy, 0, seq, dft_lat, filt, hy_bias[l])
        if l == depth - 1:
            c_ctx = jnp.zeros((n_batch, ctx_len, W_GROUP), F32)
        else:
            c_ctx = _hyena_segment(u_hy, seq, ctx_len, dft_ctx, filt, hy_bias[l])
        qkv = _short_conv_tiles(z_gd, gd_conv_w[l], jnp.zeros((3 * W_GROUP,), F32), seq, True)
        gd_f, gd_b = _gdn_scan(qkv, z_gt, gd_a_log[l], gd_dt_bias[l], ctx_len)
        mixers = [("silu", hg_f, hg_b, (z_hg, 4)),
                  ("sigmoid", rowmajor(ml_f), rowmajor(ml_b), (z_ml, 3)),
                  ("final", jnp.concatenate([c_lat, c_ctx], 1)),
                  ("silu", gd_f, gd_b, (z_gd, 3))]
        gains = jnp.stack([hg_norm_g[l], ml_norm_g[l], jnp.ones_like(ml_norm_g[l]), gd_norm_g[l]]).astype(F32)
        xs = _out_proj(xs, mixers, gains, g1, w_out[l].astype(BF16))
        last = l == depth - 1
        xs = _moe_layer(xs, norm2_g[l], sc2, sh2, g2, router_w[l], router_b[l],
                        l, exp_w1, exp_b1[l], exp_w2, exp_b2[l],
                        final_gain=final_g if last else None, n_out=seq if last else None)
    return xs
```
